```python
import math
import jax, jax.numpy as jnp
from jax import lax
import numpy as np

D_MODEL = 1024
BATCH = 16
SEQ = 2048
DEPTH = 4

D_FF = 2816
FFN_RES_WEIGHT = 0.5
CONV_A_CH = 512
CONV_A_WIDTH = 31
CONV_B_CH = 512
CONV_B_WIDTH = 3
NSA_HEADS = 16
NSA_KV_GROUPS = 1
NSA_HEADS_PER_GROUP = NSA_HEADS // NSA_KV_GROUPS
HEAD_DIM = 64
CMP_BLOCK = 32
CMP_STRIDE = 16
CMP_HIDDEN = 128
SLC_BLOCK = 64
SLC_TOPK = 16
WINDOW = 512
Q_BLOCK = 128
POOL_WINDOWS = (2, 4, 8, 16)
POOL_GROUP_CH = 64
POOL_CH = len(POOL_WINDOWS) * POOL_GROUP_CH
REL_BUCKETS = 32
REL_MAX_DIST = 128
DN_ALPHA = (2 * DEPTH) ** 0.25
DN_BETA = (8 * DEPTH) ** -0.25

N_EVEN = (DEPTH + 1) // 2
N_ODD = DEPTH // 2
EVEN_IN = 2 * CONV_A_CH + 3 * CONV_B_CH
EVEN_OUT = CONV_A_CH + CONV_B_CH
NSA_Q = NSA_HEADS * HEAD_DIM
KV_W = NSA_KV_GROUPS * HEAD_DIM
NSA_GATES = 3 * NSA_HEADS
ODD_IN = NSA_Q + 6 * KV_W + NSA_GATES + POOL_CH
ODD_OUT = NSA_Q + POOL_CH
NEG_INF = -1e30
FORCE = 1e30

kernel_name = 'hybrid_conv_nsa_pool_macaron'


def layer_norm(x, g, b, eps=1e-5):
    xf = x.astype(jnp.float32)
    mu = xf.mean(-1, keepdims=True)
    var = jnp.square(xf - mu).mean(-1, keepdims=True)
    y = (xf - mu) * lax.rsqrt(var + eps)
    return (y * g.astype(jnp.float32) + b.astype(jnp.float32)).astype(x.dtype)


def swiglu(h, w_gate, w_up, w_down):
    return (jax.nn.silu(h @ w_gate) * (h @ w_up)) @ w_down


def adaln(x, m):
    return x * (1 + m[:, 1]) + m[:, 0]


def deepnorm_update(x, y, m, res_w, g, b):
    return layer_norm(DN_ALPHA * x + res_w * (1 + m[:, 2]) * y, g, b)


def causal_dwconv(x, w):
    k, ch = w.shape
    return lax.conv_general_dilated(x, w[:, None, :].astype(x.dtype), (1,), [(k - 1, 0)],
                                    dimension_numbers=('NWC', 'WIO', 'NWC'),
                                    feature_group_count=ch)


def t5_bucket(dist):
    n = jnp.maximum(dist, 0)
    exact = REL_BUCKETS // 2
    nf = jnp.maximum(n, 1).astype(jnp.float32)
    large = exact + (jnp.log(nf / exact) / math.log(REL_MAX_DIST / exact)
                     * (REL_BUCKETS - exact)).astype(jnp.int32)
    return jnp.where(n < exact, n, jnp.minimum(large, REL_BUCKETS - 1))


def masked_softmax(logits, valid):
    p = jax.nn.softmax(jnp.where(valid, logits, NEG_INF), axis=-1)
    return p * jnp.any(valid, axis=-1, keepdims=True)


def even_mixer(h, w_in, conv_a_w, conv_a_b, norm_a_g, norm_a_b, conv_b_w, w_out):
    p = h @ w_in
    a_val, a_gate, gate_b, gate_c, b_in = jnp.split(
        p, [CONV_A_CH, 2 * CONV_A_CH, 2 * CONV_A_CH + CONV_B_CH, 2 * CONV_A_CH + 2 * CONV_B_CH], axis=-1)
    u = a_val * jax.nn.sigmoid(a_gate)
    u = causal_dwconv(u, conv_a_w) + conv_a_b
    u = jax.nn.silu(layer_norm(u, norm_a_g, norm_a_b))
    z = gate_b * causal_dwconv(gate_c * b_in, conv_b_w)
    return jnp.concatenate([u, z], axis=-1) @ w_out


def pool_mixer(u, pool_w, pool_scale):
    bsz, s, _ = u.shape
    uf = u.astype(jnp.float32).reshape(bsz, s, len(POOL_WINDOWS), POOL_GROUP_CH)
    cs = jnp.pad(jnp.cumsum(uf, axis=1), ((0, 0), (1, 0), (0, 0), (0, 0)))
    t = jnp.arange(s)
    diffs = []
    for gi, w in enumerate(POOL_WINDOWS):
        lo = jnp.maximum(t + 1 - w, 0)
        cnt = (t + 1 - lo).astype(jnp.float32)
        cs_g = cs[:, :, gi]
        mean = (cs_g[:, t + 1] - cs_g[:, lo]) / cnt[None, :, None]
        diffs.append(mean - uf[:, :, gi])
    d = jnp.stack(diffs, axis=2).astype(u.dtype)
    y = jnp.einsum('bsgc,gcd->bsgd', d, pool_w).reshape(bsz, s, POOL_CH)
    return y * pool_scale


def nsa_attention(q, gates, k_cmp, v_cmp, k_slc, v_slc, k_win, v_win,
                  pe_k, pe_v, w1_k, w2_k, w1_v, w2_v, rel_bias):
    bsz, s = q.shape[:2]
    G, HG, Dh = NSA_KV_GROUPS, NSA_HEADS_PER_GROUP, HEAD_DIM
    dt = q.dtype
    scale = Dh ** -0.5
    tb = rel_bias.astype(jnp.float32).reshape(REL_BUCKETS, G, HG)

    n_cmp = (s - CMP_BLOCK) // CMP_STRIDE + 1
    cmp_start = np.arange(n_cmp) * CMP_STRIDE
    tok_idx = cmp_start[:, None] + np.arange(CMP_BLOCK)[None, :]

    def compress(kv, pe, w1, w2):
        blk = kv[:, tok_idx] + pe[None, None, :, None, :]
        blk = jnp.moveaxis(blk, 3, 2).reshape(bsz, n_cmp, G, CMP_BLOCK * Dh)
        return jax.nn.silu(blk @ w1) @ w2

    kc = compress(k_cmp, pe_k, w1_k, w2_k)
    vc = compress(v_cmp, pe_v, w1_v, w2_v)
    cmp_end = jnp.asarray(cmp_start + CMP_BLOCK - 1, jnp.int32)

    n_slc = s // SLC_BLOCK
    slc_start = np.arange(n_slc) * SLC_BLOCK
    ov = np.clip(np.minimum(cmp_start[:, None] + CMP_BLOCK, slc_start[None, :] + SLC_BLOCK)
                 - np.maximum(cmp_start[:, None], slc_start[None, :]), 0, None) / CMP_BLOCK
    overlap = jnp.asarray(ov, jnp.float32)
    n_top = min(SLC_TOPK, n_slc)
    ks_blk = jnp.moveaxis(k_slc.reshape(bsz, n_slc, SLC_BLOCK, G, Dh), 3, 1)
    vs_blk = jnp.moveaxis(v_slc.reshape(bsz, n_slc, SLC_BLOCK, G, Dh), 3, 1)

    kw_pad = jnp.pad(k_win, ((0, 0), (WINDOW, 0), (0, 0), (0, 0)))
    vw_pad = jnp.pad(v_win, ((0, 0), (WINDOW, 0), (0, 0), (0, 0)))
    n_win = WINDOW + Q_BLOCK

    n_qb = s // Q_BLOCK
    qb = q.reshape(bsz, n_qb, Q_BLOCK, G, HG, Dh).transpose(1, 0, 3, 4, 2, 5)
    gb = gates.reshape(bsz, n_qb, Q_BLOCK, G, HG, 3).transpose(1, 0, 3, 4, 2, 5)
    bidx = jnp.arange(bsz)[:, None, None, None]
    gidx = jnp.arange(G)[None, :, None, None]
    blk_ids = jnp.arange(n_slc, dtype=jnp.int32)

    def head_bias(dist):
        return tb[t5_bucket(dist)].transpose(2, 3, 0, 1)

    def query_block(args):
        i, q_i, g_i = args
        t0 = i * Q_BLOCK
        tq = t0 + jnp.arange(Q_BLOCK, dtype=jnp.int32)
        valid_c = cmp_end[None, :] <= tq[:, None]
        l_c = (jnp.einsum('bghtd,bngd->bghtn', q_i, kc).astype(jnp.float32) * scale
               + head_bias(tq[:, None] - cmp_end[None, :]))
        p_c = masked_softmax(l_c, valid_c)
        o_c = jnp.einsum('bghtn,bngd->bghtd', p_c.astype(dt), vc)
        imp = jnp.einsum('bghtn,ns->bgts', p_c, overlap)
        cur = tq // SLC_BLOCK
        forced = ((blk_ids[None, :] == 0) | (blk_ids[None, :] == cur[:, None])
                  | (blk_ids[None, :] == cur[:, None] - 1))
        causal = blk_ids[None, :] <= cur[:, None]
        imp = jnp.where(forced, FORCE, jnp.where(causal, imp, -FORCE))
        _, sel = lax.top_k(imp, n_top)
        k_s = ks_blk[bidx, gidx, sel].reshape(bsz, G, Q_BLOCK, n_top * SLC_BLOCK, Dh)
        v_s = vs_blk[bidx, gidx, sel].reshape(bsz, G, Q_BLOCK, n_top * SLC_BLOCK, Dh)
        pos = (sel[..., None] * SLC_BLOCK + jnp.arange(SLC_BLOCK, dtype=jnp.int32)).reshape(
            bsz, G, Q_BLOCK, n_top * SLC_BLOCK)
        dist_s = tq[None, None, :, None] - pos
        bias_s = jnp.moveaxis(tb[t5_bucket(dist_s), gidx], -1, 2)
        l_s = jnp.einsum('bghtd,bgtkd->bghtk', q_i, k_s).astype(jnp.float32) * scale + bias_s
        p_s = masked_softmax(l_s, (dist_s >= 0)[:, :, None])
        o_s = jnp.einsum('bghtk,bgtkd->bghtd', p_s.astype(dt), v_s)
        k_w = lax.dynamic_slice_in_dim(kw_pad, t0, n_win, axis=1)
        v_w = lax.dynamic_slice_in_dim(vw_pad, t0, n_win, axis=1)
        kpos = t0 - WINDOW + jnp.arange(n_win, dtype=jnp.int32)
        dist_w = tq[:, None] - kpos[None, :]
        valid_w = (dist_w >= 0) & (dist_w < WINDOW) & (kpos[None, :] >= 0)
        l_w = (jnp.einsum('bghtd,blgd->bghtl', q_i, k_w).astype(jnp.float32) * scale
               + head_bias(dist_w))
        p_w = masked_softmax(l_w, valid_w)
        o_w = jnp.einsum('bghtl,blgd->bghtd', p_w.astype(dt), v_w)
        return g_i[..., 0:1] * o_c + g_i[..., 1:2] * o_s + g_i[..., 2:3] * o_w

    out = lax.map(query_block, (jnp.arange(n_qb, dtype=jnp.int32), qb, gb))
    return out.transpose(1, 0, 4, 2, 3, 5).reshape(bsz, s, NSA_HEADS * Dh)


def odd_mixer(h, w_in, pe_k, pe_v, w1_k, w2_k, w1_v, w2_v, pool_w, pool_scale, w_out, rel_bias):
    bsz, s, _ = h.shape
    p = h @ w_in
    splits = [int(v) for v in np.cumsum([NSA_Q] + [KV_W] * 6 + [NSA_GATES])]
    q, kc, vc, ks, vs, kw, vw, g, u = jnp.split(p, splits, axis=-1)
    kvs = (bsz, s, NSA_KV_GROUPS, HEAD_DIM)
    o_nsa = nsa_attention(q.reshape(bsz, s, NSA_HEADS, HEAD_DIM),
                          jax.nn.sigmoid(g).reshape(bsz, s, NSA_HEADS, 3),
                          kc.reshape(kvs), vc.reshape(kvs), ks.reshape(kvs), vs.reshape(kvs),
                          kw.reshape(kvs), vw.reshape(kvs),
                          pe_k, pe_v, w1_k, w2_k, w1_v, w2_v, rel_bias)
    o_pool = pool_mixer(u, pool_w, pool_scale)
    return jnp.concatenate([o_nsa, o_pool], axis=-1) @ w_out


def setup_inputs(seed: int = 0) -> dict:
    key = jax.random.key(seed)
    keys = iter(jax.random.split(key, 40))

    def nrm(shape, scale):
        return jax.random.normal(next(keys), shape, jnp.float32) * scale

    D = D_MODEL
    return {
        'x': nrm((BATCH, SEQ, D), 1.0),
        'c': nrm((BATCH, D), 1.0),
        'ada_w': nrm((DEPTH, D, 9 * D), 0.2 * D ** -0.5),
        'ada_b': nrm((DEPTH, 9 * D), 0.02),
        'ln_g': 1.0 + nrm((DEPTH, 3, D), 0.02),
        'ln_b': nrm((DEPTH, 3, D), 0.02),
        'ffn_w_gate': nrm((DEPTH, 2, D, D_FF), D ** -0.5),
        'ffn_w_up': nrm((DEPTH, 2, D, D_FF), D ** -0.5),
        'ffn_w_down': nrm((DEPTH, 2, D_FF, D), DN_BETA * D_FF ** -0.5),
        'ev_w_in': nrm((N_EVEN, D, EVEN_IN), D ** -0.5),
        'ev_conv_a_w': nrm((N_EVEN, CONV_A_WIDTH, CONV_A_CH), CONV_A_WIDTH ** -0.5),
        'ev_conv_a_b': nrm((N_EVEN, CONV_A_CH), 0.02),
        'ev_norm_a_g': 1.0 + nrm((N_EVEN, CONV_A_CH), 0.02),
        'ev_norm_a_b': nrm((N_EVEN, CONV_A_CH), 0.02),
        'ev_conv_b_w': nrm((N_EVEN, CONV_B_WIDTH, CONV_B_CH), CONV_B_WIDTH ** -0.5),
        'ev_w_out': nrm((N_EVEN, EVEN_OUT, D), DN_BETA * EVEN_OUT ** -0.5),
        'od_w_in': nrm((N_ODD, D, ODD_IN), D ** -0.5),
        'od_cmp_pe_k': nrm((N_ODD, CMP_BLOCK, HEAD_DIM), 0.1),
        'od_cmp_pe_v': nrm((N_ODD, CMP_BLOCK, HEAD_DIM), 0.1),
        'od_cmp_w1_k': nrm((N_ODD, CMP_BLOCK * HEAD_DIM, CMP_HIDDEN), (CMP_BLOCK * HEAD_DIM) ** -0.5),
        'od_cmp_w2_k': nrm((N_ODD, CMP_HIDDEN, HEAD_DIM), CMP_HIDDEN ** -0.5),
        'od_cmp_w1_v': nrm((N_ODD, CMP_BLOCK * HEAD_DIM, CMP_HIDDEN), (CMP_BLOCK * HEAD_DIM) ** -0.5),
        'od_cmp_w2_v': nrm((N_ODD, CMP_HIDDEN, HEAD_DIM), CMP_HIDDEN ** -0.5),
        'od_pool_w': nrm((N_ODD, len(POOL_WINDOWS), POOL_GROUP_CH, POOL_GROUP_CH), POOL_GROUP_CH ** -0.5),
        'od_pool_scale': 1.0 + nrm((N_ODD, POOL_CH), 0.1),
        'od_w_out': nrm((N_ODD, ODD_OUT, D), DN_BETA * ODD_OUT ** -0.5),
        'rel_bias': nrm((REL_BUCKETS, NSA_HEADS), 0.5),
    }


def reference(x, c, ada_w, ada_b, ln_g, ln_b, ffn_w_gate, ffn_w_up, ffn_w_down,
              ev_w_in, ev_conv_a_w, ev_conv_a_b, ev_norm_a_g, ev_norm_a_b, ev_conv_b_w, ev_w_out,
              od_w_in, od_cmp_pe_k, od_cmp_pe_v, od_cmp_w1_k, od_cmp_w2_k, od_cmp_w1_v, od_cmp_w2_v,
              od_pool_w, od_pool_scale, od_w_out, rel_bias):
    bsz = x.shape[0]
    cond = jax.nn.silu(c)
    for layer in range(DEPTH):
        mod = (cond @ ada_w[layer] + ada_b[layer]).reshape(bsz, 3, 3, 1, D_MODEL)
        y = swiglu(adaln(x, mod[:, 0]), ffn_w_gate[layer, 0], ffn_w_up[layer, 0], ffn_w_down[layer, 0])
        x = deepnorm_update(x, y, mod[:, 0], FFN_RES_WEIGHT, ln_g[layer, 0], ln_b[layer, 0])
        h = adaln(x, mod[:, 1])
        j = layer // 2
        if layer % 2 == 0:
            y = even_mixer(h, ev_w_in[j], ev_conv_a_w[j], ev_conv_a_b[j], ev_norm_a_g[j],
                           ev_norm_a_b[j], ev_conv_b_w[j], ev_w_out[j])
        else:
            y = odd_mixer(h, od_w_in[j], od_cmp_pe_k[j], od_cmp_pe_v[j], od_cmp_w1_k[j], od_cmp_w2_k[j],
                          od_cmp_w1_v[j], od_cmp_w2_v[j], od_pool_w[j], od_pool_scale[j], od_w_out[j],
                          rel_bias)
        x = deepnorm_update(x, y, mod[:, 1], 1.0, ln_g[layer, 1], ln_b[layer, 1])
        y = swiglu(adaln(x, mod[:, 2]), ffn_w_gate[layer, 1], ffn_w_up[layer, 1], ffn_w_down[layer, 1])
        x = deepnorm_update(x, y, mod[:, 2], FFN_RES_WEIGHT, ln_g[layer, 2], ln_b[layer, 2])
    return x
```

```python
import functools
import math

import numpy as np
import jax
import jax.numpy as jnp
from jax import lax
from jax.experimental import pallas as pl
from jax.experimental.pallas import tpu as pltpu

F32 = jnp.float32
BF16 = jnp.bfloat16

D_MODEL = 1024
DEPTH = 4
D_FF = 2816
FFN_RES_WEIGHT = 0.5
CONV_A_CH = 512
CONV_A_WIDTH = 31
CONV_B_CH = 512
CONV_B_WIDTH = 3
NSA_HEADS = 16
HEAD_DIM = 64
CMP_BLOCK = 32
CMP_STRIDE = 16
CMP_HIDDEN = 128
SLC_BLOCK = 64
SLC_TOPK = 16
WINDOW = 512
POOL_WINDOWS = (2, 4, 8, 16)
POOL_GROUP_CH = 64
POOL_CH = len(POOL_WINDOWS) * POOL_GROUP_CH
REL_BUCKETS = 32
REL_MAX_DIST = 128
DN_ALPHA = (2 * DEPTH) ** 0.25
LN_EPS = 1e-5
NSA_Q = NSA_HEADS * HEAD_DIM
NEG_INF = -1e30
FORCE = 1e30

LANES = 128
VMEM_LIMIT_BYTES = 56 * 1024 * 1024

FFN_ROWS = 512
MIX_ROWS = 512
ATT_Q = 256
ATT_K = 256
CONV_HALO = 32
CONV_B_HALO = 8
POOL_HALO = 16
KV_PAD = 512


def _cparams(sem):
    return pltpu.CompilerParams(dimension_semantics=sem, vmem_limit_bytes=VMEM_LIMIT_BYTES)


def _const_spec(shape):
    n = len(shape)
    return pl.BlockSpec(shape, lambda *_: (0,) * n, pipeline_mode=pl.Buffered(1))


def _layer_norm(z, g, b):
    mu = jnp.mean(z, axis=-1, keepdims=True)
    zc = z - mu
    var = jnp.mean(zc * zc, axis=-1, keepdims=True)
    return zc * lax.rsqrt(var + LN_EPS) * g + b


def _silu(v):
    return v * jax.nn.sigmoid(v)


def _dot(a, b):
    return jnp.dot(a, b, preferred_element_type=F32)


def _dot_nt(a, b):
    return lax.dot_general(a, b, (((1,), (1,)), ((), ())), preferred_element_type=F32)


def _mod_kernel(c_ref, w_ref, b_ref, o_ref):
    cond = _silu(c_ref[...])
    o_ref[0] = _dot(cond.astype(BF16), w_ref[0].astype(BF16)) + b_ref[0]


def _modulation(c, ada_w, ada_b):
    depth, d, n = ada_w.shape
    bsz = c.shape[0]
    tn = 1152
    return pl.pallas_call(
        _mod_kernel,
        grid=(depth, n // tn),
        in_specs=[
            pl.BlockSpec((bsz, d), lambda l, j: (0, 0)),
            pl.BlockSpec((1, d, tn), lambda l, j: (l, 0, j)),
            pl.BlockSpec((1, 1, tn), lambda l, j: (l, 0, j)),
        ],
        out_specs=pl.BlockSpec((1, bsz, tn), lambda l, j: (l, 0, j)),
        out_shape=jax.ShapeDtypeStruct((depth, bsz, n), F32),
        compiler_params=_cparams(("arbitrary", "arbitrary")),
        name="modulation",
    )(c, ada_w, ada_b.reshape(depth, 1, n))


def _ffn_kernel(x_ref, mod_ref, wg_ref, wu_ref, wd_ref, g_ref, b_ref, o_ref):
    x = x_ref[0]
    mod = mod_ref[0, 0, 0]
    h = (x * (1.0 + mod[1:2]) + mod[0:1]).astype(BF16)
    a = (_silu(_dot(h, wg_ref[...])) * _dot(h, wu_ref[...])).astype(BF16)
    y = _dot(a, wd_ref[...])
    z = DN_ALPHA * x + (FFN_RES_WEIGHT * (1.0 + mod[2:3])) * y
    o_ref[0] = _layer_norm(z, g_ref[...], b_ref[...])


def _ffn(x, mod, layer, sub, wg, wu, wd, g, b):
    bsz, s, d = x.shape
    ff = wg.shape[1]
    tm = min(FFN_ROWS, s)
    return pl.pallas_call(
        _ffn_kernel,
        grid=(bsz, s // tm),
        in_specs=[
            pl.BlockSpec((1, tm, d), lambda i, t: (i, t, 0)),
            pl.BlockSpec((1, 1, 1, 3, d), lambda i, t: (layer, i, sub, 0, 0)),
            _const_spec((d, ff)),
            _const_spec((d, ff)),
            _const_spec((ff, d)),
            _const_spec((1, d)),
            _const_spec((1, d)),
        ],
        out_specs=pl.BlockSpec((1, tm, d), lambda i, t: (i, t, 0)),
        out_shape=jax.ShapeDtypeStruct(x.shape, F32),
        compiler_params=_cparams(("arbitrary", "arbitrary")),
        name="ffn",
    )(x, mod, wg, wu, wd, g.reshape(1, d), b.reshape(1, d))


def _even_kernel(x_ref, mod_ref, win_ref, caw_ref, cab_ref, nag_ref, nab_ref, cbw_ref, wout_ref,
                 g_ref, b_ref, o_ref, abuf, bbuf, cat):
    tm = x_ref.shape[1]
    ca, cb = CONV_A_CH, CONV_B_CH

    @pl.when(pl.program_id(1) == 0)
    def _():
        abuf[0:CONV_HALO, :] = jnp.zeros((CONV_HALO, ca), F32)
        bbuf[0:CONV_B_HALO, :] = jnp.zeros((CONV_B_HALO, cb), F32)

    x = x_ref[0]
    mod = mod_ref[0, 0, 0]
    h = (x * (1.0 + mod[1:2]) + mod[0:1]).astype(BF16)
    p = _dot(h, win_ref[...])
    abuf[CONV_HALO:CONV_HALO + tm, :] = p[:, 0:ca] * jax.nn.sigmoid(p[:, ca:2 * ca])
    gate_b = p[:, 2 * ca:2 * ca + cb]
    bbuf[CONV_B_HALO:CONV_B_HALO + tm, :] = p[:, 2 * ca + cb:2 * ca + 2 * cb] * p[:, 2 * ca + 2 * cb:]

    rows = 64
    for r0 in range(0, tm, rows):
        acc = jnp.zeros((rows, ca), F32)
        for k in range(CONV_A_WIDTH):
            off = CONV_HALO + r0 - (CONV_A_WIDTH - 1) + k
            acc = acc + caw_ref[k:k + 1, :] * abuf[off:off + rows, :]
        u = _layer_norm(acc + cab_ref[...], nag_ref[...], nab_ref[...])
        cat[r0:r0 + rows, 0:ca] = _silu(u).astype(BF16)
        accb = jnp.zeros((rows, cb), F32)
        for k in range(CONV_B_WIDTH):
            off = CONV_B_HALO + r0 - (CONV_B_WIDTH - 1) + k
            accb = accb + cbw_ref[k:k + 1, :] * bbuf[off:off + rows, :]
        cat[r0:r0 + rows, ca:ca + cb] = (gate_b[r0:r0 + rows] * accb).astype(BF16)

    abuf[0:CONV_HALO, :] = abuf[tm:tm + CONV_HALO, :]
    bbuf[0:CONV_B_HALO, :] = bbuf[tm:tm + CONV_B_HALO, :]

    y = _dot(cat[...], wout_ref[...])
    z = DN_ALPHA * x + (1.0 + mod[2:3]) * y
    o_ref[0] = _layer_norm(z, g_ref[...], b_ref[...])


def _even_mixer(x, mod, layer, win, caw, cab, nag, nab, cbw, wout, g, b):
    bsz, s, d = x.shape
    tm = min(MIX_ROWS, s)
    ca, cb = CONV_A_CH, CONV_B_CH
    return pl.pallas_call(
        _even_kernel,
        grid=(bsz, s // tm),
        in_specs=[
            pl.BlockSpec((1, tm, d), lambda i, t: (i, t, 0)),
            pl.BlockSpec((1, 1, 1, 3, d), lambda i, t: (layer, i, 1, 0, 0)),
            _const_spec(win.shape),
            _const_spec(caw.shape),
            _const_spec((1, ca)),
            _const_spec((1, ca)),
            _const_spec((1, ca)),
            _const_spec(cbw.shape),
            _const_spec(wout.shape),
            _const_spec((1, d)),
            _const_spec((1, d)),
        ],
        out_specs=pl.BlockSpec((1, tm, d), lambda i, t: (i, t, 0)),
        out_shape=jax.ShapeDtypeStruct(x.shape, F32),
        scratch_shapes=[
            pltpu.VMEM((CONV_HALO + tm, ca), F32),
            pltpu.VMEM((CONV_B_HALO + tm, cb), F32),
            pltpu.VMEM((tm, ca + cb), BF16),
        ],
        compiler_params=_cparams(("arbitrary", "arbitrary")),
        name="even_mixer",
    )(x, mod, win, caw, cab.reshape(1, ca), nag.reshape(1, ca), nab.reshape(1, ca), cbw, wout,
      g.reshape(1, d), b.reshape(1, d))


ODD_COLS = NSA_Q + 4 * HEAD_DIM + 2 * HEAD_DIM + POOL_CH + LANES


def _odd_proj_kernel(x_ref, mod_ref, w_ref, q_ref, kv_ref, kvc_ref, u_ref, g_ref):
    x = x_ref[0]
    mod = mod_ref[0, 0, 0]
    h = (x * (1.0 + mod[1:2]) + mod[0:1]).astype(BF16)
    p = _dot(h, w_ref[...])
    c0 = NSA_Q
    c1 = c0 + 4 * HEAD_DIM
    c2 = c1 + 2 * HEAD_DIM
    c3 = c2 + POOL_CH
    q_ref[0] = (p[:, 0:c0] * (HEAD_DIM ** -0.5)).astype(BF16)
    kv_ref[0] = p[:, c0:c1].astype(BF16)
    kvc_ref[0] = p[:, c1:c2]
    u_ref[0] = p[:, c2:c3]
    g_ref[0] = jax.nn.sigmoid(p[:, c3:])


def _odd_proj(x, mod, layer, w):
    bsz, s, d = x.shape
    tm = min(MIX_ROWS, s)

    def blk(c):
        return pl.BlockSpec((1, tm, c), lambda i, t: (i, t, 0))

    return pl.pallas_call(
        _odd_proj_kernel,
        grid=(bsz, s // tm),
        in_specs=[
            blk(d),
            pl.BlockSpec((1, 1, 1, 3, d), lambda i, t: (layer, i, 1, 0, 0)),
            _const_spec(w.shape),
        ],
        out_specs=[blk(NSA_Q), blk(4 * HEAD_DIM), blk(2 * HEAD_DIM), blk(POOL_CH), blk(LANES)],
        out_shape=[
            jax.ShapeDtypeStruct((bsz, s, NSA_Q), BF16),
            jax.ShapeDtypeStruct((bsz, s, 4 * HEAD_DIM), BF16),
            jax.ShapeDtypeStruct((bsz, s, 2 * HEAD_DIM), F32),
            jax.ShapeDtypeStruct((bsz, s, POOL_CH), F32),
            jax.ShapeDtypeStruct((bsz, s, LANES), F32),
        ],
        compiler_params=_cparams(("arbitrary", "arbitrary")),
        name="odd_proj",
    )(x, mod, w)


def _compress_kernel(x_ref, pe_ref, wt_ref, wb_ref, w2_ref, o_ref):
    x = x_ref[0]
    top = _dot((x + pe_ref[0:1]).astype(BF16), wt_ref[...])
    bot = _dot((x + pe_ref[1:2]).astype(BF16), wb_ref[...])
    hid = top + pltpu.roll(bot, bot.shape[0] - 1, axis=0)
    o_ref[0] = _dot(_silu(hid).astype(BF16), w2_ref[...]).astype(BF16)


def _compress(kvc, pe, wt, wb, w2):
    bsz, s, _ = kvc.shape
    nrow = s // CMP_STRIDE
    width = CMP_STRIDE * 2 * HEAD_DIM
    x = kvc.reshape(bsz, nrow, width)
    return pl.pallas_call(
        _compress_kernel,
        grid=(bsz,),
        in_specs=[
            pl.BlockSpec((1, nrow, width), lambda i: (i, 0, 0)),
            _const_spec(pe.shape),
            _const_spec(wt.shape),
            _const_spec(wb.shape),
            _const_spec(w2.shape),
        ],
        out_specs=pl.BlockSpec((1, nrow, 2 * HEAD_DIM), lambda i: (i, 0, 0)),
        out_shape=jax.ShapeDtypeStruct((bsz, nrow, 2 * HEAD_DIM), BF16),
        compiler_params=_cparams(("arbitrary",)),
        name="compress",
    )(x, pe, wt, wb, w2)


def _t5_bucket_table(max_dist):
    n = np.arange(max_dist, dtype=np.int64)
    exact = REL_BUCKETS // 2
    nf = np.maximum(n, 1).astype(np.float32)
    large = exact + (np.log(nf / np.float32(exact)) / np.float32(math.log(REL_MAX_DIST / exact))
                     * np.float32(REL_BUCKETS - exact)).astype(np.int32)
    return np.where(n < exact, n, np.minimum(large, REL_BUCKETS - 1)).astype(np.int32)


def _bias_tables(rel_bias, n_cmp_cols):
    far = REL_BUCKETS - 1
    table = _t5_bucket_table(1024)
    rb = rel_bias.astype(F32) - rel_bias[far:far + 1].astype(F32)

    def gather(dist):
        bucket = np.where(dist >= 0, table[np.clip(dist, 0, 1023)], far)
        return jnp.transpose(rb[bucket], (2, 0, 1))

    r = np.arange(128)[:, None]
    c = np.arange(256)[None, :]
    tile = gather(r - c + 128)
    rq = np.arange(ATT_Q)[:, None]
    jj = np.arange(n_cmp_cols)[None, :]
    cmp_tab = gather(rq - CMP_STRIDE * (jj - (n_cmp_cols - 16)) - (CMP_BLOCK - 1))
    return tile, cmp_tab


def _attn_kernel(q_ref, g_ref, kvc_ref, kv_ref, tile_ref, cmpb_ref, ov_ref, o_ref,
                 qs, sbuf, pbuf, mbuf, m_s, l_s, a_s, acc_s, oc_s, os_s):
    nh, hd = NSA_HEADS, HEAD_DIM
    tq, tk = ATT_Q, ATT_K
    i = pl.program_id(1)
    t0 = i * tq
    ncmp = kvc_ref.shape[1]

    for h in range(nh):
        qs[h] = q_ref[0, :, h * hd:(h + 1) * hd]
    q_all = qs[...].reshape(nh * tq, hd)

    row = lax.broadcasted_iota(jnp.int32, (tq, tk), 0)
    col = lax.broadcasted_iota(jnp.int32, (tq, tk), 1)

    kc = kvc_ref[0, :, 0:hd]
    vc = kvc_ref[0, :, hd:2 * hd]
    sbuf[:, :, 0:ncmp] = _dot_nt(q_all, kc).reshape(nh, tq, ncmp)
    crow = lax.broadcasted_iota(jnp.int32, (tq, ncmp), 0)
    ccol = lax.broadcasted_iota(jnp.int32, (tq, ncmp), 1)
    valid_c = ccol * CMP_STRIDE + (CMP_BLOCK - 1) <= t0 + crow
    shift = ((tq // CMP_STRIDE) * i + 16) % ncmp

    def cmp_head(h, psum):
        bias = pltpu.roll(cmpb_ref[h], shift, axis=1)
        logit = jnp.where(valid_c, sbuf[h, :, 0:ncmp] + bias, NEG_INF)
        m = jnp.max(logit, axis=-1, keepdims=True)
        e = jnp.where(valid_c, jnp.exp(logit - m), 0.0)
        den = jnp.sum(e, axis=-1, keepdims=True)
        p = e / jnp.where(den > 0.0, den, 1.0)
        pbuf[h, :, 0:ncmp] = p.astype(BF16)
        return psum + p

    psum = lax.fori_loop(0, nh, cmp_head, jnp.zeros((tq, ncmp), F32))
    oc_s[...] = _dot(pbuf[:, :, 0:ncmp].reshape(nh * tq, ncmp), vc).reshape(nh, tq, hd)

    ov = ov_ref[...]
    p_hi = psum.astype(BF16)
    p_lo = (psum - p_hi.astype(F32)).astype(BF16)
    imp = _dot(p_hi, ov) + _dot(p_lo, ov)
    nslc = ov.shape[1]
    blk = lax.broadcasted_iota(jnp.int32, (tq, nslc), 1)
    cur = (t0 + lax.broadcasted_iota(jnp.int32, (tq, nslc), 0)) // SLC_BLOCK
    forced = (blk == 0) | (blk == cur) | (blk == cur - 1)
    val = jnp.where(forced, FORCE, jnp.where(blk <= cur, imp, -FORCE))
    rank = jnp.zeros((tq, nslc), F32)
    for k in range(nslc):
        vk = val[:, k:k + 1]
        rank = rank + jnp.where(blk > k, jnp.where(vk >= val, 1.0, 0.0), jnp.where(vk > val, 1.0, 0.0))
    sel = jnp.where(rank < float(min(SLC_TOPK, nslc)), 1.0, 0.0).astype(BF16)

    def sel_mask(c):
        er = lax.broadcasted_iota(jnp.int32, (nslc, tk), 0)
        ec = lax.broadcasted_iota(jnp.int32, (nslc, tk), 1)
        expand = jnp.where(er == c * (tk // SLC_BLOCK) + ec // SLC_BLOCK, 1.0, 0.0).astype(BF16)
        return _dot(sel, expand) > 0.5

    def chunk(k, v, mode, first):
        sbuf[...] = _dot_nt(q_all, k).reshape(nh, tq, tk)

        def head(h, carry):
            s = sbuf[h]
            if mode == "cur":
                t = tile_ref[h]
                s = s + jnp.concatenate(
                    [jnp.concatenate([t[:, 128:256], t[:, 128:256]], axis=1), t], axis=0)
            elif mode == "prev":
                t = tile_ref[h]
                s = s + jnp.concatenate(
                    [jnp.concatenate([jnp.zeros((128, 128), F32), t[:, 0:128]], axis=1),
                     jnp.zeros((128, 256), F32)], axis=0)
            s = s + mbuf[...]
            mx = jnp.max(s, axis=-1, keepdims=True)
            if first:
                m_new = mx
                p = jnp.exp(s - m_new)
                l_s[h] = jnp.sum(p, axis=-1, keepdims=True)
            else:
                m_old = m_s[h]
                m_new = jnp.maximum(m_old, mx)
                alpha = jnp.exp(m_old - m_new)
                p = jnp.exp(s - m_new)
                l_s[h] = alpha * l_s[h] + jnp.sum(p, axis=-1, keepdims=True)
                a_s[h] = alpha
            m_s[h] = m_new
            pbuf[h] = p.astype(BF16)
            return carry

        lax.fori_loop(0, nh, head, 0)
        pv = _dot(pbuf[...].reshape(nh * tq, tk), v).reshape(nh, tq, hd)
        if first:
            acc_s[...] = pv
        else:
            acc_s[...] = acc_s[...] * a_s[...] + pv

    def kv_rows(start, c0):
        r0 = pl.multiple_of(KV_PAD + start, tk)
        return (kv_ref[0, pl.ds(r0, tk), c0 * hd:(c0 + 1) * hd],
                kv_ref[0, pl.ds(r0, tk), (c0 + 1) * hd:(c0 + 2) * hd])

    k, v = kv_rows(t0, 0)
    mbuf[...] = jnp.where(sel_mask(i) & (col <= row), 0.0, NEG_INF)
    chunk(k, v, "cur", True)

    @pl.when(i >= 1)
    def _():
        k, v = kv_rows(t0 - tk, 0)
        mbuf[...] = jnp.where(sel_mask(i - 1), 0.0, NEG_INF)
        chunk(k, v, "prev", False)

    def far(c, carry):
        k, v = kv_rows(c * tk, 0)
        mbuf[...] = jnp.where(sel_mask(c), 0.0, NEG_INF)
        chunk(k, v, "far", False)
        return carry

    lax.fori_loop(0, jnp.maximum(i - 1, 0), far, 0)
    os_s[...] = acc_s[...] / l_s[...]

    k, v = kv_rows(t0, 2)
    mbuf[...] = jnp.where(col <= row, 0.0, NEG_INF)
    chunk(k, v, "cur", True)
    k, v = kv_rows(t0 - tk, 2)
    mbuf[...] = jnp.where(i >= 1, 0.0, NEG_INF) + jnp.zeros((tq, tk), F32)
    chunk(k, v, "prev", False)
    k, v = kv_rows(t0 - 2 * tk, 2)
    mbuf[...] = jnp.where((col > row) & (i >= 2), 0.0, NEG_INF)
    chunk(k, v, "far", False)

    g = g_ref[0]
    for h in range(nh):
        o = (g[:, h:h + 1] * oc_s[h] + g[:, nh + h:nh + h + 1] * os_s[h]
             + g[:, 2 * nh + h:2 * nh + h + 1] * (acc_s[h] / l_s[h]))
        o_ref[0, :, h * hd:(h + 1) * hd] = o.astype(BF16)


def _attention(q, g, kcvc, kvp, tile, cmp_tab, ov):
    bsz, s, _ = q.shape
    nh, hd, tq, tk = NSA_HEADS, HEAD_DIM, ATT_Q, ATT_K
    ncmp = kcvc.shape[1]
    return pl.pallas_call(
        _attn_kernel,
        grid=(bsz, s // tq),
        in_specs=[
            pl.BlockSpec((1, tq, NSA_Q), lambda b, i: (b, i, 0)),
            pl.BlockSpec((1, tq, LANES), lambda b, i: (b, i, 0)),
            pl.BlockSpec((1, ncmp, 2 * hd), lambda b, i: (b, 0, 0)),
            pl.BlockSpec((1, kvp.shape[1], 4 * hd), lambda b, i: (b, 0, 0)),
            _const_spec(tile.shape),
            _const_spec(cmp_tab.shape),
            _const_spec(ov.shape),
        ],
        out_specs=pl.BlockSpec((1, tq, NSA_Q), lambda b, i: (b, i, 0)),
        out_shape=jax.ShapeDtypeStruct((bsz, s, NSA_Q), BF16),
        scratch_shapes=[
            pltpu.VMEM((nh, tq, hd), BF16),
            pltpu.VMEM((nh, tq, tk), F32),
            pltpu.VMEM((nh, tq, tk), BF16),
            pltpu.VMEM((tq, tk), F32),
            pltpu.VMEM((nh, tq, 1), F32),
            pltpu.VMEM((nh, tq, 1), F32),
            pltpu.VMEM((nh, tq, 1), F32),
            pltpu.VMEM((nh, tq, hd), F32),
            pltpu.VMEM((nh, tq, hd), F32),
            pltpu.VMEM((nh, tq, hd), F32),
        ],
        compiler_params=_cparams(("arbitrary", "arbitrary")),
        name="nsa_attention",
    )(q, g, kcvc, kvp, tile, cmp_tab, ov)


def _odd_out_kernel(x_ref, mod_ref, o_ref_in, u_ref, pw_ref, ps_ref, wo_ref, wp_ref, g_ref, b_ref, o_ref, ubuf):
    tm = x_ref.shape[1]
    t = pl.program_id(1)

    @pl.when(t == 0)
    def _():
        ubuf[0:POOL_HALO, :] = jnp.zeros((POOL_HALO, POOL_CH), F32)

    u = u_ref[0]
    ubuf[POOL_HALO:POOL_HALO + tm, :] = u

    def back(n):
        return ubuf[POOL_HALO - n:POOL_HALO - n + tm, :]

    lane_group = lax.broadcasted_iota(jnp.int32, (tm, POOL_CH), 1) // POOL_GROUP_CH
    pos = t * tm + lax.broadcasted_iota(jnp.int32, (tm, POOL_CH), 0)
    total = jnp.zeros((tm, POOL_CH), F32)
    width = jnp.ones((tm, POOL_CH), F32)
    for gi, w in enumerate(POOL_WINDOWS):
        sw = u
        for n in range(1, w):
            sw = sw + back(n)
        total = jnp.where(lane_group == gi, sw, total)
        width = jnp.where(lane_group == gi, float(w), width)
    cnt = jnp.minimum((pos + 1).astype(F32), width)
    dlt = (total / cnt - u).astype(BF16)
    ubuf[0:POOL_HALO, :] = ubuf[tm:tm + POOL_HALO, :]

    o_pool = (_dot(dlt, pw_ref[...]) * ps_ref[...]).astype(BF16)
    y = _dot(o_ref_in[0], wo_ref[...]) + _dot(o_pool, wp_ref[...])
    x = x_ref[0]
    mod = mod_ref[0, 0, 0]
    z = DN_ALPHA * x + (1.0 + mod[2:3]) * y
    o_ref[0] = _layer_norm(z, g_ref[...], b_ref[...])


def _odd_out(x, mod, layer, o_nsa, u, pw, ps, wo, wp, g, b):
    bsz, s, d = x.shape
    tm = min(MIX_ROWS, s)
    return pl.pallas_call(
        _odd_out_kernel,
        grid=(bsz, s // tm),
        in_specs=[
            pl.BlockSpec((1, tm, d), lambda i, t: (i, t, 0)),
            pl.BlockSpec((1, 1, 1, 3, d), lambda i, t: (layer, i, 1, 0, 0)),
            pl.BlockSpec((1, tm, NSA_Q), lambda i, t: (i, t, 0)),
            pl.BlockSpec((1, tm, POOL_CH), lambda i, t: (i, t, 0)),
            _const_spec(pw.shape),
            _const_spec((1, POOL_CH)),
            _const_spec(wo.shape),
            _const_spec(wp.shape),
            _const_spec((1, d)),
            _const_spec((1, d)),
        ],
        out_specs=pl.BlockSpec((1, tm, d), lambda i, t: (i, t, 0)),
        out_shape=jax.ShapeDtypeStruct(x.shape, F32),
        scratch_shapes=[pltpu.VMEM((POOL_HALO + tm, POOL_CH), F32)],
        compiler_params=_cparams(("arbitrary", "arbitrary")),
        name="odd_out",
    )(x, mod, o_nsa, u, pw, ps.reshape(1, POOL_CH), wo, wp, g.reshape(1, d), b.reshape(1, d))


def _odd_in_weight(w_in):
    hd, nh = HEAD_DIM, NSA_HEADS
    c = NSA_Q
    q = w_in[:, 0:c]
    kvc = w_in[:, c:c + 2 * hd]
    kv4 = w_in[:, c + 2 * hd:c + 6 * hd]
    gates = w_in[:, c + 6 * hd:c + 6 * hd + 3 * nh]
    u = w_in[:, c + 6 * hd + 3 * nh:]
    perm = np.array([3 * h + j for j in range(3) for h in range(nh)])
    gates = jnp.pad(gates[:, perm], ((0, 0), (0, LANES - 3 * nh)))
    return jnp.concatenate([q, kv4, kvc, u, gates], axis=1).astype(BF16)


def _compress_weights(pe_k, pe_v, w1_k, w1_v, w2_k, w2_v):
    hd, hid, half = HEAD_DIM, CMP_HIDDEN, CMP_STRIDE

    def halves(w1):
        w = w1.reshape(CMP_BLOCK, hd, hid)
        return w[:half], w[half:]

    kt, kb = halves(w1_k)
    vt, vb = halves(w1_v)
    z = jnp.zeros((half, hd, hid), F32)

    def assemble(k_part, v_part):
        k_rows = jnp.concatenate([k_part, z], axis=2)
        v_rows = jnp.concatenate([z, v_part], axis=2)
        return jnp.concatenate([k_rows, v_rows], axis=1).reshape(half * 2 * hd, 2 * hid).astype(BF16)

    wt = assemble(kt, vt)
    wb = assemble(kb, vb)
    pe = jnp.concatenate([pe_k.reshape(2, half, hd), pe_v.reshape(2, half, hd)], axis=2)
    pe = pe.reshape(2, half * 2 * hd)
    zz = jnp.zeros((hid, hd), F32)
    w2 = jnp.concatenate([jnp.concatenate([w2_k, zz], axis=1),
                          jnp.concatenate([zz, w2_v], axis=1)], axis=0).astype(BF16)
    return pe, wt, wb, w2


def _overlap_matrix(n_rows, n_slc):
    n_cmp = n_slc * SLC_BLOCK // CMP_STRIDE - 1
    cmp_start = np.arange(n_cmp) * CMP_STRIDE
    slc_start = np.arange(n_slc) * SLC_BLOCK
    ov = np.clip(np.minimum(cmp_start[:, None] + CMP_BLOCK, slc_start[None, :] + SLC_BLOCK)
                 - np.maximum(cmp_start[:, None], slc_start[None, :]), 0, None) / CMP_BLOCK
    out = np.zeros((n_rows, n_slc), np.float32)
    out[:n_cmp] = ov
    return jnp.asarray(out, BF16)


def _pool_weight(pool_w):
    ng, gc = len(POOL_WINDOWS), POOL_GROUP_CH
    w = jnp.zeros((ng * gc, ng * gc), F32)
    for gi in range(ng):
        w = w.at[gi * gc:(gi + 1) * gc, gi * gc:(gi + 1) * gc].set(pool_w[gi])
    return w.astype(BF16)


def _odd_mixer(x, mod, layer, w_in, pe_k, pe_v, w1_k, w2_k, w1_v, w2_v, pool_w, pool_scale, w_out,
               rel_bias, g, b):
    bsz, s, _ = x.shape
    assert s % ATT_Q == 0 and s // CMP_STRIDE == LANES
    q, kv4, kvc, u, gates = _odd_proj(x, mod, layer, _odd_in_weight(w_in))
    pe, wt, wb, w2 = _compress_weights(pe_k, pe_v, w1_k, w1_v, w2_k, w2_v)
    kcvc = _compress(kvc, pe, wt, wb, w2)
    kvp = jnp.pad(kv4, ((0, 0), (KV_PAD, 0), (0, 0)))
    tile, cmp_tab = _bias_tables(rel_bias, s // CMP_STRIDE)
    ov = _overlap_matrix(s // CMP_STRIDE, s // SLC_BLOCK)
    o_nsa = _attention(q, gates, kcvc, kvp, tile, cmp_tab, ov)
    return _odd_out(x, mod, layer, o_nsa, u, _pool_weight(pool_w), pool_scale,
                    w_out[:NSA_Q].astype(BF16), w_out[NSA_Q:].astype(BF16), g, b)


def kernel(x, c, ada_w, ada_b, ln_g, ln_b, ffn_w_gate, ffn_w_up, ffn_w_down, ev_w_in, ev_conv_a_w,
           ev_conv_a_b, ev_norm_a_g, ev_norm_a_b, ev_conv_b_w, ev_w_out, od_w_in, od_cmp_pe_k,
           od_cmp_pe_v, od_cmp_w1_k, od_cmp_w2_k, od_cmp_w1_v, od_cmp_w2_v, od_pool_w, od_pool_scale,
           od_w_out, rel_bias):
    bsz, s, d = x.shape
    depth = ada_w.shape[0]
    mod = _modulation(c, ada_w, ada_b).reshape(depth, bsz, 3, 3, d)
    wg = ffn_w_gate.astype(BF16)
    wu = ffn_w_up.astype(BF16)
    wd = ffn_w_down.astype(BF16)
    for layer in range(depth):
        j = layer // 2
        x = _ffn(x, mod, layer, 0, wg[layer, 0], wu[layer, 0], wd[layer, 0], ln_g[layer, 0], ln_b[layer, 0])
        if layer % 2 == 0:
            x = _even_mixer(x, mod, layer, ev_w_in[j].astype(BF16), ev_conv_a_w[j], ev_conv_a_b[j],
                            ev_norm_a_g[j], ev_norm_a_b[j], ev_conv_b_w[j], ev_w_out[j].astype(BF16),
                            ln_g[layer, 1], ln_b[layer, 1])
        else:
            x = _odd_mixer(x, mod, layer, od_w_in[j], od_cmp_pe_k[j], od_cmp_pe_v[j], od_cmp_w1_k[j],
                           od_cmp_w2_k[j], od_cmp_w1_v[j], od_cmp_w2_v[j], od_pool_w[j], od_pool_scale[j],
                           od_w_out[j], rel_bias, ln_g[layer, 1], ln_b[layer, 1])
        x = _ffn(x, mod, layer, 2, wg[layer, 1], wu[layer, 1], wd[layer, 1], ln_g[layer, 2], ln_b[layer, 2])
    return x
```

```python
import functools
import math

import numpy as np
import jax
import jax.numpy as jnp
from jax import lax
from jax.experimental import pallas as pl
from jax.experimental.pallas import tpu as pltpu

F32 = jnp.float32
BF16 = jnp.bfloat16

D_MODEL = 1024
DEPTH = 4
D_FF = 2816
FFN_RES_WEIGHT = 0.5
CONV_A_CH = 512
CONV_A_WIDTH = 31
CONV_B_CH = 512
CONV_B_WIDTH = 3
NSA_HEADS = 16
HEAD_DIM = 64
CMP_BLOCK = 32
CMP_STRIDE = 16
CMP_HIDDEN = 128
SLC_BLOCK = 64
SLC_TOPK = 16
WINDOW = 512
POOL_WINDOWS = (2, 4, 8, 16)
POOL_GROUP_CH = 64
POOL_CH = len(POOL_WINDOWS) * POOL_GROUP_CH
REL_BUCKETS = 32
REL_MAX_DIST = 128
DN_ALPHA = (2 * DEPTH) ** 0.25
LN_EPS = 1e-5
NSA_Q = NSA_HEADS * HEAD_DIM
NEG_INF = -1e30
FORCE = 1e30

LANES = 128
VMEM_LIMIT_BYTES = 56 * 1024 * 1024

FFN_ROWS = 512
MIX_ROWS = 512
ATT_Q = 256
ATT_K = 256
CONV_HALO = 32
CONV_B_HALO = 8
POOL_HALO = 16
HEAD_LOOKAHEAD = 4


def _cparams(sem):
    return pltpu.CompilerParams(dimension_semantics=sem, vmem_limit_bytes=VMEM_LIMIT_BYTES)


def _const_spec(shape):
    n = len(shape)
    return pl.BlockSpec(shape, lambda *_: (0,) * n, pipeline_mode=pl.Buffered(1))


def _layer_norm(z, g, b):
    mu = jnp.mean(z, axis=-1, keepdims=True)
    zc = z - mu
    var = jnp.mean(zc * zc, axis=-1, keepdims=True)
    return zc * lax.rsqrt(var + LN_EPS) * g + b


def _silu(v):
    return v * jax.nn.sigmoid(v)


def _dot(a, b):
    return jnp.dot(a, b, preferred_element_type=F32)


def _dot_nt(a, b):
    return lax.dot_general(a, b, (((1,), (1,)), ((), ())), preferred_element_type=F32)


def _mod_kernel(c_ref, w_ref, b_ref, o_ref):
    cond = _silu(c_ref[...])
    o_ref[0] = _dot(cond.astype(BF16), w_ref[0].astype(BF16)) + b_ref[0]


def _modulation(c, ada_w, ada_b):
    depth, d, n = ada_w.shape
    bsz = c.shape[0]
    tn = 1152
    return pl.pallas_call(
        _mod_kernel,
        grid=(depth, n // tn),
        in_specs=[
            pl.BlockSpec((bsz, d), lambda l, j: (0, 0)),
            pl.BlockSpec((1, d, tn), lambda l, j: (l, 0, j)),
            pl.BlockSpec((1, 1, tn), lambda l, j: (l, 0, j)),
        ],
        out_specs=pl.BlockSpec((1, bsz, tn), lambda l, j: (l, 0, j)),
        out_shape=jax.ShapeDtypeStruct((depth, bsz, n), F32),
        compiler_params=_cparams(("arbitrary", "arbitrary")),
        name="modulation",
    )(c, ada_w, ada_b.reshape(depth, 1, n))


def _ffn_kernel(x_ref, mod_ref, wg_ref, wu_ref, wd_ref, g_ref, b_ref, o_ref):
    x = x_ref[0]
    mod = mod_ref[0, 0, 0]
    h = (x * (1.0 + mod[1:2]) + mod[0:1]).astype(BF16)
    a = (_silu(_dot(h, wg_ref[...])) * _dot(h, wu_ref[...])).astype(BF16)
    y = _dot(a, wd_ref[...])
    z = DN_ALPHA * x + (FFN_RES_WEIGHT * (1.0 + mod[2:3])) * y
    o_ref[0] = _layer_norm(z, g_ref[...], b_ref[...])


def _ffn(x, mod, layer, sub, wg, wu, wd, g, b):
    bsz, s, d = x.shape
    ff = wg.shape[1]
    tm = min(FFN_ROWS, s)
    return pl.pallas_call(
        _ffn_kernel,
        grid=(bsz, s // tm),
        in_specs=[
            pl.BlockSpec((1, tm, d), lambda i, t: (i, t, 0)),
            pl.BlockSpec((1, 1, 1, 3, d), lambda i, t: (layer, i, sub, 0, 0)),
            _const_spec((d, ff)),
            _const_spec((d, ff)),
            _const_spec((ff, d)),
            _const_spec((1, d)),
            _const_spec((1, d)),
        ],
        out_specs=pl.BlockSpec((1, tm, d), lambda i, t: (i, t, 0)),
        out_shape=jax.ShapeDtypeStruct(x.shape, F32),
        compiler_params=_cparams(("arbitrary", "arbitrary")),
        name="ffn",
    )(x, mod, wg, wu, wd, g.reshape(1, d), b.reshape(1, d))


def _even_kernel(x_ref, mod_ref, win_ref, caw_ref, cab_ref, nag_ref, nab_ref, cbw_ref, wout_ref,
                 g_ref, b_ref, o_ref, abuf, bbuf, cat):
    tm = x_ref.shape[1]
    ca, cb = CONV_A_CH, CONV_B_CH

    @pl.when(pl.program_id(1) == 0)
    def _():
        abuf[0:CONV_HALO, :] = jnp.zeros((CONV_HALO, ca), F32)
        bbuf[0:CONV_B_HALO, :] = jnp.zeros((CONV_B_HALO, cb), F32)

    x = x_ref[0]
    mod = mod_ref[0, 0, 0]
    h = (x * (1.0 + mod[1:2]) + mod[0:1]).astype(BF16)
    p = _dot(h, win_ref[...])
    abuf[CONV_HALO:CONV_HALO + tm, :] = p[:, 0:ca] * jax.nn.sigmoid(p[:, ca:2 * ca])
    gate_b = p[:, 2 * ca:2 * ca + cb]
    bbuf[CONV_B_HALO:CONV_B_HALO + tm, :] = p[:, 2 * ca + cb:2 * ca + 2 * cb] * p[:, 2 * ca + 2 * cb:]

    rows = 64
    for r0 in range(0, tm, rows):
        acc = jnp.zeros((rows, ca), F32)
        for k in range(CONV_A_WIDTH):
            off = CONV_HALO + r0 - (CONV_A_WIDTH - 1) + k
            acc = acc + caw_ref[k:k + 1, :] * abuf[off:off + rows, :]
        u = _layer_norm(acc + cab_ref[...], nag_ref[...], nab_ref[...])
        cat[r0:r0 + rows, 0:ca] = _silu(u).astype(BF16)
        accb = jnp.zeros((rows, cb), F32)
        for k in range(CONV_B_WIDTH):
            off = CONV_B_HALO + r0 - (CONV_B_WIDTH - 1) + k
            accb = accb + cbw_ref[k:k + 1, :] * bbuf[off:off + rows, :]
        cat[r0:r0 + rows, ca:ca + cb] = (gate_b[r0:r0 + rows] * accb).astype(BF16)

    abuf[0:CONV_HALO, :] = abuf[tm:tm + CONV_HALO, :]
    bbuf[0:CONV_B_HALO, :] = bbuf[tm:tm + CONV_B_HALO, :]

    y = _dot(cat[...], wout_ref[...])
    z = DN_ALPHA * x + (1.0 + mod[2:3]) * y
    o_ref[0] = _layer_norm(z, g_ref[...], b_ref[...])


def _even_mixer(x, mod, layer, win, caw, cab, nag, nab, cbw, wout, g, b):
    bsz, s, d = x.shape
    tm = min(MIX_ROWS, s)
    ca, cb = CONV_A_CH, CONV_B_CH
    return pl.pallas_call(
        _even_kernel,
        grid=(bsz, s // tm),
        in_specs=[
            pl.BlockSpec((1, tm, d), lambda i, t: (i, t, 0)),
            pl.BlockSpec((1, 1, 1, 3, d), lambda i, t: (layer, i, 1, 0, 0)),
            _const_spec(win.shape),
            _const_spec(caw.shape),
            _const_spec((1, ca)),
            _const_spec((1, ca)),
            _const_spec((1, ca)),
            _const_spec(cbw.shape),
            _const_spec(wout.shape),
            _const_spec((1, d)),
            _const_spec((1, d)),
        ],
        out_specs=pl.BlockSpec((1, tm, d), lambda i, t: (i, t, 0)),
        out_shape=jax.ShapeDtypeStruct(x.shape, F32),
        scratch_shapes=[
            pltpu.VMEM((CONV_HALO + tm, ca), F32),
            pltpu.VMEM((CONV_B_HALO + tm, cb), F32),
            pltpu.VMEM((tm, ca + cb), BF16),
        ],
        compiler_params=_cparams(("arbitrary", "arbitrary")),
        name="even_mixer",
    )(x, mod, win, caw, cab.reshape(1, ca), nag.reshape(1, ca), nab.reshape(1, ca), cbw, wout,
      g.reshape(1, d), b.reshape(1, d))


ODD_STD_COLS = 2 * HEAD_DIM + 2 * HEAD_DIM + POOL_CH
ODD_T_ROWS = NSA_Q + 2 * HEAD_DIM + 3 * NSA_HEADS


def _odd_proj_kernel(x_ref, mod_ref, w_ref, wt_ref, k_ref, kvc_ref, u_ref, qt_ref, vt_ref, gt_ref):
    x = x_ref[0]
    mod = mod_ref[0, 0, 0]
    h = (x * (1.0 + mod[1:2]) + mod[0:1]).astype(BF16)
    p = _dot(h, w_ref[...])
    c1 = 2 * HEAD_DIM
    c2 = c1 + 2 * HEAD_DIM
    k_ref[0] = p[:, 0:c1].astype(BF16)
    kvc_ref[0] = p[:, c1:c2]
    u_ref[0] = p[:, c2:]
    pt = _dot_nt(wt_ref[...], h)
    r1 = NSA_Q
    r2 = r1 + 2 * HEAD_DIM
    qt_ref[0] = (pt[0:r1] * (HEAD_DIM ** -0.5)).astype(BF16)
    vt_ref[0] = pt[r1:r2].astype(BF16)
    gt_ref[0] = jax.nn.sigmoid(pt[r2:])


def _odd_proj(x, mod, layer, w, wt):
    bsz, s, d = x.shape
    tm = min(MIX_ROWS, s)

    def blk(c):
        return pl.BlockSpec((1, tm, c), lambda i, t: (i, t, 0))

    def blk_t(r):
        return pl.BlockSpec((1, r, tm), lambda i, t: (i, 0, t))

    return pl.pallas_call(
        _odd_proj_kernel,
        grid=(bsz, s // tm),
        in_specs=[
            blk(d),
            pl.BlockSpec((1, 1, 1, 3, d), lambda i, t: (layer, i, 1, 0, 0)),
            _const_spec(w.shape),
            _const_spec(wt.shape),
        ],
        out_specs=[blk(2 * HEAD_DIM), blk(2 * HEAD_DIM), blk(POOL_CH),
                   blk_t(NSA_Q), blk_t(2 * HEAD_DIM), blk_t(3 * NSA_HEADS)],
        out_shape=[
            jax.ShapeDtypeStruct((bsz, s, 2 * HEAD_DIM), BF16),
            jax.ShapeDtypeStruct((bsz, s, 2 * HEAD_DIM), F32),
            jax.ShapeDtypeStruct((bsz, s, POOL_CH), F32),
            jax.ShapeDtypeStruct((bsz, NSA_Q, s), BF16),
            jax.ShapeDtypeStruct((bsz, 2 * HEAD_DIM, s), BF16),
            jax.ShapeDtypeStruct((bsz, 3 * NSA_HEADS, s), F32),
        ],
        compiler_params=_cparams(("arbitrary", "arbitrary")),
        name="odd_proj",
    )(x, mod, w, wt)


def _compress_kernel(x_ref, pe_ref, wt_ref, wb_ref, w2k_ref, w2vt_ref, kc_ref, vct_ref):
    x = x_ref[0]
    top = _dot((x + pe_ref[0:1]).astype(BF16), wt_ref[...])
    bot = _dot((x + pe_ref[1:2]).astype(BF16), wb_ref[...])
    hid = top + pltpu.roll(bot, bot.shape[0] - 1, axis=0)
    act = _silu(hid).astype(BF16)
    kc_ref[0] = _dot(act[:, 0:CMP_HIDDEN], w2k_ref[...]).astype(BF16)
    vct_ref[0] = _dot_nt(w2vt_ref[...], act[:, CMP_HIDDEN:]).astype(BF16)


def _compress(kvc, pe, wt, wb, w2k, w2vt):
    bsz, s, _ = kvc.shape
    nrow = s // CMP_STRIDE
    width = CMP_STRIDE * 2 * HEAD_DIM
    x = kvc.reshape(bsz, nrow, width)
    return pl.pallas_call(
        _compress_kernel,
        grid=(bsz,),
        in_specs=[
            pl.BlockSpec((1, nrow, width), lambda i: (i, 0, 0)),
            _const_spec(pe.shape),
            _const_spec(wt.shape),
            _const_spec(wb.shape),
            _const_spec(w2k.shape),
            _const_spec(w2vt.shape),
        ],
        out_specs=[pl.BlockSpec((1, nrow, HEAD_DIM), lambda i: (i, 0, 0)),
                   pl.BlockSpec((1, HEAD_DIM, nrow), lambda i: (i, 0, 0))],
        out_shape=[jax.ShapeDtypeStruct((bsz, nrow, HEAD_DIM), BF16),
                   jax.ShapeDtypeStruct((bsz, HEAD_DIM, nrow), BF16)],
        compiler_params=_cparams(("arbitrary",)),
        name="compress",
    )(x, pe, wt, wb, w2k, w2vt)


KX_COLS = 256
KX_WIN = 128
QX_ROWS = HEAD_DIM + 16
VX_ROWS = HEAD_DIM + 16
CMP_TAB_ROWS = 240


def _t5_bucket_table(max_dist):
    n = np.arange(max_dist, dtype=np.int64)
    exact = REL_BUCKETS // 2
    nf = np.maximum(n, 1).astype(np.float32)
    large = exact + (np.log(nf / np.float32(exact)) / np.float32(math.log(REL_MAX_DIST / exact))
                     * np.float32(REL_BUCKETS - exact)).astype(np.int32)
    return np.where(n < exact, n, np.minimum(large, REL_BUCKETS - 1)).astype(np.int32)


def _bias_tables(rel_bias):
    far = REL_BUCKETS - 1
    table = _t5_bucket_table(1024)

    def buckets(dist):
        return np.where(dist >= 0, table[np.clip(dist, 0, 1023)], REL_BUCKETS).reshape(-1)

    r = np.arange(128)[:, None]
    c = np.arange(ATT_Q)[None, :]
    jj = np.arange(CMP_TAB_ROWS)[:, None]
    bucket = np.concatenate([buckets(c - r), buckets(c - CMP_STRIDE * (jj - 112) - (CMP_BLOCK - 1))])
    onehot = (jnp.asarray(bucket)[None, :] == jnp.arange(REL_BUCKETS + 1)[:, None]).astype(F32)
    rb = (rel_bias.astype(F32) - rel_bias[far:far + 1].astype(F32)).T
    rb = jnp.concatenate([rb, jnp.full((rb.shape[0], 1), NEG_INF, F32)], axis=1)
    flat = jnp.dot(rb, onehot, precision=lax.Precision.HIGHEST)
    nh = rel_bias.shape[1]
    tile = flat[:, :128 * ATT_Q].reshape(nh, 128, ATT_Q)
    cmp_tab = flat[:, 128 * ATT_Q:].reshape(nh, CMP_TAB_ROWS, ATT_Q)
    return tile, cmp_tab


def _attn_kernel(qt_ref, gt_ref, kc_ref, vct_ref, kx_ref, vx_ref, tile_ref, cmpb_ref, ovt_ref, pm_ref,
                 o_ref, qx, selb, old_mask, acc, m_s, ot):
    nh, hd = NSA_HEADS, HEAD_DIM
    tq, tk = ATT_Q, ATT_K
    i = pl.program_id(1)
    t0 = i * tq
    ncmp = kc_ref.shape[1]

    @pl.when(i == 0)
    def _():
        r = lax.broadcasted_iota(jnp.int32, (tk, tq), 0)
        c = lax.broadcasted_iota(jnp.int32, (tk, tq), 1)
        old_mask[...] = jnp.where(r > c, 0.0, NEG_INF)

    for h in range(nh):
        qx[h, 0:hd, :] = qt_ref[0, h * hd:(h + 1) * hd, :]

    kc = kc_ref[0]
    vct = vct_ref[0]
    start = pl.multiple_of(112 - (tq // CMP_STRIDE) * i, 16)
    tok = t0 + lax.broadcasted_iota(jnp.int32, (1, tq), 1)
    any_valid = jnp.where(tok >= CMP_BLOCK - 1, 1.0, 0.0)

    def cmp_logits(h):
        return _dot(kc, qx[h, 0:hd, :]) + cmpb_ref[h, pl.ds(start, ncmp), :]

    pending = {h: cmp_logits(h) for h in range(HEAD_LOOKAHEAD)}
    psum = jnp.zeros((ncmp, tq), F32)
    for h in range(nh):
        s = pending.pop(h)
        e = jnp.exp(s - jnp.max(s, axis=0, keepdims=True))
        p = e * (any_valid / jnp.sum(e, axis=0, keepdims=True))
        psum = psum + p
        ot[h] = gt_ref[0, 0, h] * _dot(vct, p.astype(BF16))
        if h + HEAD_LOOKAHEAD < nh:
            pending[h + HEAD_LOOKAHEAD] = cmp_logits(h + HEAD_LOOKAHEAD)

    p_hi = psum.astype(BF16)
    p_lo = (psum - p_hi.astype(F32)).astype(BF16)
    ovt = ovt_ref[...]
    imp = _dot(ovt, p_hi) + _dot(ovt, p_lo)
    nslc = ovt.shape[0]
    blk = lax.broadcasted_iota(jnp.int32, (nslc, tq), 0)
    cur = (t0 + lax.broadcasted_iota(jnp.int32, (nslc, tq), 1)) // SLC_BLOCK
    forced = (blk == 0) | (blk == cur) | (blk == cur - 1)
    val = jnp.where(forced, FORCE, jnp.where(blk <= cur, imp, -FORCE))
    rank = jnp.zeros((nslc, tq), F32)
    for k in range(nslc):
        vk = val[k:k + 1, :]
        rank = rank + jnp.where(blk > k, jnp.where(vk >= val, 1.0, 0.0), jnp.where(vk > val, 1.0, 0.0))
    selbias = jnp.where(rank < float(min(SLC_TOPK, nslc)), 0.0, NEG_INF).astype(BF16)
    selb[...] = _dot(pm_ref[...], selbias).reshape(selb.shape).astype(BF16)

    def chunk(row0, selected, mode, first):
        row0 = pl.multiple_of(row0, tk)
        if selected:
            kx = kx_ref[0, pl.ds(row0, tk), 0:QX_ROWS]
            vx = vx_ref[0, 0:VX_ROWS, pl.ds(row0, tk)]
        else:
            kx = kx_ref[0, pl.ds(row0, tk), KX_WIN:KX_WIN + hd]
            vx = vx_ref[0, VX_ROWS:2 * VX_ROWS, pl.ds(row0, tk)]

        def logits(h):
            s = _dot(kx, qx[h] if selected else qx[h, 0:hd, :])
            if mode == "cur":
                t = tile_ref[h]
                bot = jnp.concatenate([jnp.full((128, 128), NEG_INF, F32),
                                       s[128:256, 128:256] + t[:, 0:128]], axis=1)
                s = jnp.concatenate([s[0:128] + t, bot], axis=0)
            elif mode == "prev":
                t = tile_ref[h]
                bot = jnp.concatenate([s[128:256, 0:128] + t[:, 128:256], s[128:256, 128:256]], axis=1)
                s = jnp.concatenate([s[0:128], bot], axis=0)
            elif mode == "old":
                s = s + old_mask[...]
            return s

        pending = {h: logits(h) for h in range(HEAD_LOOKAHEAD)}
        for h in range(nh):
            s = pending.pop(h)
            mc = jnp.max(s, axis=0, keepdims=True)
            if first:
                m_new = mc
                acc[h] = _dot(vx, jnp.exp(s - m_new).astype(BF16))
            else:
                m_old = m_s[h]
                m_new = jnp.maximum(m_old, mc)
                pv = _dot(vx, jnp.exp(s - m_new).astype(BF16))
                acc[h] = jnp.exp(m_old - m_new) * acc[h] + pv
            m_s[h] = m_new
            if h + HEAD_LOOKAHEAD < nh:
                pending[h + HEAD_LOOKAHEAD] = logits(h + HEAD_LOOKAHEAD)

    def set_sel_rows(c):
        tile = selb[c]
        for h in range(nh):
            qx[h, hd:QX_ROWS, :] = tile

    def finish(branch):
        for h in range(nh):
            a = acc[h]
            ot[h] = ot[h] + gt_ref[0, branch, h] * (a[0:hd] / a[hd:hd + 1])

    set_sel_rows(i)
    chunk(t0, True, "cur", True)

    @pl.when(i >= 1)
    def _():
        set_sel_rows(i - 1)
        chunk(t0 - tk, True, "prev", False)

    def far(c, carry):
        set_sel_rows(c)
        chunk(c * tk, True, "far", False)
        return carry

    lax.fori_loop(0, jnp.maximum(i - 1, 0), far, 0)
    finish(1)

    chunk(t0, False, "cur", True)

    @pl.when(i >= 1)
    def _():
        chunk(t0 - tk, False, "prev", False)

    @pl.when(i >= 2)
    def _():
        chunk(t0 - 2 * tk, False, "old", False)

    finish(2)
    o_ref[0] = ot[...].reshape(nh * hd, tq).T.astype(BF16)


def _attention(qt, gt, kc, vct, kx, vx, tile, cmp_tab, ovt, pm):
    bsz, _, s = qt.shape
    nh, hd, tq, tk = NSA_HEADS, HEAD_DIM, ATT_Q, ATT_K
    ncmp = kc.shape[1]
    nchunk = s // tk
    return pl.pallas_call(
        _attn_kernel,
        grid=(bsz, s // tq),
        in_specs=[
            pl.BlockSpec((1, NSA_Q, tq), lambda b, i: (b, 0, i)),
            pl.BlockSpec((1, 3, nh, 1, tq), lambda b, i: (b, 0, 0, 0, i)),
            pl.BlockSpec((1, ncmp, hd), lambda b, i: (b, 0, 0)),
            pl.BlockSpec((1, hd, ncmp), lambda b, i: (b, 0, 0)),
            pl.BlockSpec((1, kx.shape[1], KX_COLS), lambda b, i: (b, 0, 0)),
            pl.BlockSpec((1, 2 * VX_ROWS, vx.shape[2]), lambda b, i: (b, 0, 0)),
            _const_spec(tile.shape),
            _const_spec(cmp_tab.shape),
            _const_spec(ovt.shape),
            _const_spec(pm.shape),
        ],
        out_specs=pl.BlockSpec((1, tq, NSA_Q), lambda b, i: (b, i, 0)),
        out_shape=jax.ShapeDtypeStruct((bsz, s, NSA_Q), BF16),
        scratch_shapes=[
            pltpu.VMEM((nh, QX_ROWS, tq), BF16),
            pltpu.VMEM((nchunk, 16, tq), BF16),
            pltpu.VMEM((tk, tq), F32),
            pltpu.VMEM((nh, VX_ROWS, tq), F32),
            pltpu.VMEM((nh, 1, tq), F32),
            pltpu.VMEM((nh, hd, tq), F32),
        ],
        compiler_params=_cparams(("arbitrary", "arbitrary")),
        name="nsa_attention",
    )(qt, gt, kc, vct, kx, vx, tile, cmp_tab, ovt, pm)


def _odd_out_kernel(x_ref, mod_ref, o_ref_in, u_ref, pw_ref, ps_ref, wo_ref, wp_ref, g_ref, b_ref, o_ref, ubuf):
    tm = x_ref.shape[1]
    t = pl.program_id(1)

    @pl.when(t == 0)
    def _():
        ubuf[0:POOL_HALO, :] = jnp.zeros((POOL_HALO, POOL_CH), F32)

    u = u_ref[0]
    ubuf[POOL_HALO:POOL_HALO + tm, :] = u

    def back(n):
        return ubuf[POOL_HALO - n:POOL_HALO - n + tm, :]

    lane_group = lax.broadcasted_iota(jnp.int32, (tm, POOL_CH), 1) // POOL_GROUP_CH
    pos = t * tm + lax.broadcasted_iota(jnp.int32, (tm, POOL_CH), 0)
    total = jnp.zeros((tm, POOL_CH), F32)
    width = jnp.ones((tm, POOL_CH), F32)
    for gi, w in enumerate(POOL_WINDOWS):
        sw = u
        for n in range(1, w):
            sw = sw + back(n)
        total = jnp.where(lane_group == gi, sw, total)
        width = jnp.where(lane_group == gi, float(w), width)
    cnt = jnp.minimum((pos + 1).astype(F32), width)
    dlt = (total / cnt - u).astype(BF16)
    ubuf[0:POOL_HALO, :] = ubuf[tm:tm + POOL_HALO, :]

    o_pool = (_dot(dlt, pw_ref[...]) * ps_ref[...]).astype(BF16)
    y = _dot(o_ref_in[0], wo_ref[...]) + _dot(o_pool, wp_ref[...])
    x = x_ref[0]
    mod = mod_ref[0, 0, 0]
    z = DN_ALPHA * x + (1.0 + mod[2:3]) * y
    o_ref[0] = _layer_norm(z, g_ref[...], b_ref[...])


def _odd_out(x, mod, layer, o_nsa, u, pw, ps, wo, wp, g, b):
    bsz, s, d = x.shape
    tm = min(MIX_ROWS, s)
    return pl.pallas_call(
        _odd_out_kernel,
        grid=(bsz, s // tm),
        in_specs=[
            pl.BlockSpec((1, tm, d), lambda i, t: (i, t, 0)),
            pl.BlockSpec((1, 1, 1, 3, d), lambda i, t: (layer, i, 1, 0, 0)),
            pl.BlockSpec((1, tm, NSA_Q), lambda i, t: (i, t, 0)),
            pl.BlockSpec((1, tm, POOL_CH), lambda i, t: (i, t, 0)),
            _const_spec(pw.shape),
            _const_spec((1, POOL_CH)),
            _const_spec(wo.shape),
            _const_spec(wp.shape),
            _const_spec((1, d)),
            _const_spec((1, d)),
        ],
        out_specs=pl.BlockSpec((1, tm, d), lambda i, t: (i, t, 0)),
        out_shape=jax.ShapeDtypeStruct(x.shape, F32),
        scratch_shapes=[pltpu.VMEM((POOL_HALO + tm, POOL_CH), F32)],
        compiler_params=_cparams(("arbitrary", "arbitrary")),
        name="odd_out",
    )(x, mod, o_nsa, u, pw, ps.reshape(1, POOL_CH), wo, wp, g.reshape(1, d), b.reshape(1, d))


def _odd_in_weights(w_in):
    hd, nh = HEAD_DIM, NSA_HEADS
    c = NSA_Q
    col = lambda k: w_in[:, c + k * hd:c + (k + 1) * hd]
    kc, vc, ks, vs, kw, vw = (col(k) for k in range(6))
    gates = w_in[:, c + 6 * hd:c + 6 * hd + 3 * nh]
    u = w_in[:, c + 6 * hd + 3 * nh:]
    perm = np.array([3 * h + j for j in range(3) for h in range(nh)])
    w_std = jnp.concatenate([ks, kw, kc, vc, u], axis=1).astype(BF16)
    w_t = jnp.concatenate([w_in[:, 0:c], vs, vw, gates[:, perm]], axis=1).T.astype(BF16)
    return w_std, w_t


def _compress_weights(pe_k, pe_v, w1_k, w1_v, w2_k, w2_v):
    hd, hid, half = HEAD_DIM, CMP_HIDDEN, CMP_STRIDE

    def halves(w1):
        w = w1.reshape(CMP_BLOCK, hd, hid)
        return w[:half], w[half:]

    kt, kb = halves(w1_k)
    vt, vb = halves(w1_v)
    z = jnp.zeros((half, hd, hid), F32)

    def assemble(k_part, v_part):
        k_rows = jnp.concatenate([k_part, z], axis=2)
        v_rows = jnp.concatenate([z, v_part], axis=2)
        return jnp.concatenate([k_rows, v_rows], axis=1).reshape(half * 2 * hd, 2 * hid).astype(BF16)

    wt = assemble(kt, vt)
    wb = assemble(kb, vb)
    pe = jnp.concatenate([pe_k.reshape(2, half, hd), pe_v.reshape(2, half, hd)], axis=2)
    pe = pe.reshape(2, half * 2 * hd)
    return pe, wt, wb, w2_k.astype(BF16), w2_v.T.astype(BF16)


def _overlap_matrix_t(n_cols, n_slc):
    n_cmp = n_slc * SLC_BLOCK // CMP_STRIDE - 1
    cmp_start = np.arange(n_cmp) * CMP_STRIDE
    slc_start = np.arange(n_slc) * SLC_BLOCK
    ov = np.clip(np.minimum(cmp_start[:, None] + CMP_BLOCK, slc_start[None, :] + SLC_BLOCK)
                 - np.maximum(cmp_start[:, None], slc_start[None, :]), 0, None) / CMP_BLOCK
    out = np.zeros((n_slc, n_cols), np.float32)
    out[:, :n_cmp] = ov.T
    return jnp.asarray(out, BF16)


def _chunk_row_placement(n_chunk, n_slc):
    per = ATT_K // SLC_BLOCK
    pm = np.zeros((n_chunk * 16, n_slc), np.float32)
    for c in range(n_chunk):
        for b in range(per):
            pm[16 * c + b, per * c + b] = 1.0
    return jnp.asarray(pm, BF16)


def _pool_weight(pool_w):
    ng, gc = len(POOL_WINDOWS), POOL_GROUP_CH
    w = jnp.zeros((ng * gc, ng * gc), F32)
    for gi in range(ng):
        w = w.at[gi * gc:(gi + 1) * gc, gi * gc:(gi + 1) * gc].set(pool_w[gi])
    return w.astype(BF16)


def _key_value_operands(k, vt):
    bsz, s, _ = k.shape
    hd = HEAD_DIM
    per = ATT_K // SLC_BLOCK
    member = (np.arange(s)[:, None] % ATT_K) // SLC_BLOCK == np.arange(16)[None, :]
    member = member & (np.arange(16)[None, :] < per)
    onehot = jnp.broadcast_to(jnp.asarray(member, BF16)[None], (bsz, s, 16))
    kx = jnp.concatenate([k[:, :, 0:hd], onehot, jnp.zeros((bsz, s, KX_WIN - hd - 16), BF16),
                          k[:, :, hd:2 * hd], jnp.zeros((bsz, s, KX_COLS - KX_WIN - hd), BF16)], axis=2)
    ones = jnp.ones((bsz, 1, s), BF16)
    zeros = jnp.zeros((bsz, VX_ROWS - hd - 1, s), BF16)
    vx = jnp.concatenate([vt[:, 0:hd], ones, zeros, vt[:, hd:2 * hd], ones, zeros], axis=1)
    return kx, vx


def _odd_mixer(x, mod, layer, w_in, pe_k, pe_v, w1_k, w2_k, w1_v, w2_v, pool_w, pool_scale, w_out,
               bias_tables, g, b):
    bsz, s, _ = x.shape
    assert s % ATT_Q == 0 and s // CMP_STRIDE == LANES and ATT_Q == ATT_K
    w_std, w_t = _odd_in_weights(w_in)
    k, kvc, u, qt, vt, gt = _odd_proj(x, mod, layer, w_std, w_t)
    kc, vct = _compress(kvc, *_compress_weights(pe_k, pe_v, w1_k, w1_v, w2_k, w2_v))
    kx, vx = _key_value_operands(k, vt)
    tile, cmp_tab = bias_tables
    ovt = _overlap_matrix_t(s // CMP_STRIDE, s // SLC_BLOCK)
    pm = _chunk_row_placement(s // ATT_K, s // SLC_BLOCK)
    o_nsa = _attention(qt, gt.reshape(bsz, 3, NSA_HEADS, 1, s), kc, vct, kx, vx, tile, cmp_tab, ovt, pm)
    return _odd_out(x, mod, layer, o_nsa, u, _pool_weight(pool_w), pool_scale,
                    w_out[:NSA_Q].astype(BF16), w_out[NSA_Q:].astype(BF16), g, b)


def kernel(x, c, ada_w, ada_b, ln_g, ln_b, ffn_w_gate, ffn_w_up, ffn_w_down, ev_w_in, ev_conv_a_w,
           ev_conv_a_b, ev_norm_a_g, ev_norm_a_b, ev_conv_b_w, ev_w_out, od_w_in, od_cmp_pe_k,
           od_cmp_pe_v, od_cmp_w1_k, od_cmp_w2_k, od_cmp_w1_v, od_cmp_w2_v, od_pool_w, od_pool_scale,
           od_w_out, rel_bias):
    bsz, s, d = x.shape
    depth = ada_w.shape[0]
    mod = _modulation(c, ada_w, ada_b).reshape(depth, bsz, 3, 3, d)
    bias_tables = _bias_tables(rel_bias)
    wg = ffn_w_gate.astype(BF16)
    wu = ffn_w_up.astype(BF16)
    wd = ffn_w_down.astype(BF16)
    for layer in range(depth):
        j = layer // 2
        x = _ffn(x, mod, layer, 0, wg[layer, 0], wu[layer, 0], wd[layer, 0], ln_g[layer, 0], ln_b[layer, 0])
        if layer % 2 == 0:
            x = _even_mixer(x, mod, layer, ev_w_in[j].astype(BF16), ev_conv_a_w[j], ev_conv_a_b[j],
                            ev_norm_a_g[j], ev_norm_a_b[j], ev_conv_b_w[j], ev_w_out[j].astype(BF16),
                            ln_g[layer, 1], ln_b[layer, 1])
        else:
            x = _odd_mixer(x, mod, layer, od_w_in[j], od_cmp_pe_k[j], od_cmp_pe_v[j], od_cmp_w1_k[j],
                           od_cmp_w2_k[j], od_cmp_w1_v[j], od_cmp_w2_v[j], od_pool_w[j], od_pool_scale[j],
                           od_w_out[j], bias_tables, ln_g[layer, 1], ln_b[layer, 1])
        x = _ffn(x, mod, layer, 2, wg[layer, 1], wu[layer, 1], wd[layer, 1], ln_g[layer, 2], ln_b[layer, 2])
    return x
```

```python
import functools
import math

import numpy as np
import jax
import jax.numpy as jnp
from jax import lax
from jax.experimental import pallas as pl
from jax.experimental.pallas import tpu as pltpu

F32 = jnp.float32
BF16 = jnp.bfloat16

D_MODEL = 1024
DEPTH = 4
D_FF = 2816
FFN_RES_WEIGHT = 0.5
CONV_A_CH = 512
CONV_A_WIDTH = 31
CONV_B_CH = 512
CONV_B_WIDTH = 3
NSA_HEADS = 16
HEAD_DIM = 64
CMP_BLOCK = 32
CMP_STRIDE = 16
CMP_HIDDEN = 128
SLC_BLOCK = 64
SLC_TOPK = 16
WINDOW = 512
POOL_WINDOWS = (2, 4, 8, 16)
POOL_GROUP_CH = 64
POOL_CH = len(POOL_WINDOWS) * POOL_GROUP_CH
REL_BUCKETS = 32
REL_MAX_DIST = 128
DN_ALPHA = (2 * DEPTH) ** 0.25
LN_EPS = 1e-5
NSA_Q = NSA_HEADS * HEAD_DIM
NEG_INF = -1e30
FORCE = 1e30

LANES = 128
SUBLANES = 8
VMEM_LIMIT_BYTES = 56 * 1024 * 1024

FFN_ROWS = 512
MIX_ROWS = 512
ATT_Q = 256
ATT_K = 256
CONV_HALO = 32
CONV_B_HALO = 8
POOL_HALO = 16
HEAD_LOOKAHEAD = 6


def _cparams(sem):
    return pltpu.CompilerParams(dimension_semantics=sem, vmem_limit_bytes=VMEM_LIMIT_BYTES)


def _const_spec(shape):
    n = len(shape)
    return pl.BlockSpec(shape, lambda *_: (0,) * n, pipeline_mode=pl.Buffered(1))


def _layer_norm(z, g, b):
    mu = jnp.mean(z, axis=-1, keepdims=True)
    zc = z - mu
    var = jnp.mean(zc * zc, axis=-1, keepdims=True)
    return zc * lax.rsqrt(var + LN_EPS) * g + b


def _silu(v):
    return v * jax.nn.sigmoid(v)


def _dot(a, b):
    return jnp.dot(a, b, preferred_element_type=F32)


def _dot_nt(a, b):
    return lax.dot_general(a, b, (((1,), (1,)), ((), ())), preferred_element_type=F32)


def _mod_kernel(c_ref, w_ref, b_ref, o_ref):
    cond = _silu(c_ref[...])
    o_ref[0] = _dot(cond.astype(BF16), w_ref[0].astype(BF16)) + b_ref[0]


def _modulation(c, ada_w, ada_b):
    depth, d, n = ada_w.shape
    bsz = c.shape[0]
    tn = 1152
    return pl.pallas_call(
        _mod_kernel,
        grid=(depth, n // tn),
        in_specs=[
            pl.BlockSpec((bsz, d), lambda l, j: (0, 0)),
            pl.BlockSpec((1, d, tn), lambda l, j: (l, 0, j)),
            pl.BlockSpec((1, 1, tn), lambda l, j: (l, 0, j)),
        ],
        out_specs=pl.BlockSpec((1, bsz, tn), lambda l, j: (l, 0, j)),
        out_shape=jax.ShapeDtypeStruct((depth, bsz, n), F32),
        compiler_params=_cparams(("arbitrary", "arbitrary")),
        name="modulation",
    )(c, ada_w, ada_b.reshape(depth, 1, n))


def _ffn_kernel(x_ref, mod_ref, wg_ref, wu_ref, wd_ref, g_ref, b_ref, o_ref):
    x = x_ref[0]
    mod = mod_ref[0, 0, 0]
    h = (x * (1.0 + mod[1:2]) + mod[0:1]).astype(BF16)
    a = (_silu(_dot(h, wg_ref[0, 0])) * _dot(h, wu_ref[0, 0])).astype(BF16)
    y = _dot(a, wd_ref[0, 0])
    z = DN_ALPHA * x + (FFN_RES_WEIGHT * (1.0 + mod[2:3])) * y
    o_ref[0] = _layer_norm(z, g_ref[...], b_ref[...])


def _ffn(x, mod, layer, sub, half, wg, wu, wd, g, b):
    bsz, s, d = x.shape
    ff = wg.shape[3]
    tm = min(FFN_ROWS, s)

    def slab(r, c):
        return pl.BlockSpec((1, 1, r, c), lambda i, t: (layer, half, 0, 0), pipeline_mode=pl.Buffered(1))

    return pl.pallas_call(
        _ffn_kernel,
        grid=(bsz, s // tm),
        in_specs=[
            pl.BlockSpec((1, tm, d), lambda i, t: (i, t, 0)),
            pl.BlockSpec((1, 1, 1, 3, d), lambda i, t: (layer, i, sub, 0, 0)),
            slab(d, ff),
            slab(d, ff),
            slab(ff, d),
            _const_spec((1, d)),
            _const_spec((1, d)),
        ],
        out_specs=pl.BlockSpec((1, tm, d), lambda i, t: (i, t, 0)),
        out_shape=jax.ShapeDtypeStruct(x.shape, F32),
        compiler_params=_cparams(("arbitrary", "arbitrary")),
        name="ffn",
    )(x, mod, wg, wu, wd, g.reshape(1, d), b.reshape(1, d))


def _even_kernel(x_ref, mod_ref, win_ref, caw_ref, cab_ref, nag_ref, nab_ref, cbw_ref, wout_ref,
                 g_ref, b_ref, o_ref, abuf, ash, bbuf, cat):
    tm = x_ref.shape[1]
    ca, cb = CONV_A_CH, CONV_B_CH

    @pl.when(pl.program_id(1) == 0)
    def _():
        abuf[0:CONV_HALO, :] = jnp.zeros((CONV_HALO, ca), F32)
        bbuf[0:CONV_B_HALO, :] = jnp.zeros((CONV_B_HALO, cb), F32)

    x = x_ref[0]
    mod = mod_ref[0, 0, 0]
    h = (x * (1.0 + mod[1:2]) + mod[0:1]).astype(BF16)
    p = _dot(h, win_ref[...])
    abuf[CONV_HALO:CONV_HALO + tm, :] = p[:, 0:ca] * jax.nn.sigmoid(p[:, ca:2 * ca])
    gate_b = p[:, 2 * ca:2 * ca + cb]
    bbuf[CONV_B_HALO:CONV_B_HALO + tm, :] = p[:, 2 * ca + cb:2 * ca + 2 * cb] * p[:, 2 * ca + 2 * cb:]

    span = tm + CONV_HALO - SUBLANES
    for r in range(1, SUBLANES):
        ash[r - 1] = abuf[r:r + span, :]

    rows = 64
    for r0 in range(0, tm, rows):
        acc = jnp.zeros((rows, ca), F32)
        for k in range(CONV_A_WIDTH):
            off = CONV_HALO + r0 - (CONV_A_WIDTH - 1) + k
            r = off % SUBLANES
            tap = abuf[off:off + rows, :] if r == 0 else ash[r - 1, off - r:off - r + rows, :]
            acc = acc + caw_ref[k:k + 1, :] * tap
        u = _layer_norm(acc + cab_ref[...], nag_ref[...], nab_ref[...])
        cat[r0:r0 + rows, 0:ca] = _silu(u).astype(BF16)
        accb = jnp.zeros((rows, cb), F32)
        for k in range(CONV_B_WIDTH):
            off = CONV_B_HALO + r0 - (CONV_B_WIDTH - 1) + k
            accb = accb + cbw_ref[k:k + 1, :] * bbuf[off:off + rows, :]
        cat[r0:r0 + rows, ca:ca + cb] = (gate_b[r0:r0 + rows] * accb).astype(BF16)

    abuf[0:CONV_HALO, :] = abuf[tm:tm + CONV_HALO, :]
    bbuf[0:CONV_B_HALO, :] = bbuf[tm:tm + CONV_B_HALO, :]

    y = _dot(cat[...], wout_ref[...])
    z = DN_ALPHA * x + (1.0 + mod[2:3]) * y
    o_ref[0] = _layer_norm(z, g_ref[...], b_ref[...])


def _even_mixer(x, mod, layer, win, caw, cab, nag, nab, cbw, wout, g, b):
    bsz, s, d = x.shape
    tm = min(MIX_ROWS, s)
    ca, cb = CONV_A_CH, CONV_B_CH
    return pl.pallas_call(
        _even_kernel,
        grid=(bsz, s // tm),
        in_specs=[
            pl.BlockSpec((1, tm, d), lambda i, t: (i, t, 0)),
            pl.BlockSpec((1, 1, 1, 3, d), lambda i, t: (layer, i, 1, 0, 0)),
            _const_spec(win.shape),
            _const_spec(caw.shape),
            _const_spec((1, ca)),
            _const_spec((1, ca)),
            _const_spec((1, ca)),
            _const_spec(cbw.shape),
            _const_spec(wout.shape),
            _const_spec((1, d)),
            _const_spec((1, d)),
        ],
        out_specs=pl.BlockSpec((1, tm, d), lambda i, t: (i, t, 0)),
        out_shape=jax.ShapeDtypeStruct(x.shape, F32),
        scratch_shapes=[
            pltpu.VMEM((CONV_HALO + tm, ca), F32),
            pltpu.VMEM((SUBLANES - 1, CONV_HALO + tm - SUBLANES, ca), F32),
            pltpu.VMEM((CONV_B_HALO + tm, cb), F32),
            pltpu.VMEM((tm, ca + cb), BF16),
        ],
        compiler_params=_cparams(("arbitrary", "arbitrary")),
        name="even_mixer",
    )(x, mod, win, caw, cab.reshape(1, ca), nag.reshape(1, ca), nab.reshape(1, ca), cbw, wout,
      g.reshape(1, d), b.reshape(1, d))


KX_COLS = 256
KX_WIN = 128
QX_ROWS = HEAD_DIM + 16
VX_ROWS = HEAD_DIM + 16
LOG2E = math.log2(math.e)


def _odd_proj_kernel(x_ref, mod_ref, w_ref, wt_ref, kx_ref, kvc_ref, u_ref, qt_ref, vx_ref, gt_ref):
    tm = x_ref.shape[1]
    x = x_ref[0]
    mod = mod_ref[0, 0, 0]
    h = (x * (1.0 + mod[1:2]) + mod[0:1]).astype(BF16)
    p = _dot(h, w_ref[...])
    c1 = KX_COLS
    c2 = c1 + 2 * HEAD_DIM
    row = lax.broadcasted_iota(jnp.int32, (tm, KX_COLS), 0)
    col = lax.broadcasted_iota(jnp.int32, (tm, KX_COLS), 1)
    member = (col - HEAD_DIM == (row % ATT_K) // SLC_BLOCK) & (col >= HEAD_DIM)
    kx_ref[0] = jnp.where(member, 1.0, p[:, 0:c1]).astype(BF16)
    kvc_ref[0] = p[:, c1:c2]
    u_ref[0] = p[:, c2:]
    pt = _dot_nt(wt_ref[...], h)
    r1 = NSA_Q
    r2 = r1 + 2 * VX_ROWS
    qt_ref[0] = (pt[0:r1] * (HEAD_DIM ** -0.5 * LOG2E)).astype(BF16)
    vrow = lax.broadcasted_iota(jnp.int32, (2 * VX_ROWS, tm), 0)
    vx_ref[0] = jnp.where(vrow % VX_ROWS == HEAD_DIM, 1.0, pt[r1:r2]).astype(BF16)
    gt_ref[0] = jax.nn.sigmoid(pt[r2:])


def _odd_proj(x, mod, layer, w, wt):
    bsz, s, d = x.shape
    tm = min(MIX_ROWS, s)

    def blk(c):
        return pl.BlockSpec((1, tm, c), lambda i, t: (i, t, 0))

    def blk_t(r):
        return pl.BlockSpec((1, r, tm), lambda i, t: (i, 0, t))

    return pl.pallas_call(
        _odd_proj_kernel,
        grid=(bsz, s // tm),
        in_specs=[
            blk(d),
            pl.BlockSpec((1, 1, 1, 3, d), lambda i, t: (layer, i, 1, 0, 0)),
            _const_spec(w.shape),
            _const_spec(wt.shape),
        ],
        out_specs=[blk(KX_COLS), blk(2 * HEAD_DIM), blk(POOL_CH),
                   blk_t(NSA_Q), blk_t(2 * VX_ROWS), blk_t(3 * NSA_HEADS)],
        out_shape=[
            jax.ShapeDtypeStruct((bsz, s, KX_COLS), BF16),
            jax.ShapeDtypeStruct((bsz, s, 2 * HEAD_DIM), F32),
            jax.ShapeDtypeStruct((bsz, s, POOL_CH), F32),
            jax.ShapeDtypeStruct((bsz, NSA_Q, s), BF16),
            jax.ShapeDtypeStruct((bsz, 2 * VX_ROWS, s), BF16),
            jax.ShapeDtypeStruct((bsz, 3 * NSA_HEADS, s), F32),
        ],
        compiler_params=_cparams(("arbitrary", "arbitrary")),
        name="odd_proj",
    )(x, mod, w, wt)


def _compress_kernel(x_ref, pe_ref, wt_ref, wb_ref, w2k_ref, w2vt_ref, kc_ref, vct_ref):
    x = x_ref[0]
    top = _dot((x + pe_ref[0:1]).astype(BF16), wt_ref[...])
    bot = _dot((x + pe_ref[1:2]).astype(BF16), wb_ref[...])
    hid = top + pltpu.roll(bot, bot.shape[0] - 1, axis=0)
    act = _silu(hid).astype(BF16)
    kc_ref[0] = _dot(act[:, 0:CMP_HIDDEN], w2k_ref[...]).astype(BF16)
    vct_ref[0] = _dot_nt(w2vt_ref[...], act[:, CMP_HIDDEN:]).astype(BF16)


def _compress(kvc, pe, wt, wb, w2k, w2vt):
    bsz, s, _ = kvc.shape
    nrow = s // CMP_STRIDE
    width = CMP_STRIDE * 2 * HEAD_DIM
    x = kvc.reshape(bsz, nrow, width)
    return pl.pallas_call(
        _compress_kernel,
        grid=(bsz,),
        in_specs=[
            pl.BlockSpec((1, nrow, width), lambda i: (i, 0, 0)),
            _const_spec(pe.shape),
            _const_spec(wt.shape),
            _const_spec(wb.shape),
            _const_spec(w2k.shape),
            _const_spec(w2vt.shape),
        ],
        out_specs=[pl.BlockSpec((1, nrow, HEAD_DIM), lambda i: (i, 0, 0)),
                   pl.BlockSpec((1, HEAD_DIM, nrow), lambda i: (i, 0, 0))],
        out_shape=[jax.ShapeDtypeStruct((bsz, nrow, HEAD_DIM), BF16),
                   jax.ShapeDtypeStruct((bsz, HEAD_DIM, nrow), BF16)],
        compiler_params=_cparams(("arbitrary",)),
        name="compress",
    )(x, pe, wt, wb, w2k, w2vt)


CMP_TAB_ROWS = 240


def _t5_bucket_table(max_dist):
    n = np.arange(max_dist, dtype=np.int64)
    exact = REL_BUCKETS // 2
    nf = np.maximum(n, 1).astype(np.float32)
    large = exact + (np.log(nf / np.float32(exact)) / np.float32(math.log(REL_MAX_DIST / exact))
                     * np.float32(REL_BUCKETS - exact)).astype(np.int32)
    return np.where(n < exact, n, np.minimum(large, REL_BUCKETS - 1)).astype(np.int32)


def _bias_tables(rel_bias):
    far = REL_BUCKETS - 1
    table = _t5_bucket_table(1024)

    def buckets(dist):
        return np.where(dist >= 0, table[np.clip(dist, 0, 1023)], REL_BUCKETS).reshape(-1)

    r = np.arange(128)[:, None]
    c = np.arange(ATT_Q)[None, :]
    jj = np.arange(CMP_TAB_ROWS)[:, None]
    bucket = np.concatenate([buckets(c - r), buckets(c - CMP_STRIDE * (jj - 112) - (CMP_BLOCK - 1))])
    onehot = (jnp.asarray(bucket)[None, :] == jnp.arange(REL_BUCKETS + 1)[:, None]).astype(F32)
    rb = (rel_bias.astype(F32) - rel_bias[far:far + 1].astype(F32)).T * LOG2E
    rb = jnp.concatenate([rb, jnp.full((rb.shape[0], 1), NEG_INF, F32)], axis=1)
    flat = jnp.dot(rb, onehot, precision=lax.Precision.HIGHEST)
    nh = rel_bias.shape[1]
    tile = flat[:, :128 * ATT_Q].reshape(nh, 128, ATT_Q)
    cmp_tab = flat[:, 128 * ATT_Q:].reshape(nh, CMP_TAB_ROWS, ATT_Q)
    return tile, cmp_tab


def _attn_kernel(qt_ref, gt_ref, kc_ref, vct_ref, kx_ref, vx_ref, tile_ref, cmpb_ref, ovt_ref, pm_ref,
                 o_ref, qx, selb, old_mask, acc, m_s, ot):
    nh, hd = NSA_HEADS, HEAD_DIM
    tq, tk = ATT_Q, ATT_K
    i = pl.program_id(1)
    t0 = i * tq
    ncmp = kc_ref.shape[1]

    @pl.when(i == 0)
    def _():
        r = lax.broadcasted_iota(jnp.int32, (tk, tq), 0)
        c = lax.broadcasted_iota(jnp.int32, (tk, tq), 1)
        old_mask[...] = jnp.where(r > c, 0.0, NEG_INF)

    for h in range(nh):
        qx[h, 0:hd, :] = qt_ref[0, h * hd:(h + 1) * hd, :]

    def pipelined(issue, consume):
        pending = {h: issue(h) for h in range(HEAD_LOOKAHEAD)}
        for h in range(nh):
            if h + HEAD_LOOKAHEAD < nh:
                pending[h + HEAD_LOOKAHEAD] = issue(h + HEAD_LOOKAHEAD)
            consume(h, pending.pop(h))

    kc = kc_ref[0]
    vct = vct_ref[0]
    start = pl.multiple_of(112 - (tq // CMP_STRIDE) * i, 16)
    tok = t0 + lax.broadcasted_iota(jnp.int32, (1, tq), 1)
    any_valid = jnp.where(tok >= CMP_BLOCK - 1, 1.0, 0.0)
    psum = [jnp.zeros((ncmp, tq), F32)]

    def cmp_logits(h):
        return _dot(kc, qx[h, 0:hd, :]) + cmpb_ref[h, pl.ds(start, ncmp), :]

    def cmp_softmax(h, s):
        e = jnp.exp2(s - jnp.max(s, axis=0, keepdims=True))
        prob = e * (any_valid / jnp.sum(e, axis=0, keepdims=True))
        psum[0] = psum[0] + prob
        ot[h] = gt_ref[0, 0, h] * _dot(vct, prob.astype(BF16))

    pipelined(cmp_logits, cmp_softmax)
    psum = psum[0]

    p_hi = psum.astype(BF16)
    p_lo = (psum - p_hi.astype(F32)).astype(BF16)
    ovt = ovt_ref[...]
    imp = _dot(ovt, p_hi) + _dot(ovt, p_lo)
    nslc = ovt.shape[0]
    blk = lax.broadcasted_iota(jnp.int32, (nslc, tq), 0)
    cur = (t0 + lax.broadcasted_iota(jnp.int32, (nslc, tq), 1)) // SLC_BLOCK
    forced = (blk == 0) | (blk == cur) | (blk == cur - 1)
    val = jnp.where(forced, FORCE, jnp.where(blk <= cur, imp, -FORCE))
    rank = jnp.zeros((nslc, tq), F32)
    for k in range(nslc):
        vk = val[k:k + 1, :]
        rank = rank + jnp.where(blk > k, jnp.where(vk >= val, 1.0, 0.0), jnp.where(vk > val, 1.0, 0.0))
    selbias = jnp.where(rank < float(min(SLC_TOPK, nslc)), 0.0, NEG_INF).astype(BF16)
    selb[...] = _dot(pm_ref[...], selbias).reshape(selb.shape).astype(BF16)

    def chunk(row0, selected, mode, first):
        row0 = pl.multiple_of(row0, tk)
        if selected:
            kx = kx_ref[0, pl.ds(row0, tk), 0:QX_ROWS]
            vx = vx_ref[0, 0:VX_ROWS, pl.ds(row0, tk)]
        else:
            kx = kx_ref[0, pl.ds(row0, tk), KX_WIN:KX_WIN + hd]
            vx = vx_ref[0, VX_ROWS:2 * VX_ROWS, pl.ds(row0, tk)]

        def logits(h):
            s = _dot(kx, qx[h] if selected else qx[h, 0:hd, :])
            if mode == "cur":
                t = tile_ref[h]
                bot = jnp.concatenate([jnp.full((128, 128), NEG_INF, F32),
                                       s[128:256, 128:256] + t[:, 0:128]], axis=1)
                s = jnp.concatenate([s[0:128] + t, bot], axis=0)
            elif mode == "prev":
                t = tile_ref[h]
                bot = jnp.concatenate([s[128:256, 0:128] + t[:, 128:256], s[128:256, 128:256]], axis=1)
                s = jnp.concatenate([s[0:128], bot], axis=0)
            elif mode == "old":
                s = s + old_mask[...]
            return s

        def update(h, s):
            mc = jnp.max(s, axis=0, keepdims=True)
            if first:
                m_new = mc
                acc[h] = _dot(vx, jnp.exp2(s - m_new).astype(BF16))
            else:
                m_old = m_s[h]
                m_new = jnp.maximum(m_old, mc)
                pv = _dot(vx, jnp.exp2(s - m_new).astype(BF16))
                acc[h] = jnp.exp2(m_old - m_new) * acc[h] + pv
            m_s[h] = m_new

        pipelined(logits, update)

    def set_sel_rows(c):
        tile = selb[c]
        for h in range(nh):
            qx[h, hd:QX_ROWS, :] = tile

    def finish(branch):
        for h in range(nh):
            a = acc[h]
            ot[h] = ot[h] + gt_ref[0, branch, h] * (a[0:hd] / a[hd:hd + 1])

    set_sel_rows(i)
    chunk(t0, True, "cur", True)

    @pl.when(i >= 1)
    def _():
        set_sel_rows(i - 1)
        chunk(t0 - tk, True, "prev", False)

    def far(c, carry):
        set_sel_rows(c)
        chunk(c * tk, True, "far", False)
        return carry

    lax.fori_loop(0, jnp.maximum(i - 1, 0), far, 0)
    finish(1)

    chunk(t0, False, "cur", True)

    @pl.when(i >= 1)
    def _():
        chunk(t0 - tk, False, "prev", False)

    @pl.when(i >= 2)
    def _():
        chunk(t0 - 2 * tk, False, "old", False)

    finish(2)
    o_ref[0] = ot[...].reshape(nh * hd, tq).T.astype(BF16)


def _attention(qt, gt, kc, vct, kx, vx, tile, cmp_tab, ovt, pm):
    bsz, _, s = qt.shape
    nh, hd, tq, tk = NSA_HEADS, HEAD_DIM, ATT_Q, ATT_K
    ncmp = kc.shape[1]
    nchunk = s // tk
    return pl.pallas_call(
        _attn_kernel,
        grid=(bsz, s // tq),
        in_specs=[
            pl.BlockSpec((1, NSA_Q, tq), lambda b, i: (b, 0, i)),
            pl.BlockSpec((1, 3, nh, 1, tq), lambda b, i: (b, 0, 0, 0, i)),
            pl.BlockSpec((1, ncmp, hd), lambda b, i: (b, 0, 0)),
            pl.BlockSpec((1, hd, ncmp), lambda b, i: (b, 0, 0)),
            pl.BlockSpec((1, kx.shape[1], KX_COLS), lambda b, i: (b, 0, 0)),
            pl.BlockSpec((1, 2 * VX_ROWS, vx.shape[2]), lambda b, i: (b, 0, 0)),
            _const_spec(tile.shape),
            _const_spec(cmp_tab.shape),
            _const_spec(ovt.shape),
            _const_spec(pm.shape),
        ],
        out_specs=pl.BlockSpec((1, tq, NSA_Q), lambda b, i: (b, i, 0)),
        out_shape=jax.ShapeDtypeStruct((bsz, s, NSA_Q), BF16),
        scratch_shapes=[
            pltpu.VMEM((nh, QX_ROWS, tq), BF16),
            pltpu.VMEM((nchunk, 16, tq), BF16),
            pltpu.VMEM((tk, tq), F32),
            pltpu.VMEM((nh, VX_ROWS, tq), F32),
            pltpu.VMEM((nh, 1, tq), F32),
            pltpu.VMEM((nh, hd, tq), F32),
        ],
        compiler_params=_cparams(("arbitrary", "arbitrary")),
        name="nsa_attention",
    )(qt, gt, kc, vct, kx, vx, tile, cmp_tab, ovt, pm)


def _odd_out_kernel(x_ref, mod_ref, o_ref_in, u_ref, pw_ref, ps_ref, wo_ref, wp_ref, g_ref, b_ref, o_ref, ubuf):
    tm = x_ref.shape[1]
    t = pl.program_id(1)

    @pl.when(t == 0)
    def _():
        ubuf[0:POOL_HALO, :] = jnp.zeros((POOL_HALO, POOL_CH), F32)

    u = u_ref[0]
    ubuf[POOL_HALO:POOL_HALO + tm, :] = u

    def back(n):
        return ubuf[POOL_HALO - n:POOL_HALO - n + tm, :]

    lane_group = lax.broadcasted_iota(jnp.int32, (tm, POOL_CH), 1) // POOL_GROUP_CH
    pos = t * tm + lax.broadcasted_iota(jnp.int32, (tm, POOL_CH), 0)
    total = jnp.zeros((tm, POOL_CH), F32)
    width = jnp.ones((tm, POOL_CH), F32)
    for gi, w in enumerate(POOL_WINDOWS):
        sw = u
        for n in range(1, w):
            sw = sw + back(n)
        total = jnp.where(lane_group == gi, sw, total)
        width = jnp.where(lane_group == gi, float(w), width)
    cnt = jnp.minimum((pos + 1).astype(F32), width)
    dlt = (total / cnt - u).astype(BF16)
    ubuf[0:POOL_HALO, :] = ubuf[tm:tm + POOL_HALO, :]

    o_pool = (_dot(dlt, pw_ref[...]) * ps_ref[...]).astype(BF16)
    y = _dot(o_ref_in[0], wo_ref[...]) + _dot(o_pool, wp_ref[...])
    x = x_ref[0]
    mod = mod_ref[0, 0, 0]
    z = DN_ALPHA * x + (1.0 + mod[2:3]) * y
    o_ref[0] = _layer_norm(z, g_ref[...], b_ref[...])


def _odd_out(x, mod, layer, o_nsa, u, pw, ps, wo, wp, g, b):
    bsz, s, d = x.shape
    tm = min(MIX_ROWS, s)
    return pl.pallas_call(
        _odd_out_kernel,
        grid=(bsz, s // tm),
        in_specs=[
            pl.BlockSpec((1, tm, d), lambda i, t: (i, t, 0)),
            pl.BlockSpec((1, 1, 1, 3, d), lambda i, t: (layer, i, 1, 0, 0)),
            pl.BlockSpec((1, tm, NSA_Q), lambda i, t: (i, t, 0)),
            pl.BlockSpec((1, tm, POOL_CH), lambda i, t: (i, t, 0)),
            _const_spec(pw.shape),
            _const_spec((1, POOL_CH)),
            _const_spec(wo.shape),
            _const_spec(wp.shape),
            _const_spec((1, d)),
            _const_spec((1, d)),
        ],
        out_specs=pl.BlockSpec((1, tm, d), lambda i, t: (i, t, 0)),
        out_shape=jax.ShapeDtypeStruct(x.shape, F32),
        scratch_shapes=[pltpu.VMEM((POOL_HALO + tm, POOL_CH), F32)],
        compiler_params=_cparams(("arbitrary", "arbitrary")),
        name="odd_out",
    )(x, mod, o_nsa, u, pw, ps.reshape(1, POOL_CH), wo, wp, g.reshape(1, d), b.reshape(1, d))


def _odd_in_weights(w_in):
    hd, nh = HEAD_DIM, NSA_HEADS
    c = NSA_Q
    col = lambda k: w_in[:, c + k * hd:c + (k + 1) * hd]
    kc, vc, ks, vs, kw, vw = (col(k) for k in range(6))
    gates = w_in[:, c + 6 * hd:c + 6 * hd + 3 * nh]
    u = w_in[:, c + 6 * hd + 3 * nh:]
    perm = np.array([3 * h + j for j in range(3) for h in range(nh)])
    zc = lambda n: jnp.zeros((w_in.shape[0], n), w_in.dtype)
    w_std = jnp.concatenate([ks, zc(KX_WIN - hd), kw, zc(KX_COLS - KX_WIN - hd), kc, vc, u], axis=1)
    w_t = jnp.concatenate([w_in[:, 0:c], vs, zc(VX_ROWS - hd), vw, zc(VX_ROWS - hd), gates[:, perm]], axis=1)
    return w_std.astype(BF16), w_t.T.astype(BF16)


def _compress_weights(pe_k, pe_v, w1_k, w1_v, w2_k, w2_v):
    hd, hid, half = HEAD_DIM, CMP_HIDDEN, CMP_STRIDE

    def halves(w1):
        w = w1.reshape(CMP_BLOCK, hd, hid)
        return w[:half], w[half:]

    kt, kb = halves(w1_k)
    vt, vb = halves(w1_v)
    z = jnp.zeros((half, hd, hid), F32)

    def assemble(k_part, v_part):
        k_rows = jnp.concatenate([k_part, z], axis=2)
        v_rows = jnp.concatenate([z, v_part], axis=2)
        return jnp.concatenate([k_rows, v_rows], axis=1).reshape(half * 2 * hd, 2 * hid).astype(BF16)

    wt = assemble(kt, vt)
    wb = assemble(kb, vb)
    pe = jnp.concatenate([pe_k.reshape(2, half, hd), pe_v.reshape(2, half, hd)], axis=2)
    pe = pe.reshape(2, half * 2 * hd)
    return pe, wt, wb, w2_k.astype(BF16), w2_v.T.astype(BF16)


def _overlap_matrix_t(n_cols, n_slc):
    n_cmp = n_slc * SLC_BLOCK // CMP_STRIDE - 1
    cmp_start = np.arange(n_cmp) * CMP_STRIDE
    slc_start = np.arange(n_slc) * SLC_BLOCK
    ov = np.clip(np.minimum(cmp_start[:, None] + CMP_BLOCK, slc_start[None, :] + SLC_BLOCK)
                 - np.maximum(cmp_start[:, None], slc_start[None, :]), 0, None) / CMP_BLOCK
    out = np.zeros((n_slc, n_cols), np.float32)
    out[:, :n_cmp] = ov.T
    return jnp.asarray(out, BF16)


def _chunk_row_placement(n_chunk, n_slc):
    per = ATT_K // SLC_BLOCK
    pm = np.zeros((n_chunk * 16, n_slc), np.float32)
    for c in range(n_chunk):
        for b in range(per):
            pm[16 * c + b, per * c + b] = 1.0
    return jnp.asarray(pm, BF16)


def _pool_weight(pool_w):
    ng, gc = len(POOL_WINDOWS), POOL_GROUP_CH
    w = jnp.zeros((ng * gc, ng * gc), F32)
    for gi in range(ng):
        w = w.at[gi * gc:(gi + 1) * gc, gi * gc:(gi + 1) * gc].set(pool_w[gi])
    return w.astype(BF16)


def _odd_mixer(x, mod, layer, w_in, pe_k, pe_v, w1_k, w2_k, w1_v, w2_v, pool_w, pool_scale, w_out,
               bias_tables, g, b):
    bsz, s, _ = x.shape
    assert s % ATT_Q == 0 and s // CMP_STRIDE == LANES and ATT_Q == ATT_K and MIX_ROWS % ATT_K == 0
    w_std, w_t = _odd_in_weights(w_in)
    kx, kvc, u, qt, vx, gt = _odd_proj(x, mod, layer, w_std, w_t)
    kc, vct = _compress(kvc, *_compress_weights(pe_k, pe_v, w1_k, w1_v, w2_k, w2_v))
    tile, cmp_tab = bias_tables
    ovt = _overlap_matrix_t(s // CMP_STRIDE, s // SLC_BLOCK)
    pm = _chunk_row_placement(s // ATT_K, s // SLC_BLOCK)
    o_nsa = _attention(qt, gt.reshape(bsz, 3, NSA_HEADS, 1, s), kc, vct, kx, vx, tile, cmp_tab, ovt, pm)
    return _odd_out(x, mod, layer, o_nsa, u, _pool_weight(pool_w), pool_scale,
                    w_out[:NSA_Q].astype(BF16), w_out[NSA_Q:].astype(BF16), g, b)


def kernel(x, c, ada_w, ada_b, ln_g, ln_b, ffn_w_gate, ffn_w_up, ffn_w_down, ev_w_in, ev_conv_a_w,
           ev_conv_a_b, ev_norm_a_g, ev_norm_a_b, ev_conv_b_w, ev_w_out, od_w_in, od_cmp_pe_k,
           od_cmp_pe_v, od_cmp_w1_k, od_cmp_w2_k, od_cmp_w1_v, od_cmp_w2_v, od_pool_w, od_pool_scale,
           od_w_out, rel_bias):
    bsz, s, d = x.shape
    depth = ada_w.shape[0]
    mod = _modulation(c, ada_w, ada_b).reshape(depth, bsz, 3, 3, d)
    bias_tables = _bias_tables(rel_bias)
    wg = ffn_w_gate.astype(BF16)
    wu = ffn_w_up.astype(BF16)
    wd = ffn_w_down.astype(BF16)
    for layer in range(depth):
        j = layer // 2
        x = _ffn(x, mod, layer, 0, 0, wg, wu, wd, ln_g[layer, 0], ln_b[layer, 0])
        if layer % 2 == 0:
            x = _even_mixer(x, mod, layer, ev_w_in[j].astype(BF16), ev_conv_a_w[j], ev_conv_a_b[j],
                            ev_norm_a_g[j], ev_norm_a_b[j], ev_conv_b_w[j], ev_w_out[j].astype(BF16),
                            ln_g[layer, 1], ln_b[layer, 1])
        else:
            x = _odd_mixer(x, mod, layer, od_w_in[j], od_cmp_pe_k[j], od_cmp_pe_v[j], od_cmp_w1_k[j],
                           od_cmp_w2_k[j], od_cmp_w1_v[j], od_cmp_w2_v[j], od_pool_w[j], od_pool_scale[j],
                           od_w_out[j], bias_tables, ln_g[layer, 1], ln_b[layer, 1])
        x = _ffn(x, mod, layer, 2, 1, wg, wu, wd, ln_g[layer, 2], ln_b[layer, 2])
    return x
```

```python
import functools
import math

import numpy as np
import jax
import jax.numpy as jnp
from jax import lax
from jax.experimental import pallas as pl
from jax.experimental.pallas import tpu as pltpu

F32 = jnp.float32
BF16 = jnp.bfloat16

D_MODEL = 1024
DEPTH = 4
D_FF = 2816
FFN_RES_WEIGHT = 0.5
CONV_A_CH = 512
CONV_A_WIDTH = 31
CONV_B_CH = 512
CONV_B_WIDTH = 3
NSA_HEADS = 16
HEAD_DIM = 64
CMP_BLOCK = 32
CMP_STRIDE = 16
CMP_HIDDEN = 128
SLC_BLOCK = 64
SLC_TOPK = 16
WINDOW = 512
POOL_WINDOWS = (2, 4, 8, 16)
POOL_GROUP_CH = 64
POOL_CH = len(POOL_WINDOWS) * POOL_GROUP_CH
REL_BUCKETS = 32
REL_MAX_DIST = 128
DN_ALPHA = (2 * DEPTH) ** 0.25
LN_EPS = 1e-5
NSA_Q = NSA_HEADS * HEAD_DIM
NEG_INF = -1e30
FORCE = 1e30

LANES = 128
SUBLANES = 8
VMEM_LIMIT_BYTES = 56 * 1024 * 1024

FFN_ROWS = 512
FFN_SUB_ROWS = 256
MIX_ROWS = 512
ATT_Q = 256
ATT_K = 256
CONV_HALO = 32
CONV_B_HALO = 8
POOL_HALO = 16
HEAD_LOOKAHEAD = 6


def _cparams(sem):
    return pltpu.CompilerParams(dimension_semantics=sem, vmem_limit_bytes=VMEM_LIMIT_BYTES)


def _const_spec(shape):
    n = len(shape)
    return pl.BlockSpec(shape, lambda *_: (0,) * n, pipeline_mode=pl.Buffered(1))


def _layer_norm(z, g, b):
    mu = jnp.mean(z, axis=-1, keepdims=True)
    zc = z - mu
    var = jnp.mean(zc * zc, axis=-1, keepdims=True)
    return zc * lax.rsqrt(var + LN_EPS) * g + b


def _silu(v):
    return v * jax.nn.sigmoid(v)


def _dot(a, b):
    return jnp.dot(a, b, preferred_element_type=F32)


def _dot_nt(a, b):
    return lax.dot_general(a, b, (((1,), (1,)), ((), ())), preferred_element_type=F32)


def _mod_kernel(c_ref, w_ref, b_ref, o_ref):
    cond = _silu(c_ref[...])
    o_ref[0] = _dot(cond.astype(BF16), w_ref[0].astype(BF16)) + b_ref[0]


def _modulation(c, ada_w, ada_b):
    depth, d, n = ada_w.shape
    bsz = c.shape[0]
    tn = 1152
    return pl.pallas_call(
        _mod_kernel,
        grid=(depth, n // tn),
        in_specs=[
            pl.BlockSpec((bsz, d), lambda l, j: (0, 0)),
            pl.BlockSpec((1, d, tn), lambda l, j: (l, 0, j)),
            pl.BlockSpec((1, 1, tn), lambda l, j: (l, 0, j)),
        ],
        out_specs=pl.BlockSpec((1, bsz, tn), lambda l, j: (l, 0, j)),
        out_shape=jax.ShapeDtypeStruct((depth, bsz, n), F32),
        compiler_params=_cparams(("arbitrary", "arbitrary")),
        name="modulation",
    )(c, ada_w, ada_b.reshape(depth, 1, n))


def _ffn_kernel(x_ref, mod_ref, wg_ref, wu_ref, wd_ref, g_ref, b_ref, o_ref):
    mod = mod_ref[0, 0, 0]
    tm = x_ref.shape[1]
    for r0 in range(0, tm, FFN_SUB_ROWS):
        x = x_ref[0, r0:r0 + FFN_SUB_ROWS, :]
        h = (x * (1.0 + mod[1:2]) + mod[0:1]).astype(BF16)
        a = (_silu(_dot(h, wg_ref[0, 0])) * _dot(h, wu_ref[0, 0])).astype(BF16)
        y = _dot(a, wd_ref[0, 0])
        z = DN_ALPHA * x + (FFN_RES_WEIGHT * (1.0 + mod[2:3])) * y
        o_ref[0, r0:r0 + FFN_SUB_ROWS, :] = _layer_norm(z, g_ref[...], b_ref[...])


def _ffn(x, mod, layer, sub, half, wg, wu, wd, g, b):
    bsz, s, d = x.shape
    ff = wg.shape[3]
    tm = min(FFN_ROWS, s)

    def slab(r, c):
        return pl.BlockSpec((1, 1, r, c), lambda i, t: (layer, half, 0, 0), pipeline_mode=pl.Buffered(1))

    return pl.pallas_call(
        _ffn_kernel,
        grid=(bsz, s // tm),
        in_specs=[
            pl.BlockSpec((1, tm, d), lambda i, t: (i, t, 0)),
            pl.BlockSpec((1, 1, 1, 3, d), lambda i, t: (layer, i, sub, 0, 0)),
            slab(d, ff),
            slab(d, ff),
            slab(ff, d),
            _const_spec((1, d)),
            _const_spec((1, d)),
        ],
        out_specs=pl.BlockSpec((1, tm, d), lambda i, t: (i, t, 0)),
        out_shape=jax.ShapeDtypeStruct(x.shape, F32),
        compiler_params=_cparams(("arbitrary", "arbitrary")),
        name="ffn",
    )(x, mod, wg, wu, wd, g.reshape(1, d), b.reshape(1, d))


def _even_kernel(x_ref, mod_ref, win_ref, caw_ref, cab_ref, nag_ref, nab_ref, cbw_ref, wout_ref,
                 g_ref, b_ref, o_ref, abuf, ash, bbuf, cat):
    tm = x_ref.shape[1]
    ca, cb = CONV_A_CH, CONV_B_CH

    @pl.when(pl.program_id(1) == 0)
    def _():
        abuf[0:CONV_HALO, :] = jnp.zeros((CONV_HALO, ca), F32)
        bbuf[0:CONV_B_HALO, :] = jnp.zeros((CONV_B_HALO, cb), F32)

    x = x_ref[0]
    mod = mod_ref[0, 0, 0]
    h = (x * (1.0 + mod[1:2]) + mod[0:1]).astype(BF16)
    p = _dot(h, win_ref[...])
    abuf[CONV_HALO:CONV_HALO + tm, :] = p[:, 0:ca] * jax.nn.sigmoid(p[:, ca:2 * ca])
    gate_b = p[:, 2 * ca:2 * ca + cb]
    bbuf[CONV_B_HALO:CONV_B_HALO + tm, :] = p[:, 2 * ca + cb:2 * ca + 2 * cb] * p[:, 2 * ca + 2 * cb:]

    span = tm + CONV_HALO - SUBLANES
    for r in range(1, SUBLANES):
        ash[r - 1] = abuf[r:r + span, :]

    rows = 64
    for r0 in range(0, tm, rows):
        acc = jnp.zeros((rows // SUBLANES, SUBLANES, ca), F32)
        for k in range(CONV_A_WIDTH):
            off = CONV_HALO + r0 - (CONV_A_WIDTH - 1) + k
            r = off % SUBLANES
            tap = abuf[off:off + rows, :] if r == 0 else ash[r - 1, off - r:off - r + rows, :]
            acc = acc + caw_ref[k][None] * tap.reshape(rows // SUBLANES, SUBLANES, ca)
        acc = acc.reshape(rows, ca)
        u = _layer_norm(acc + cab_ref[...], nag_ref[...], nab_ref[...])
        cat[r0:r0 + rows, 0:ca] = _silu(u).astype(BF16)
        accb = jnp.zeros((rows, cb), F32)
        for k in range(CONV_B_WIDTH):
            off = CONV_B_HALO + r0 - (CONV_B_WIDTH - 1) + k
            accb = accb + cbw_ref[k:k + 1, :] * bbuf[off:off + rows, :]
        cat[r0:r0 + rows, ca:ca + cb] = (gate_b[r0:r0 + rows] * accb).astype(BF16)

    abuf[0:CONV_HALO, :] = abuf[tm:tm + CONV_HALO, :]
    bbuf[0:CONV_B_HALO, :] = bbuf[tm:tm + CONV_B_HALO, :]

    y = _dot(cat[...], wout_ref[...])
    z = DN_ALPHA * x + (1.0 + mod[2:3]) * y
    o_ref[0] = _layer_norm(z, g_ref[...], b_ref[...])


def _even_mixer(x, mod, layer, win, caw, cab, nag, nab, cbw, wout, g, b):
    bsz, s, d = x.shape
    tm = min(MIX_ROWS, s)
    ca, cb = CONV_A_CH, CONV_B_CH
    return pl.pallas_call(
        _even_kernel,
        grid=(bsz, s // tm),
        in_specs=[
            pl.BlockSpec((1, tm, d), lambda i, t: (i, t, 0)),
            pl.BlockSpec((1, 1, 1, 3, d), lambda i, t: (layer, i, 1, 0, 0)),
            _const_spec(win.shape),
            _const_spec(caw.shape),
            _const_spec((1, ca)),
            _const_spec((1, ca)),
            _const_spec((1, ca)),
            _const_spec(cbw.shape),
            _const_spec(wout.shape),
            _const_spec((1, d)),
            _const_spec((1, d)),
        ],
        out_specs=pl.BlockSpec((1, tm, d), lambda i, t: (i, t, 0)),
        out_shape=jax.ShapeDtypeStruct(x.shape, F32),
        scratch_shapes=[
            pltpu.VMEM((CONV_HALO + tm, ca), F32),
            pltpu.VMEM((SUBLANES - 1, CONV_HALO + tm - SUBLANES, ca), F32),
            pltpu.VMEM((CONV_B_HALO + tm, cb), F32),
            pltpu.VMEM((tm, ca + cb), BF16),
        ],
        compiler_params=_cparams(("arbitrary", "arbitrary")),
        name="even_mixer",
    )(x, mod, win, caw, cab.reshape(1, ca), nag.reshape(1, ca), nab.reshape(1, ca), cbw, wout,
      g.reshape(1, d), b.reshape(1, d))


def _sublane_replicated(w):
    return jnp.broadcast_to(w[:, None, :], (w.shape[0], SUBLANES, w.shape[1]))


KX_COLS = 256
KX_WIN = 128
MASK_ROWS = 16
QX_ROWS = HEAD_DIM + 2 * MASK_ROWS
VX_ROWS = HEAD_DIM + 16
LOG2E = math.log2(math.e)


def _odd_proj_kernel(x_ref, mod_ref, w_ref, wt_ref, kx_ref, kvc_ref, u_ref, qt_ref, vx_ref, gt_ref):
    tm = x_ref.shape[1]
    x = x_ref[0]
    mod = mod_ref[0, 0, 0]
    h = (x * (1.0 + mod[1:2]) + mod[0:1]).astype(BF16)
    p = _dot(h, w_ref[...])
    c1 = KX_COLS
    c2 = c1 + 2 * HEAD_DIM
    row = lax.broadcasted_iota(jnp.int32, (tm, KX_COLS), 0)
    col = lax.broadcasted_iota(jnp.int32, (tm, KX_COLS), 1)
    first = HEAD_DIM + MASK_ROWS * ((row // ATT_K) % 2)
    member = (col - first == (row % ATT_K) // SLC_BLOCK) & (col >= first)
    kx_ref[0] = jnp.where(member, 1.0, p[:, 0:c1]).astype(BF16)
    kvc_ref[0] = p[:, c1:c2]
    u_ref[0] = p[:, c2:]
    pt = _dot_nt(wt_ref[...], h)
    r1 = NSA_Q
    r2 = r1 + 2 * VX_ROWS
    qt_ref[0] = (pt[0:r1] * (HEAD_DIM ** -0.5 * LOG2E)).astype(BF16)
    vrow = lax.broadcasted_iota(jnp.int32, (2 * VX_ROWS, tm), 0)
    vx_ref[0] = jnp.where(vrow % VX_ROWS == HEAD_DIM, 1.0, pt[r1:r2]).astype(BF16)
    gt_ref[0] = jax.nn.sigmoid(pt[r2:])


def _odd_proj(x, mod, layer, w, wt):
    bsz, s, d = x.shape
    tm = min(MIX_ROWS, s)

    def blk(c):
        return pl.BlockSpec((1, tm, c), lambda i, t: (i, t, 0))

    def blk_t(r):
        return pl.BlockSpec((1, r, tm), lambda i, t: (i, 0, t))

    return pl.pallas_call(
        _odd_proj_kernel,
        grid=(bsz, s // tm),
        in_specs=[
            blk(d),
            pl.BlockSpec((1, 1, 1, 3, d), lambda i, t: (layer, i, 1, 0, 0)),
            _const_spec(w.shape),
            _const_spec(wt.shape),
        ],
        out_specs=[blk(KX_COLS), blk(2 * HEAD_DIM), blk(POOL_CH),
                   blk_t(NSA_Q), blk_t(2 * VX_ROWS), blk_t(3 * NSA_HEADS)],
        out_shape=[
            jax.ShapeDtypeStruct((bsz, s, KX_COLS), BF16),
            jax.ShapeDtypeStruct((bsz, s, 2 * HEAD_DIM), F32),
            jax.ShapeDtypeStruct((bsz, s, POOL_CH), F32),
            jax.ShapeDtypeStruct((bsz, NSA_Q, s), BF16),
            jax.ShapeDtypeStruct((bsz, 2 * VX_ROWS, s), BF16),
            jax.ShapeDtypeStruct((bsz, 3 * NSA_HEADS, s), F32),
        ],
        compiler_params=_cparams(("arbitrary", "arbitrary")),
        name="odd_proj",
    )(x, mod, w, wt)


def _compress_kernel(x_ref, pe_ref, wt_ref, wb_ref, w2k_ref, w2vt_ref, kc_ref, vct_ref):
    x = x_ref[0]
    top = _dot((x + pe_ref[0:1]).astype(BF16), wt_ref[...])
    bot = _dot((x + pe_ref[1:2]).astype(BF16), wb_ref[...])
    hid = top + pltpu.roll(bot, bot.shape[0] - 1, axis=0)
    act = _silu(hid).astype(BF16)
    kc_ref[0] = _dot(act[:, 0:CMP_HIDDEN], w2k_ref[...]).astype(BF16)
    vct_ref[0] = _dot_nt(w2vt_ref[...], act[:, CMP_HIDDEN:]).astype(BF16)


def _compress(kvc, pe, wt, wb, w2k, w2vt):
    bsz, s, _ = kvc.shape
    nrow = s // CMP_STRIDE
    width = CMP_STRIDE * 2 * HEAD_DIM
    x = kvc.reshape(bsz, nrow, width)
    return pl.pallas_call(
        _compress_kernel,
        grid=(bsz,),
        in_specs=[
            pl.BlockSpec((1, nrow, width), lambda i: (i, 0, 0)),
            _const_spec(pe.shape),
            _const_spec(wt.shape),
            _const_spec(wb.shape),
            _const_spec(w2k.shape),
            _const_spec(w2vt.shape),
        ],
        out_specs=[pl.BlockSpec((1, nrow, HEAD_DIM), lambda i: (i, 0, 0)),
                   pl.BlockSpec((1, HEAD_DIM, nrow), lambda i: (i, 0, 0))],
        out_shape=[jax.ShapeDtypeStruct((bsz, nrow, HEAD_DIM), BF16),
                   jax.ShapeDtypeStruct((bsz, HEAD_DIM, nrow), BF16)],
        compiler_params=_cparams(("arbitrary",)),
        name="compress",
    )(x, pe, wt, wb, w2k, w2vt)


CMP_TAB_ROWS = 240


def _t5_bucket_table(max_dist):
    n = np.arange(max_dist, dtype=np.int64)
    exact = REL_BUCKETS // 2
    nf = np.maximum(n, 1).astype(np.float32)
    large = exact + (np.log(nf / np.float32(exact)) / np.float32(math.log(REL_MAX_DIST / exact))
                     * np.float32(REL_BUCKETS - exact)).astype(np.int32)
    return np.where(n < exact, n, np.minimum(large, REL_BUCKETS - 1)).astype(np.int32)


def _bias_tables(rel_bias):
    far = REL_BUCKETS - 1
    table = _t5_bucket_table(1024)

    def buckets(dist):
        return np.where(dist >= 0, table[np.clip(dist, 0, 1023)], REL_BUCKETS).reshape(-1)

    r = np.arange(128)[:, None]
    c = np.arange(ATT_Q)[None, :]
    jj = np.arange(CMP_TAB_ROWS)[:, None]
    bucket = np.concatenate([buckets(c - r), buckets(c - CMP_STRIDE * (jj - 112) - (CMP_BLOCK - 1))])
    onehot = (jnp.asarray(bucket)[None, :] == jnp.arange(REL_BUCKETS + 1)[:, None]).astype(F32)
    rb = (rel_bias.astype(F32) - rel_bias[far:far + 1].astype(F32)).T * LOG2E
    rb = jnp.concatenate([rb, jnp.full((rb.shape[0], 1), NEG_INF, F32)], axis=1)
    flat = jnp.dot(rb, onehot, precision=lax.Precision.HIGHEST)
    nh = rel_bias.shape[1]
    tile = flat[:, :128 * ATT_Q].reshape(nh, 128, ATT_Q)
    cmp_tab = flat[:, 128 * ATT_Q:].reshape(nh, CMP_TAB_ROWS, ATT_Q)
    return tile, cmp_tab


def _attn_kernel(qt_ref, gt_ref, kc_ref, vct_ref, kx_ref, vx_ref, tile_ref, cmpb_ref, ovt_ref, pm_ref,
                 o_ref, qx, selb, old_mask, acc_s, m_sel, acc_w, m_win, ot):
    nh, hd = NSA_HEADS, HEAD_DIM
    tq, tk = ATT_Q, ATT_K
    i = pl.program_id(1)
    t0 = i * tq
    ncmp = kc_ref.shape[1]

    @pl.when(i == 0)
    def _():
        r = lax.broadcasted_iota(jnp.int32, (tk, tq), 0)
        c = lax.broadcasted_iota(jnp.int32, (tk, tq), 1)
        old_mask[...] = jnp.where(r > c, 0.0, NEG_INF)
        for h in range(nh):
            qx[h, hd:QX_ROWS, :] = jnp.zeros((QX_ROWS - hd, tq), BF16)

    for h in range(nh):
        qx[h, 0:hd, :] = qt_ref[0, h * hd:(h + 1) * hd, :]

    def run_pipelined(steps):
        work, before = [], {}
        for st in steps:
            if callable(st):
                before.setdefault(len(work), []).append(st)
            else:
                work.append(st)
        pending = {}

        def issue(j):
            for f in before.get(j, ()):
                f()
            pending[j] = work[j][0]()

        for j in range(min(HEAD_LOOKAHEAD, len(work))):
            issue(j)
        for j in range(len(work)):
            if j + HEAD_LOOKAHEAD < len(work):
                issue(j + HEAD_LOOKAHEAD)
            work[j][1](pending.pop(j))

    kc = kc_ref[0]
    vct = vct_ref[0]
    start = pl.multiple_of(112 - (tq // CMP_STRIDE) * i, 16)
    tok = t0 + lax.broadcasted_iota(jnp.int32, (1, tq), 1)
    any_valid = jnp.where(tok >= CMP_BLOCK - 1, 1.0, 0.0)
    psum = [jnp.zeros((ncmp, tq), F32)]

    def cmp_logits(h):
        return _dot(kc, qx[h, 0:hd, :]) + cmpb_ref[h, pl.ds(start, ncmp), :]

    def cmp_softmax(h, s):
        e = jnp.exp2(s - jnp.max(s, axis=0, keepdims=True))
        prob = e * (any_valid / jnp.sum(e, axis=0, keepdims=True))
        psum[0] = psum[0] + prob
        ot[h] = gt_ref[0, 0, h] * _dot(vct, prob.astype(BF16))

    cmp_steps = [(functools.partial(cmp_logits, h), functools.partial(cmp_softmax, h)) for h in range(nh)]

    def select_blocks():
        p_hi = psum[0].astype(BF16)
        p_lo = (psum[0] - p_hi.astype(F32)).astype(BF16)
        ovt = ovt_ref[...]
        imp = _dot(ovt, p_hi) + _dot(ovt, p_lo)
        nslc = ovt.shape[0]
        blk = lax.broadcasted_iota(jnp.int32, (nslc, tq), 0)
        cur = (t0 + lax.broadcasted_iota(jnp.int32, (nslc, tq), 1)) // SLC_BLOCK
        forced = (blk == 0) | (blk == cur) | (blk == cur - 1)
        val = jnp.where(forced, FORCE, jnp.where(blk <= cur, imp, -FORCE))
        rank = jnp.zeros((nslc, tq), F32)
        for k in range(nslc):
            vk = val[k:k + 1, :]
            rank = rank + jnp.where(blk > k, jnp.where(vk >= val, 1.0, 0.0), jnp.where(vk > val, 1.0, 0.0))
        selbias = jnp.where(rank < float(min(SLC_TOPK, nslc)), 0.0, NEG_INF).astype(BF16)
        selb[...] = _dot(pm_ref[...], selbias).reshape(selb.shape).astype(BF16)

    def set_sel_rows(c):
        tile = selb[c]
        off = hd + MASK_ROWS * (c % 2)
        if not isinstance(off, int):
            off = pl.multiple_of(off, MASK_ROWS)
        for h in range(nh):
            qx[h, pl.ds(off, MASK_ROWS), :] = tile

    def chunk_steps(row0, selected, mode, first):
        row0 = pl.multiple_of(row0, tk)
        acc, m_run = (acc_s, m_sel) if selected else (acc_w, m_win)

        def logits(h):
            if selected:
                s = _dot(kx_ref[0, pl.ds(row0, tk), 0:QX_ROWS], qx[h])
            else:
                s = _dot(kx_ref[0, pl.ds(row0, tk), KX_WIN:KX_WIN + hd], qx[h, 0:hd, :])
            if mode == "cur":
                t = tile_ref[h]
                bot = jnp.concatenate([jnp.full((128, 128), NEG_INF, F32),
                                       s[128:256, 128:256] + t[:, 0:128]], axis=1)
                s = jnp.concatenate([s[0:128] + t, bot], axis=0)
            elif mode == "prev":
                t = tile_ref[h]
                bot = jnp.concatenate([s[128:256, 0:128] + t[:, 128:256], s[128:256, 128:256]], axis=1)
                s = jnp.concatenate([s[0:128], bot], axis=0)
            elif mode == "old":
                s = s + old_mask[...]
            return s

        def update(h, s):
            if selected:
                vx = vx_ref[0, 0:VX_ROWS, pl.ds(row0, tk)]
            else:
                vx = vx_ref[0, VX_ROWS:2 * VX_ROWS, pl.ds(row0, tk)]
            mc = jnp.max(s, axis=0, keepdims=True)
            if first:
                m_new = mc
                acc[h] = _dot(vx, jnp.exp2(s - m_new).astype(BF16))
            else:
                m_old = m_run[h]
                m_new = jnp.maximum(m_old, mc)
                pv = _dot(vx, jnp.exp2(s - m_new).astype(BF16))
                acc[h] = jnp.exp2(m_old - m_new) * acc[h] + pv
            m_run[h] = m_new

        return [(functools.partial(logits, h), functools.partial(update, h)) for h in range(nh)]

    run_pipelined(cmp_steps
                  + chunk_steps(t0, False, "cur", True)
                  + [select_blocks, functools.partial(set_sel_rows, i)]
                  + chunk_steps(t0, True, "cur", True))

    def prev_steps():
        return ([functools.partial(set_sel_rows, i - 1)]
                + chunk_steps(t0 - tk, True, "prev", False)
                + chunk_steps(t0 - tk, False, "prev", False))

    @pl.when(i == 1)
    def _():
        run_pipelined(prev_steps())

    @pl.when(i >= 2)
    def _():
        run_pipelined(prev_steps() + chunk_steps(t0 - 2 * tk, False, "old", False))

    def far_steps(c):
        return chunk_steps(c * tk, True, "far", False)

    n_far = jnp.maximum(i - 1, 0)

    def far_pair(j, carry):
        set_sel_rows(2 * j)
        set_sel_rows(2 * j + 1)
        run_pipelined(far_steps(2 * j) + far_steps(2 * j + 1))
        return carry

    lax.fori_loop(0, n_far // 2, far_pair, 0)

    @pl.when(n_far % 2 == 1)
    def _():
        set_sel_rows(n_far - 1)
        run_pipelined(far_steps(n_far - 1))

    for h in range(nh):
        a_s = acc_s[h]
        a_w = acc_w[h]
        ot[h] = (ot[h] + gt_ref[0, 1, h] * (a_s[0:hd] / a_s[hd:hd + 1])
                 + gt_ref[0, 2, h] * (a_w[0:hd] / a_w[hd:hd + 1]))
    o_ref[0] = ot[...].reshape(nh * hd, tq).T.astype(BF16)


def _attention(qt, gt, kc, vct, kx, vx, tile, cmp_tab, ovt, pm):
    bsz, _, s = qt.shape
    nh, hd, tq, tk = NSA_HEADS, HEAD_DIM, ATT_Q, ATT_K
    ncmp = kc.shape[1]
    nchunk = s // tk
    return pl.pallas_call(
        _attn_kernel,
        grid=(bsz, s // tq),
        in_specs=[
            pl.BlockSpec((1, NSA_Q, tq), lambda b, i: (b, 0, i)),
            pl.BlockSpec((1, 3, nh, 1, tq), lambda b, i: (b, 0, 0, 0, i)),
            pl.BlockSpec((1, ncmp, hd), lambda b, i: (b, 0, 0)),
            pl.BlockSpec((1, hd, ncmp), lambda b, i: (b, 0, 0)),
            pl.BlockSpec((1, kx.shape[1], KX_COLS), lambda b, i: (b, 0, 0)),
            pl.BlockSpec((1, 2 * VX_ROWS, vx.shape[2]), lambda b, i: (b, 0, 0)),
            _const_spec(tile.shape),
            _const_spec(cmp_tab.shape),
            _const_spec(ovt.shape),
            _const_spec(pm.shape),
        ],
        out_specs=pl.BlockSpec((1, tq, NSA_Q), lambda b, i: (b, i, 0)),
        out_shape=jax.ShapeDtypeStruct((bsz, s, NSA_Q), BF16),
        scratch_shapes=[
            pltpu.VMEM((nh, QX_ROWS, tq), BF16),
            pltpu.VMEM((nchunk, 16, tq), BF16),
            pltpu.VMEM((tk, tq), F32),
            pltpu.VMEM((nh, VX_ROWS, tq), F32),
            pltpu.VMEM((nh, 1, tq), F32),
            pltpu.VMEM((nh, VX_ROWS, tq), F32),
            pltpu.VMEM((nh, 1, tq), F32),
            pltpu.VMEM((nh, hd, tq), F32),
        ],
        compiler_params=_cparams(("arbitrary", "arbitrary")),
        name="nsa_attention",
    )(qt, gt, kc, vct, kx, vx, tile, cmp_tab, ovt, pm)


def _odd_out_kernel(x_ref, mod_ref, o_ref_in, u_ref, pw_ref, ps_ref, wo_ref, wp_ref, g_ref, b_ref, o_ref, ubuf):
    tm = x_ref.shape[1]
    t = pl.program_id(1)

    @pl.when(t == 0)
    def _():
        ubuf[0:POOL_HALO, :] = jnp.zeros((POOL_HALO, POOL_CH), F32)

    u = u_ref[0]
    ubuf[POOL_HALO:POOL_HALO + tm, :] = u

    lane_group = lax.broadcasted_iota(jnp.int32, (tm, POOL_CH), 1) // POOL_GROUP_CH
    pos = t * tm + lax.broadcasted_iota(jnp.int32, (tm, POOL_CH), 0)
    total = jnp.zeros((tm, POOL_CH), F32)
    width = jnp.ones((tm, POOL_CH), F32)
    sw = ubuf[...]
    span = 1
    for gi, w in enumerate(POOL_WINDOWS):
        while span < w:
            sw = sw + pltpu.roll(sw, span, axis=0)
            span *= 2
        total = jnp.where(lane_group == gi, sw[POOL_HALO:], total)
        width = jnp.where(lane_group == gi, float(w), width)
    cnt = jnp.minimum((pos + 1).astype(F32), width)
    dlt = (total / cnt - u).astype(BF16)
    ubuf[0:POOL_HALO, :] = ubuf[tm:tm + POOL_HALO, :]

    o_pool = (_dot(dlt, pw_ref[...]) * ps_ref[...]).astype(BF16)
    y = _dot(o_ref_in[0], wo_ref[...]) + _dot(o_pool, wp_ref[...])
    x = x_ref[0]
    mod = mod_ref[0, 0, 0]
    z = DN_ALPHA * x + (1.0 + mod[2:3]) * y
    o_ref[0] = _layer_norm(z, g_ref[...], b_ref[...])


def _odd_out(x, mod, layer, o_nsa, u, pw, ps, wo, wp, g, b):
    bsz, s, d = x.shape
    tm = min(MIX_ROWS, s)
    return pl.pallas_call(
        _odd_out_kernel,
        grid=(bsz, s // tm),
        in_specs=[
            pl.BlockSpec((1, tm, d), lambda i, t: (i, t, 0)),
            pl.BlockSpec((1, 1, 1, 3, d), lambda i, t: (layer, i, 1, 0, 0)),
            pl.BlockSpec((1, tm, NSA_Q), lambda i, t: (i, t, 0)),
            pl.BlockSpec((1, tm, POOL_CH), lambda i, t: (i, t, 0)),
            _const_spec(pw.shape),
            _const_spec((1, POOL_CH)),
            _const_spec(wo.shape),
            _const_spec(wp.shape),
            _const_spec((1, d)),
            _const_spec((1, d)),
        ],
        out_specs=pl.BlockSpec((1, tm, d), lambda i, t: (i, t, 0)),
        out_shape=jax.ShapeDtypeStruct(x.shape, F32),
        scratch_shapes=[pltpu.VMEM((POOL_HALO + tm, POOL_CH), F32)],
        compiler_params=_cparams(("arbitrary", "arbitrary")),
        name="odd_out",
    )(x, mod, o_nsa, u, pw, ps.reshape(1, POOL_CH), wo, wp, g.reshape(1, d), b.reshape(1, d))


def _odd_in_weights(w_in):
    hd, nh = HEAD_DIM, NSA_HEADS
    c = NSA_Q
    col = lambda k: w_in[:, c + k * hd:c + (k + 1) * hd]
    kc, vc, ks, vs, kw, vw = (col(k) for k in range(6))
    gates = w_in[:, c + 6 * hd:c + 6 * hd + 3 * nh]
    u = w_in[:, c + 6 * hd + 3 * nh:]
    perm = np.array([3 * h + j for j in range(3) for h in range(nh)])
    zc = lambda n: jnp.zeros((w_in.shape[0], n), w_in.dtype)
    w_std = jnp.concatenate([ks, zc(KX_WIN - hd), kw, zc(KX_COLS - KX_WIN - hd), kc, vc, u], axis=1)
    w_t = jnp.concatenate([w_in[:, 0:c], vs, zc(VX_ROWS - hd), vw, zc(VX_ROWS - hd), gates[:, perm]], axis=1)
    return w_std.astype(BF16), w_t.T.astype(BF16)


def _compress_weights(pe_k, pe_v, w1_k, w1_v, w2_k, w2_v):
    hd, hid, half = HEAD_DIM, CMP_HIDDEN, CMP_STRIDE

    def halves(w1):
        w = w1.reshape(CMP_BLOCK, hd, hid)
        return w[:half], w[half:]

    kt, kb = halves(w1_k)
    vt, vb = halves(w1_v)
    z = jnp.zeros((half, hd, hid), F32)

    def assemble(k_part, v_part):
        k_rows = jnp.concatenate([k_part, z], axis=2)
        v_rows = jnp.concatenate([z, v_part], axis=2)
        return jnp.concatenate([k_rows, v_rows], axis=1).reshape(half * 2 * hd, 2 * hid).astype(BF16)

    wt = assemble(kt, vt)
    wb = assemble(kb, vb)
    pe = jnp.concatenate([pe_k.reshape(2, half, hd), pe_v.reshape(2, half, hd)], axis=2)
    pe = pe.reshape(2, half * 2 * hd)
    return pe, wt, wb, w2_k.astype(BF16), w2_v.T.astype(BF16)


def _overlap_matrix_t(n_cols, n_slc):
    n_cmp = n_slc * SLC_BLOCK // CMP_STRIDE - 1
    cmp_start = np.arange(n_cmp) * CMP_STRIDE
    slc_start = np.arange(n_slc) * SLC_BLOCK
    ov = np.clip(np.minimum(cmp_start[:, None] + CMP_BLOCK, slc_start[None, :] + SLC_BLOCK)
                 - np.maximum(cmp_start[:, None], slc_start[None, :]), 0, None) / CMP_BLOCK
    out = np.zeros((n_slc, n_cols), np.float32)
    out[:, :n_cmp] = ov.T
    return jnp.asarray(out, BF16)


def _chunk_row_placement(n_chunk, n_slc):
    per = ATT_K // SLC_BLOCK
    pm = np.zeros((n_chunk * 16, n_slc), np.float32)
    for c in range(n_chunk):
        for b in range(per):
            pm[16 * c + b, per * c + b] = 1.0
    return jnp.asarray(pm, BF16)


def _pool_weight(pool_w):
    ng, gc = len(POOL_WINDOWS), POOL_GROUP_CH
    w = jnp.zeros((ng * gc, ng * gc), F32)
    for gi in range(ng):
        w = w.at[gi * gc:(gi + 1) * gc, gi * gc:(gi + 1) * gc].set(pool_w[gi])
    return w.astype(BF16)


def _odd_mixer(x, mod, layer, w_in, pe_k, pe_v, w1_k, w2_k, w1_v, w2_v, pool_w, pool_scale, w_out,
               bias_tables, g, b):
    bsz, s, _ = x.shape
    assert s % ATT_Q == 0 and s // CMP_STRIDE == LANES and ATT_Q == ATT_K and MIX_ROWS % ATT_K == 0
    w_std, w_t = _odd_in_weights(w_in)
    kx, kvc, u, qt, vx, gt = _odd_proj(x, mod, layer, w_std, w_t)
    kc, vct = _compress(kvc, *_compress_weights(pe_k, pe_v, w1_k, w1_v, w2_k, w2_v))
    tile, cmp_tab = bias_tables
    ovt = _overlap_matrix_t(s // CMP_STRIDE, s // SLC_BLOCK)
    pm = _chunk_row_placement(s // ATT_K, s // SLC_BLOCK)
    o_nsa = _attention(qt, gt.reshape(bsz, 3, NSA_HEADS, 1, s), kc, vct, kx, vx, tile, cmp_tab, ovt, pm)
    return _odd_out(x, mod, layer, o_nsa, u, _pool_weight(pool_w), pool_scale,
                    w_out[:NSA_Q].astype(BF16), w_out[NSA_Q:].astype(BF16), g, b)


def kernel(x, c, ada_w, ada_b, ln_g, ln_b, ffn_w_gate, ffn_w_up, ffn_w_down, ev_w_in, ev_conv_a_w,
           ev_conv_a_b, ev_norm_a_g, ev_norm_a_b, ev_conv_b_w, ev_w_out, od_w_in, od_cmp_pe_k,
           od_cmp_pe_v, od_cmp_w1_k, od_cmp_w2_k, od_cmp_w1_v, od_cmp_w2_v, od_pool_w, od_pool_scale,
           od_w_out, rel_bias):
    bsz, s, d = x.shape
    depth = ada_w.shape[0]
    mod = _modulation(c, ada_w, ada_b).reshape(depth, bsz, 3, 3, d)
    bias_tables = _bias_tables(rel_bias)
    wg = ffn_w_gate.astype(BF16)
    wu = ffn_w_up.astype(BF16)
    wd = ffn_w_down.astype(BF16)
    for layer in range(depth):
        j = layer // 2
        x = _ffn(x, mod, layer, 0, 0, wg, wu, wd, ln_g[layer, 0], ln_b[layer, 0])
        if layer % 2 == 0:
            x = _even_mixer(x, mod, layer, ev_w_in[j].astype(BF16), _sublane_replicated(ev_conv_a_w[j]),
                            ev_conv_a_b[j],
                            ev_norm_a_g[j], ev_norm_a_b[j], ev_conv_b_w[j], ev_w_out[j].astype(BF16),
                            ln_g[layer, 1], ln_b[layer, 1])
        else:
            x = _odd_mixer(x, mod, layer, od_w_in[j], od_cmp_pe_k[j], od_cmp_pe_v[j], od_cmp_w1_k[j],
                           od_cmp_w2_k[j], od_cmp_w1_v[j], od_cmp_w2_v[j], od_pool_w[j], od_pool_scale[j],
                           od_w_out[j], bias_tables, ln_g[layer, 1], ln_b[layer, 1])
        x = _ffn(x, mod, layer, 2, 1, wg, wu, wd, ln_g[layer, 2], ln_b[layer, 2])
    return x
```

```python
import functools
import math

import numpy as np
import jax
import jax.numpy as jnp
from jax import lax
from jax.experimental import pallas as pl
from jax.experimental.pallas import tpu as pltpu

F32 = jnp.float32
BF16 = jnp.bfloat16

D_MODEL = 1024
DEPTH = 4
D_FF = 2816
FFN_RES_WEIGHT = 0.5
CONV_A_CH = 512
CONV_A_WIDTH = 31
CONV_B_CH = 512
CONV_B_WIDTH = 3
NSA_HEADS = 16
HEAD_DIM = 64
CMP_BLOCK = 32
CMP_STRIDE = 16
CMP_HIDDEN = 128
SLC_BLOCK = 64
SLC_TOPK = 16
WINDOW = 512
POOL_WINDOWS = (2, 4, 8, 16)
POOL_GROUP_CH = 64
POOL_CH = len(POOL_WINDOWS) * POOL_GROUP_CH
REL_BUCKETS = 32
REL_MAX_DIST = 128
DN_ALPHA = (2 * DEPTH) ** 0.25
LN_EPS = 1e-5
NSA_Q = NSA_HEADS * HEAD_DIM
NEG_INF = -1e30
FORCE = 1e30

LANES = 128
SUBLANES = 8
VMEM_LIMIT_BYTES = 56 * 1024 * 1024

FFN_ROWS = 1024
FFN_SUB_ROWS = 256
MIX_ROWS = 512
EVEN_OUT_ROWS = 256
ATT_Q = 256
ATT_K = 256
CONV_HALO = 32
CONV_B_HALO = 8
POOL_HALO = 16
HEAD_LOOKAHEAD = 6


def _cparams(sem):
    return pltpu.CompilerParams(dimension_semantics=sem, vmem_limit_bytes=VMEM_LIMIT_BYTES)


def _const_spec(shape):
    n = len(shape)
    return pl.BlockSpec(shape, lambda *_: (0,) * n, pipeline_mode=pl.Buffered(1))


def _layer_norm(z, g, b):
    mu = jnp.mean(z, axis=-1, keepdims=True)
    zc = z - mu
    var = jnp.mean(zc * zc, axis=-1, keepdims=True)
    return zc * lax.rsqrt(var + LN_EPS) * g + b


def _silu(v):
    return v * jax.nn.sigmoid(v)


def _dot(a, b):
    return jnp.dot(a, b, preferred_element_type=F32)


def _dot_nt(a, b):
    return lax.dot_general(a, b, (((1,), (1,)), ((), ())), preferred_element_type=F32)


def _mod_kernel(c_ref, w_ref, b_ref, o_ref):
    cond = _silu(c_ref[...])
    o_ref[0] = _dot(cond.astype(BF16), w_ref[0].astype(BF16)) + b_ref[0]


def _modulation(c, ada_w, ada_b):
    depth, d, n = ada_w.shape
    bsz = c.shape[0]
    tn = 1152
    return pl.pallas_call(
        _mod_kernel,
        grid=(depth, n // tn),
        in_specs=[
            pl.BlockSpec((bsz, d), lambda l, j: (0, 0)),
            pl.BlockSpec((1, d, tn), lambda l, j: (l, 0, j)),
            pl.BlockSpec((1, 1, tn), lambda l, j: (l, 0, j)),
        ],
        out_specs=pl.BlockSpec((1, bsz, tn), lambda l, j: (l, 0, j)),
        out_shape=jax.ShapeDtypeStruct((depth, bsz, n), F32),
        compiler_params=_cparams(("arbitrary", "arbitrary")),
        name="modulation",
    )(c, ada_w, ada_b.reshape(depth, 1, n))


def _ffn_kernel(*refs, cast_next):
    if cast_next:
        (x_ref, mod_ref, wg_ref, wu_ref, wd_ref, g_ref, b_ref, ng_ref, nu_ref, nd_ref,
         o_ref, og_ref, ou_ref, od_ref) = refs
        og_ref[...] = ng_ref[0, 0].astype(BF16)
        ou_ref[...] = nu_ref[0, 0].astype(BF16)
        od_ref[...] = nd_ref[0, 0].astype(BF16)
    else:
        x_ref, mod_ref, wg_ref, wu_ref, wd_ref, g_ref, b_ref, o_ref = refs
    mod = mod_ref[0, 0, 0]
    tm = x_ref.shape[1]
    for r0 in range(0, tm, FFN_SUB_ROWS):
        x = x_ref[0, r0:r0 + FFN_SUB_ROWS, :]
        h = (x * (1.0 + mod[1:2]) + mod[0:1]).astype(BF16)
        a = (_silu(_dot(h, wg_ref[...])) * _dot(h, wu_ref[...])).astype(BF16)
        y = _dot(a, wd_ref[...])
        z = DN_ALPHA * x + (FFN_RES_WEIGHT * (1.0 + mod[2:3])) * y
        o_ref[0, r0:r0 + FFN_SUB_ROWS, :] = _layer_norm(z, g_ref[...], b_ref[...])


def _ffn(x, mod, layer, sub, weights, g, b, nxt=None):
    bsz, s, d = x.shape
    wg, wu, wd = weights
    ff = wg.shape[1]
    tm = min(FFN_ROWS, s)
    nt = s // tm
    steps = bsz * nt
    in_specs = [
        pl.BlockSpec((1, tm, d), lambda i, t: (i, t, 0)),
        pl.BlockSpec((1, 1, 1, 3, d), lambda i, t: (layer, i, sub, 0, 0)),
        _const_spec((d, ff)),
        _const_spec((d, ff)),
        _const_spec((ff, d)),
        _const_spec((1, d)),
        _const_spec((1, d)),
    ]
    out_specs = [pl.BlockSpec((1, tm, d), lambda i, t: (i, t, 0))]
    out_shape = [jax.ShapeDtypeStruct(x.shape, F32)]
    args = [x, mod, wg, wu, wd, g.reshape(1, d), b.reshape(1, d)]
    if nxt is not None:
        nl, nh, w_gate, w_up, w_down = nxt
        up_rows = d // steps
        down_steps = steps // 2
        down_rows = ff // down_steps
        assert d % steps == 0 and up_rows % 16 == 0 and ff % down_steps == 0 and down_rows % 16 == 0
        in_specs += [
            pl.BlockSpec((1, 1, up_rows, ff), lambda i, t: (nl, nh, i * nt + t, 0)),
            pl.BlockSpec((1, 1, up_rows, ff), lambda i, t: (nl, nh, i * nt + t, 0)),
            pl.BlockSpec((1, 1, down_rows, d), lambda i, t: (nl, nh, (i * nt + t) // 2, 0)),
        ]
        out_specs += [
            pl.BlockSpec((up_rows, ff), lambda i, t: (i * nt + t, 0)),
            pl.BlockSpec((up_rows, ff), lambda i, t: (i * nt + t, 0)),
            pl.BlockSpec((down_rows, d), lambda i, t: ((i * nt + t) // 2, 0)),
        ]
        out_shape += [jax.ShapeDtypeStruct((d, ff), BF16), jax.ShapeDtypeStruct((d, ff), BF16),
                      jax.ShapeDtypeStruct((ff, d), BF16)]
        args += [w_gate, w_up, w_down]
    out = pl.pallas_call(
        functools.partial(_ffn_kernel, cast_next=nxt is not None),
        grid=(bsz, nt),
        in_specs=in_specs,
        out_specs=out_specs,
        out_shape=out_shape,
        compiler_params=_cparams(("arbitrary", "arbitrary")),
        name="ffn",
    )(*args)
    return out[0], (tuple(out[1:]) if nxt is not None else None)


def _even_kernel(x_ref, mod_ref, win_ref, caw_ref, cab_ref, nag_ref, nab_ref, cbw_ref, wout_ref,
                 g_ref, b_ref, o_ref, abuf, ash, bbuf, cat):
    tm = x_ref.shape[1]
    ca, cb = CONV_A_CH, CONV_B_CH

    @pl.when(pl.program_id(1) == 0)
    def _():
        abuf[0:CONV_HALO, :] = jnp.zeros((CONV_HALO, ca), F32)
        bbuf[0:CONV_B_HALO, :] = jnp.zeros((CONV_B_HALO, cb), F32)

    x = x_ref[0]
    mod = mod_ref[0, 0, 0]
    h = (x * (1.0 + mod[1:2]) + mod[0:1]).astype(BF16)
    p = _dot(h, win_ref[...])
    abuf[CONV_HALO:CONV_HALO + tm, :] = p[:, 0:ca] * jax.nn.sigmoid(p[:, ca:2 * ca])
    gate_b = p[:, 2 * ca:2 * ca + cb]
    bbuf[CONV_B_HALO:CONV_B_HALO + tm, :] = p[:, 2 * ca + cb:2 * ca + 2 * cb] * p[:, 2 * ca + 2 * cb:]

    span = tm + CONV_HALO - SUBLANES
    for r in range(1, SUBLANES):
        ash[r - 1] = abuf[r:r + span, :]

    rows = 64
    for r0 in range(0, tm, rows):
        acc = jnp.zeros((rows // SUBLANES, SUBLANES, ca), F32)
        for k in range(CONV_A_WIDTH):
            off = CONV_HALO + r0 - (CONV_A_WIDTH - 1) + k
            r = off % SUBLANES
            tap = abuf[off:off + rows, :] if r == 0 else ash[r - 1, off - r:off - r + rows, :]
            acc = acc + caw_ref[k][None] * tap.reshape(rows // SUBLANES, SUBLANES, ca)
        acc = acc.reshape(rows, ca)
        u = _layer_norm(acc + cab_ref[...], nag_ref[...], nab_ref[...])
        cat[r0:r0 + rows, 0:ca] = _silu(u).astype(BF16)
        accb = jnp.zeros((rows, cb), F32)
        for k in range(CONV_B_WIDTH):
            off = CONV_B_HALO + r0 - (CONV_B_WIDTH - 1) + k
            accb = accb + cbw_ref[k:k + 1, :] * bbuf[off:off + rows, :]
        cat[r0:r0 + rows, ca:ca + cb] = (gate_b[r0:r0 + rows] * accb).astype(BF16)

        s1 = r0 + rows
        if s1 % EVEN_OUT_ROWS == 0:
            s0 = s1 - EVEN_OUT_ROWS
            y = _dot(cat[s0:s1, :], wout_ref[...])
            z = DN_ALPHA * x[s0:s1] + (1.0 + mod[2:3]) * y
            o_ref[0, s0:s1, :] = _layer_norm(z, g_ref[...], b_ref[...])

    abuf[0:CONV_HALO, :] = abuf[tm:tm + CONV_HALO, :]
    bbuf[0:CONV_B_HALO, :] = bbuf[tm:tm + CONV_B_HALO, :]


def _even_mixer(x, mod, layer, win, caw, cab, nag, nab, cbw, wout, g, b):
    bsz, s, d = x.shape
    tm = min(MIX_ROWS, s)
    ca, cb = CONV_A_CH, CONV_B_CH
    return pl.pallas_call(
        _even_kernel,
        grid=(bsz, s // tm),
        in_specs=[
            pl.BlockSpec((1, tm, d), lambda i, t: (i, t, 0)),
            pl.BlockSpec((1, 1, 1, 3, d), lambda i, t: (layer, i, 1, 0, 0)),
            _const_spec(win.shape),
            _const_spec(caw.shape),
            _const_spec((1, ca)),
            _const_spec((1, ca)),
            _const_spec((1, ca)),
            _const_spec(cbw.shape),
            _const_spec(wout.shape),
            _const_spec((1, d)),
            _const_spec((1, d)),
        ],
        out_specs=pl.BlockSpec((1, tm, d), lambda i, t: (i, t, 0)),
        out_shape=jax.ShapeDtypeStruct(x.shape, F32),
        scratch_shapes=[
            pltpu.VMEM((CONV_HALO + tm, ca), F32),
            pltpu.VMEM((SUBLANES - 1, CONV_HALO + tm - SUBLANES, ca), F32),
            pltpu.VMEM((CONV_B_HALO + tm, cb), F32),
            pltpu.VMEM((tm, ca + cb), BF16),
        ],
        compiler_params=_cparams(("arbitrary", "arbitrary")),
        name="even_mixer",
    )(x, mod, win, caw, cab.reshape(1, ca), nag.reshape(1, ca), nab.reshape(1, ca), cbw, wout,
      g.reshape(1, d), b.reshape(1, d))


def _sublane_replicated(w):
    return jnp.broadcast_to(w[:, None, :], (w.shape[0], SUBLANES, w.shape[1]))


KX_COLS = 256
KX_WIN = 128
MASK_ROWS = 16
QX_ROWS = HEAD_DIM + 2 * MASK_ROWS
VX_ROWS = HEAD_DIM + 16
LOG2E = math.log2(math.e)


def _odd_proj_kernel(x_ref, mod_ref, w_ref, wt_ref, kx_ref, kvc_ref, u_ref, qt_ref, vx_ref, gt_ref):
    tm = x_ref.shape[1]
    x = x_ref[0]
    mod = mod_ref[0, 0, 0]
    h = (x * (1.0 + mod[1:2]) + mod[0:1]).astype(BF16)
    p = _dot(h, w_ref[...])
    c1 = KX_COLS
    c2 = c1 + 2 * HEAD_DIM
    row = lax.broadcasted_iota(jnp.int32, (tm, KX_COLS), 0)
    col = lax.broadcasted_iota(jnp.int32, (tm, KX_COLS), 1)
    first = HEAD_DIM + MASK_ROWS * ((row // ATT_K) % 2)
    member = (col - first == (row % ATT_K) // SLC_BLOCK) & (col >= first)
    kx_ref[0] = jnp.where(member, 1.0, p[:, 0:c1]).astype(BF16)
    kvc_ref[0] = p[:, c1:c2]
    u_ref[0] = p[:, c2:]
    pt = _dot_nt(wt_ref[...], h)
    r1 = NSA_Q
    r2 = r1 + 2 * VX_ROWS
    qt_ref[0] = (pt[0:r1] * (HEAD_DIM ** -0.5 * LOG2E)).astype(BF16)
    vrow = lax.broadcasted_iota(jnp.int32, (2 * VX_ROWS, tm), 0)
    vx_ref[0] = jnp.where(vrow % VX_ROWS == HEAD_DIM, 1.0, pt[r1:r2]).astype(BF16)
    gt_ref[0] = jax.nn.sigmoid(pt[r2:])


def _odd_proj(x, mod, layer, w, wt):
    bsz, s, d = x.shape
    tm = min(MIX_ROWS, s)

    def blk(c):
        return pl.BlockSpec((1, tm, c), lambda i, t: (i, t, 0))

    def blk_t(r):
        return pl.BlockSpec((1, r, tm), lambda i, t: (i, 0, t))

    return pl.pallas_call(
        _odd_proj_kernel,
        grid=(bsz, s // tm),
        in_specs=[
            blk(d),
            pl.BlockSpec((1, 1, 1, 3, d), lambda i, t: (layer, i, 1, 0, 0)),
            _const_spec(w.shape),
            _const_spec(wt.shape),
        ],
        out_specs=[blk(KX_COLS), blk(2 * HEAD_DIM), blk(POOL_CH),
                   blk_t(NSA_Q), blk_t(2 * VX_ROWS), blk_t(3 * NSA_HEADS)],
        out_shape=[
            jax.ShapeDtypeStruct((bsz, s, KX_COLS), BF16),
            jax.ShapeDtypeStruct((bsz, s, 2 * HEAD_DIM), F32),
            jax.ShapeDtypeStruct((bsz, s, POOL_CH), F32),
            jax.ShapeDtypeStruct((bsz, NSA_Q, s), BF16),
            jax.ShapeDtypeStruct((bsz, 2 * VX_ROWS, s), BF16),
            jax.ShapeDtypeStruct((bsz, 3 * NSA_HEADS, s), F32),
        ],
        compiler_params=_cparams(("arbitrary", "arbitrary")),
        name="odd_proj",
    )(x, mod, w, wt)


def _compress_kernel(x_ref, pe_ref, wt_ref, wb_ref, w2k_ref, w2vt_ref, kc_ref, vct_ref):
    x = x_ref[0]
    top = _dot((x + pe_ref[0:1]).astype(BF16), wt_ref[...])
    bot = _dot((x + pe_ref[1:2]).astype(BF16), wb_ref[...])
    hid = top + pltpu.roll(bot, bot.shape[0] - 1, axis=0)
    act = _silu(hid).astype(BF16)
    kc_ref[0] = _dot(act[:, 0:CMP_HIDDEN], w2k_ref[...]).astype(BF16)
    vct_ref[0] = _dot_nt(w2vt_ref[...], act[:, CMP_HIDDEN:]).astype(BF16)


def _compress(kvc, pe, wt, wb, w2k, w2vt):
    bsz, s, _ = kvc.shape
    nrow = s // CMP_STRIDE
    width = CMP_STRIDE * 2 * HEAD_DIM
    x = kvc.reshape(bsz, nrow, width)
    return pl.pallas_call(
        _compress_kernel,
        grid=(bsz,),
        in_specs=[
            pl.BlockSpec((1, nrow, width), lambda i: (i, 0, 0)),
            _const_spec(pe.shape),
            _const_spec(wt.shape),
            _const_spec(wb.shape),
            _const_spec(w2k.shape),
            _const_spec(w2vt.shape),
        ],
        out_specs=[pl.BlockSpec((1, nrow, HEAD_DIM), lambda i: (i, 0, 0)),
                   pl.BlockSpec((1, HEAD_DIM, nrow), lambda i: (i, 0, 0))],
        out_shape=[jax.ShapeDtypeStruct((bsz, nrow, HEAD_DIM), BF16),
                   jax.ShapeDtypeStruct((bsz, HEAD_DIM, nrow), BF16)],
        compiler_params=_cparams(("arbitrary",)),
        name="compress",
    )(x, pe, wt, wb, w2k, w2vt)


CMP_TAB_ROWS = 240


def _t5_bucket_table(max_dist):
    n = np.arange(max_dist, dtype=np.int64)
    exact = REL_BUCKETS // 2
    nf = np.maximum(n, 1).astype(np.float32)
    large = exact + (np.log(nf / np.float32(exact)) / np.float32(math.log(REL_MAX_DIST / exact))
                     * np.float32(REL_BUCKETS - exact)).astype(np.int32)
    return np.where(n < exact, n, np.minimum(large, REL_BUCKETS - 1)).astype(np.int32)


def _bias_tables(rel_bias):
    far = REL_BUCKETS - 1
    table = _t5_bucket_table(1024)

    def buckets(dist):
        return np.where(dist >= 0, table[np.clip(dist, 0, 1023)], REL_BUCKETS).reshape(-1)

    r = np.arange(128)[:, None]
    c = np.arange(ATT_Q)[None, :]
    jj = np.arange(CMP_TAB_ROWS)[:, None]
    bucket = np.concatenate([buckets(c - r), buckets(c - CMP_STRIDE * (jj - 112) - (CMP_BLOCK - 1))])
    onehot = (jnp.asarray(bucket)[None, :] == jnp.arange(REL_BUCKETS + 1)[:, None]).astype(F32)
    rb = (rel_bias.astype(F32) - rel_bias[far:far + 1].astype(F32)).T * LOG2E
    rb = jnp.concatenate([rb, jnp.full((rb.shape[0], 1), NEG_INF, F32)], axis=1)
    flat = jnp.dot(rb, onehot, precision=lax.Precision.HIGHEST)
    nh = rel_bias.shape[1]
    tile = flat[:, :128 * ATT_Q].reshape(nh, 128, ATT_Q)
    cmp_tab = flat[:, 128 * ATT_Q:].reshape(nh, CMP_TAB_ROWS, ATT_Q)
    return tile, cmp_tab


def _attn_kernel(qt_ref, gt_ref, kc_ref, vct_ref, kx_ref, vx_ref, tile_ref, cmpb_ref, ovt_ref, pm_ref,
                 o_ref, qx, selb, old_mask, acc_s, m_sel, acc_w, m_win, ot):
    nh, hd = NSA_HEADS, HEAD_DIM
    tq, tk = ATT_Q, ATT_K
    i = pl.program_id(1)
    t0 = i * tq
    ncmp = kc_ref.shape[1]

    @pl.when(i == 0)
    def _():
        r = lax.broadcasted_iota(jnp.int32, (tk, tq), 0)
        c = lax.broadcasted_iota(jnp.int32, (tk, tq), 1)
        old_mask[...] = jnp.where(r > c, 0.0, NEG_INF)
        for h in range(nh):
            qx[h, hd:QX_ROWS, :] = jnp.zeros((QX_ROWS - hd, tq), BF16)

    for h in range(nh):
        qx[h, 0:hd, :] = qt_ref[0, h * hd:(h + 1) * hd, :]

    def gate(branch, h):
        r = branch * nh + h
        return gt_ref[0, r:r + 1, :]

    def run_pipelined(steps):
        work, before = [], {}
        for st in steps:
            if callable(st):
                before.setdefault(len(work), []).append(st)
            else:
                work.append(st)
        pending = {}

        def issue(j):
            for f in before.get(j, ()):
                f()
            pending[j] = work[j][0]()

        for j in range(min(HEAD_LOOKAHEAD, len(work))):
            issue(j)
        for j in range(len(work)):
            if j + HEAD_LOOKAHEAD < len(work):
                issue(j + HEAD_LOOKAHEAD)
            work[j][1](pending.pop(j))

    kc = kc_ref[0]
    vct = vct_ref[0]
    start = pl.multiple_of(112 - (tq // CMP_STRIDE) * i, 16)
    tok = t0 + lax.broadcasted_iota(jnp.int32, (1, tq), 1)
    any_valid = jnp.where(tok >= CMP_BLOCK - 1, 1.0, 0.0)
    psum = [jnp.zeros((ncmp, tq), F32)]

    def cmp_logits(h):
        return _dot(kc, qx[h, 0:hd, :]) + cmpb_ref[h, pl.ds(start, ncmp), :]

    def cmp_softmax(h, s):
        e = jnp.exp2(s - jnp.max(s, axis=0, keepdims=True))
        prob = e * (any_valid / jnp.sum(e, axis=0, keepdims=True))
        psum[0] = psum[0] + prob
        ot[h] = gate(0, h) * _dot(vct, prob.astype(BF16))

    cmp_steps = [(functools.partial(cmp_logits, h), functools.partial(cmp_softmax, h)) for h in range(nh)]

    def select_blocks():
        p_hi = psum[0].astype(BF16)
        p_lo = (psum[0] - p_hi.astype(F32)).astype(BF16)
        ovt = ovt_ref[...]
        imp = _dot(ovt, p_hi) + _dot(ovt, p_lo)
        nslc = ovt.shape[0]
        blk = lax.broadcasted_iota(jnp.int32, (nslc, tq), 0)
        cur = (t0 + lax.broadcasted_iota(jnp.int32, (nslc, tq), 1)) // SLC_BLOCK
        forced = (blk == 0) | (blk == cur) | (blk == cur - 1)
        val = jnp.where(forced, FORCE, jnp.where(blk <= cur, imp, -FORCE))
        rank = jnp.zeros((nslc, tq), F32)
        for k in range(nslc):
            vk = val[k:k + 1, :]
            rank = rank + jnp.where(blk > k, jnp.where(vk >= val, 1.0, 0.0), jnp.where(vk > val, 1.0, 0.0))
        selbias = jnp.where(rank < float(min(SLC_TOPK, nslc)), 0.0, NEG_INF).astype(BF16)
        selb[...] = _dot(pm_ref[...], selbias).reshape(selb.shape).astype(BF16)

    def set_sel_rows(c):
        tile = selb[c]
        off = hd + MASK_ROWS * (c % 2)
        if not isinstance(off, int):
            off = pl.multiple_of(off, MASK_ROWS)
        for h in range(nh):
            qx[h, pl.ds(off, MASK_ROWS), :] = tile

    def chunk_steps(row0, selected, mode, first):
        row0 = pl.multiple_of(row0, tk)
        acc, m_run = (acc_s, m_sel) if selected else (acc_w, m_win)

        def logits(h):
            if selected:
                s = _dot(kx_ref[0, pl.ds(row0, tk), 0:QX_ROWS], qx[h])
            else:
                s = _dot(kx_ref[0, pl.ds(row0, tk), KX_WIN:KX_WIN + hd], qx[h, 0:hd, :])
            if mode == "cur":
                t = tile_ref[h]
                bot = jnp.concatenate([jnp.full((128, 128), NEG_INF, F32),
                                       s[128:256, 128:256] + t[:, 0:128]], axis=1)
                s = jnp.concatenate([s[0:128] + t, bot], axis=0)
            elif mode == "prev":
                t = tile_ref[h]
                bot = jnp.concatenate([s[128:256, 0:128] + t[:, 128:256], s[128:256, 128:256]], axis=1)
                s = jnp.concatenate([s[0:128], bot], axis=0)
            elif mode == "old":
                s = s + old_mask[...]
            return s

        def update(h, s):
            if selected:
                vx = vx_ref[0, 0:VX_ROWS, pl.ds(row0, tk)]
            else:
                vx = vx_ref[0, VX_ROWS:2 * VX_ROWS, pl.ds(row0, tk)]
            mc = jnp.max(s, axis=0, keepdims=True)
            if first:
                m_new = mc
                acc[h] = _dot(vx, jnp.exp2(s - m_new).astype(BF16))
            else:
                m_old = m_run[h]
                m_new = jnp.maximum(m_old, mc)
                pv = _dot(vx, jnp.exp2(s - m_new).astype(BF16))
                acc[h] = jnp.exp2(m_old - m_new) * acc[h] + pv
            m_run[h] = m_new

        return [(functools.partial(logits, h), functools.partial(update, h)) for h in range(nh)]

    run_pipelined(cmp_steps
                  + chunk_steps(t0, False, "cur", True)
                  + [select_blocks, functools.partial(set_sel_rows, i)]
                  + chunk_steps(t0, True, "cur", True))

    def prev_steps():
        return ([functools.partial(set_sel_rows, i - 1)]
                + chunk_steps(t0 - tk, True, "prev", False)
                + chunk_steps(t0 - tk, False, "prev", False))

    @pl.when(i == 1)
    def _():
        run_pipelined(prev_steps())

    @pl.when(i >= 2)
    def _():
        run_pipelined(prev_steps() + chunk_steps(t0 - 2 * tk, False, "old", False))

    def far_steps(c):
        return chunk_steps(c * tk, True, "far", False)

    n_far = jnp.maximum(i - 1, 0)

    def far_pair(j, carry):
        set_sel_rows(2 * j)
        set_sel_rows(2 * j + 1)
        run_pipelined(far_steps(2 * j) + far_steps(2 * j + 1))
        return carry

    lax.fori_loop(0, n_far // 2, far_pair, 0)

    @pl.when(n_far % 2 == 1)
    def _():
        set_sel_rows(n_far - 1)
        run_pipelined(far_steps(n_far - 1))

    for h in range(nh):
        a_s = acc_s[h]
        a_w = acc_w[h]
        ot[h] = (ot[h] + gate(1, h) * (a_s[0:hd] / a_s[hd:hd + 1])
                 + gate(2, h) * (a_w[0:hd] / a_w[hd:hd + 1]))
    o_ref[0] = ot[...].reshape(nh * hd, tq).T.astype(BF16)


def _attention(qt, gt, kc, vct, kx, vx, tile, cmp_tab, ovt, pm):
    bsz, _, s = qt.shape
    nh, hd, tq, tk = NSA_HEADS, HEAD_DIM, ATT_Q, ATT_K
    ncmp = kc.shape[1]
    nchunk = s // tk
    return pl.pallas_call(
        _attn_kernel,
        grid=(bsz, s // tq),
        in_specs=[
            pl.BlockSpec((1, NSA_Q, tq), lambda b, i: (b, 0, i)),
            pl.BlockSpec((1, 3 * nh, tq), lambda b, i: (b, 0, i)),
            pl.BlockSpec((1, ncmp, hd), lambda b, i: (b, 0, 0)),
            pl.BlockSpec((1, hd, ncmp), lambda b, i: (b, 0, 0)),
            pl.BlockSpec((1, kx.shape[1], KX_COLS), lambda b, i: (b, 0, 0)),
            pl.BlockSpec((1, 2 * VX_ROWS, vx.shape[2]), lambda b, i: (b, 0, 0)),
            _const_spec(tile.shape),
            _const_spec(cmp_tab.shape),
            _const_spec(ovt.shape),
            _const_spec(pm.shape),
        ],
        out_specs=pl.BlockSpec((1, tq, NSA_Q), lambda b, i: (b, i, 0)),
        out_shape=jax.ShapeDtypeStruct((bsz, s, NSA_Q), BF16),
        scratch_shapes=[
            pltpu.VMEM((nh, QX_ROWS, tq), BF16),
            pltpu.VMEM((nchunk, 16, tq), BF16),
            pltpu.VMEM((tk, tq), F32),
            pltpu.VMEM((nh, VX_ROWS, tq), F32),
            pltpu.VMEM((nh, 1, tq), F32),
            pltpu.VMEM((nh, VX_ROWS, tq), F32),
            pltpu.VMEM((nh, 1, tq), F32),
            pltpu.VMEM((nh, hd, tq), F32),
        ],
        compiler_params=_cparams(("arbitrary", "arbitrary")),
        name="nsa_attention",
    )(qt, gt, kc, vct, kx, vx, tile, cmp_tab, ovt, pm)


def _odd_out_kernel(x_ref, mod_ref, o_ref_in, u_ref, pw_ref, ps_ref, wo_ref, wp_ref, g_ref, b_ref, o_ref, ubuf):
    tm = x_ref.shape[1]
    t = pl.program_id(1)

    @pl.when(t == 0)
    def _():
        ubuf[0:POOL_HALO, :] = jnp.zeros((POOL_HALO, POOL_CH), F32)

    u = u_ref[0]
    ubuf[POOL_HALO:POOL_HALO + tm, :] = u

    lane_group = lax.broadcasted_iota(jnp.int32, (tm, POOL_CH), 1) // POOL_GROUP_CH
    pos = t * tm + lax.broadcasted_iota(jnp.int32, (tm, POOL_CH), 0)
    total = jnp.zeros((tm, POOL_CH), F32)
    width = jnp.ones((tm, POOL_CH), F32)
    sw = ubuf[...]
    span = 1
    for gi, w in enumerate(POOL_WINDOWS):
        while span < w:
            sw = sw + pltpu.roll(sw, span, axis=0)
            span *= 2
        total = jnp.where(lane_group == gi, sw[POOL_HALO:], total)
        width = jnp.where(lane_group == gi, float(w), width)
    cnt = jnp.minimum((pos + 1).astype(F32), width)
    dlt = (total / cnt - u).astype(BF16)
    ubuf[0:POOL_HALO, :] = ubuf[tm:tm + POOL_HALO, :]

    o_pool = (_dot(dlt, pw_ref[...]) * ps_ref[...]).astype(BF16)
    y = _dot(o_ref_in[0], wo_ref[...]) + _dot(o_pool, wp_ref[...])
    x = x_ref[0]
    mod = mod_ref[0, 0, 0]
    z = DN_ALPHA * x + (1.0 + mod[2:3]) * y
    o_ref[0] = _layer_norm(z, g_ref[...], b_ref[...])


def _odd_out(x, mod, layer, o_nsa, u, pw, ps, wo, wp, g, b):
    bsz, s, d = x.shape
    tm = min(MIX_ROWS, s)
    return pl.pallas_call(
        _odd_out_kernel,
        grid=(bsz, s // tm),
        in_specs=[
            pl.BlockSpec((1, tm, d), lambda i, t: (i, t, 0)),
            pl.BlockSpec((1, 1, 1, 3, d), lambda i, t: (layer, i, 1, 0, 0)),
            pl.BlockSpec((1, tm, NSA_Q), lambda i, t: (i, t, 0)),
            pl.BlockSpec((1, tm, POOL_CH), lambda i, t: (i, t, 0)),
            _const_spec(pw.shape),
            _const_spec((1, POOL_CH)),
            _const_spec(wo.shape),
            _const_spec(wp.shape),
            _const_spec((1, d)),
            _const_spec((1, d)),
        ],
        out_specs=pl.BlockSpec((1, tm, d), lambda i, t: (i, t, 0)),
        out_shape=jax.ShapeDtypeStruct(x.shape, F32),
        scratch_shapes=[pltpu.VMEM((POOL_HALO + tm, POOL_CH), F32)],
        compiler_params=_cparams(("arbitrary", "arbitrary")),
        name="odd_out",
    )(x, mod, o_nsa, u, pw, ps.reshape(1, POOL_CH), wo, wp, g.reshape(1, d), b.reshape(1, d))


def _odd_in_weights(w_in):
    hd, nh = HEAD_DIM, NSA_HEADS
    c = NSA_Q
    col = lambda k: w_in[:, c + k * hd:c + (k + 1) * hd]
    kc, vc, ks, vs, kw, vw = (col(k) for k in range(6))
    gates = w_in[:, c + 6 * hd:c + 6 * hd + 3 * nh]
    u = w_in[:, c + 6 * hd + 3 * nh:]
    perm = np.array([3 * h + j for j in range(3) for h in range(nh)])
    zc = lambda n: jnp.zeros((w_in.shape[0], n), w_in.dtype)
    w_std = jnp.concatenate([ks, zc(KX_WIN - hd), kw, zc(KX_COLS - KX_WIN - hd), kc, vc, u], axis=1)
    w_t = jnp.concatenate([w_in[:, 0:c], vs, zc(VX_ROWS - hd), vw, zc(VX_ROWS - hd), gates[:, perm]], axis=1)
    return w_std.astype(BF16), w_t.T.astype(BF16)


def _compress_weights(pe_k, pe_v, w1_k, w1_v, w2_k, w2_v):
    hd, hid, half = HEAD_DIM, CMP_HIDDEN, CMP_STRIDE

    def halves(w1):
        w = w1.reshape(CMP_BLOCK, hd, hid)
        return w[:half], w[half:]

    kt, kb = halves(w1_k)
    vt, vb = halves(w1_v)
    z = jnp.zeros((half, hd, hid), F32)

    def assemble(k_part, v_part):
        k_rows = jnp.concatenate([k_part, z], axis=2)
        v_rows = jnp.concatenate([z, v_part], axis=2)
        return jnp.concatenate([k_rows, v_rows], axis=1).reshape(half * 2 * hd, 2 * hid).astype(BF16)

    wt = assemble(kt, vt)
    wb = assemble(kb, vb)
    pe = jnp.concatenate([pe_k.reshape(2, half, hd), pe_v.reshape(2, half, hd)], axis=2)
    pe = pe.reshape(2, half * 2 * hd)
    return pe, wt, wb, w2_k.astype(BF16), w2_v.T.astype(BF16)


def _overlap_matrix_t(n_cols, n_slc):
    n_cmp = n_slc * SLC_BLOCK // CMP_STRIDE - 1
    cmp_start = np.arange(n_cmp) * CMP_STRIDE
    slc_start = np.arange(n_slc) * SLC_BLOCK
    ov = np.clip(np.minimum(cmp_start[:, None] + CMP_BLOCK, slc_start[None, :] + SLC_BLOCK)
                 - np.maximum(cmp_start[:, None], slc_start[None, :]), 0, None) / CMP_BLOCK
    out = np.zeros((n_slc, n_cols), np.float32)
    out[:, :n_cmp] = ov.T
    return jnp.asarray(out, BF16)


def _chunk_row_placement(n_chunk, n_slc):
    per = ATT_K // SLC_BLOCK
    pm = np.zeros((n_chunk * 16, n_slc), np.float32)
    for c in range(n_chunk):
        for b in range(per):
            pm[16 * c + b, per * c + b] = 1.0
    return jnp.asarray(pm, BF16)


def _pool_weight(pool_w):
    ng, gc = len(POOL_WINDOWS), POOL_GROUP_CH
    w = jnp.zeros((ng * gc, ng * gc), F32)
    for gi in range(ng):
        w = w.at[gi * gc:(gi + 1) * gc, gi * gc:(gi + 1) * gc].set(pool_w[gi])
    return w.astype(BF16)


def _odd_mixer(x, mod, layer, w_in, pe_k, pe_v, w1_k, w2_k, w1_v, w2_v, pool_w, pool_scale, w_out,
               bias_tables, g, b):
    bsz, s, _ = x.shape
    assert s % ATT_Q == 0 and s // CMP_STRIDE == LANES and ATT_Q == ATT_K and MIX_ROWS % ATT_K == 0
    w_std, w_t = _odd_in_weights(w_in)
    kx, kvc, u, qt, vx, gt = _odd_proj(x, mod, layer, w_std, w_t)
    kc, vct = _compress(kvc, *_compress_weights(pe_k, pe_v, w1_k, w1_v, w2_k, w2_v))
    tile, cmp_tab = bias_tables
    ovt = _overlap_matrix_t(s // CMP_STRIDE, s // SLC_BLOCK)
    pm = _chunk_row_placement(s // ATT_K, s // SLC_BLOCK)
    o_nsa = _attention(qt, gt, kc, vct, kx, vx, tile, cmp_tab, ovt, pm)
    return _odd_out(x, mod, layer, o_nsa, u, _pool_weight(pool_w), pool_scale,
                    w_out[:NSA_Q].astype(BF16), w_out[NSA_Q:].astype(BF16), g, b)


def kernel(x, c, ada_w, ada_b, ln_g, ln_b, ffn_w_gate, ffn_w_up, ffn_w_down, ev_w_in, ev_conv_a_w,
           ev_conv_a_b, ev_norm_a_g, ev_norm_a_b, ev_conv_b_w, ev_w_out, od_w_in, od_cmp_pe_k,
           od_cmp_pe_v, od_cmp_w1_k, od_cmp_w2_k, od_cmp_w1_v, od_cmp_w2_v, od_pool_w, od_pool_scale,
           od_w_out, rel_bias):
    bsz, s, d = x.shape
    depth = ada_w.shape[0]
    mod = _modulation(c, ada_w, ada_b).reshape(depth, bsz, 3, 3, d)
    bias_tables = _bias_tables(rel_bias)
    weights = tuple(w[0, 0].astype(BF16) for w in (ffn_w_gate, ffn_w_up, ffn_w_down))
    f32_weights = (ffn_w_gate, ffn_w_up, ffn_w_down)
    for layer in range(depth):
        j = layer // 2
        x, weights = _ffn(x, mod, layer, 0, weights, ln_g[layer, 0], ln_b[layer, 0],
                          nxt=(layer, 1) + f32_weights)
        if layer % 2 == 0:
            x = _even_mixer(x, mod, layer, ev_w_in[j].astype(BF16), _sublane_replicated(ev_conv_a_w[j]),
                            ev_conv_a_b[j], ev_norm_a_g[j], ev_norm_a_b[j], ev_conv_b_w[j],
                            ev_w_out[j].astype(BF16), ln_g[layer, 1], ln_b[layer, 1])
        else:
            x = _odd_mixer(x, mod, layer, od_w_in[j], od_cmp_pe_k[j], od_cmp_pe_v[j], od_cmp_w1_k[j],
                           od_cmp_w2_k[j], od_cmp_w1_v[j], od_cmp_w2_v[j], od_pool_w[j], od_pool_scale[j],
                           od_w_out[j], bias_tables, ln_g[layer, 1], ln_b[layer, 1])
        x, weights = _ffn(x, mod, layer, 2, weights, ln_g[layer, 2], ln_b[layer, 2],
                          nxt=(layer + 1, 0) + f32_weights if layer + 1 < depth else None)
    return x
```

```python
import functools
import math

import numpy as np
import jax
import jax.numpy as jnp
from jax import lax
from jax.experimental import pallas as pl
from jax.experimental.pallas import tpu as pltpu

F32 = jnp.float32
BF16 = jnp.bfloat16

D_MODEL = 1024
DEPTH = 4
D_FF = 2816
FFN_RES_WEIGHT = 0.5
CONV_A_CH = 512
CONV_A_WIDTH = 31
CONV_B_CH = 512
CONV_B_WIDTH = 3
NSA_HEADS = 16
HEAD_DIM = 64
CMP_BLOCK = 32
CMP_STRIDE = 16
CMP_HIDDEN = 128
SLC_BLOCK = 64
SLC_TOPK = 16
WINDOW = 512
POOL_WINDOWS = (2, 4, 8, 16)
POOL_GROUP_CH = 64
POOL_CH = len(POOL_WINDOWS) * POOL_GROUP_CH
REL_BUCKETS = 32
REL_MAX_DIST = 128
DN_ALPHA = (2 * DEPTH) ** 0.25
LN_EPS = 1e-5
NSA_Q = NSA_HEADS * HEAD_DIM
NEG_INF = -1e30
FORCE = 1e30

LANES = 128
SUBLANES = 8
VMEM_LIMIT_BYTES = 56 * 1024 * 1024

FFN_ROWS = 1024
FFN_SUB_ROWS = 256
MIX_ROWS = 512
EVEN_OUT_ROWS = 256
ATT_Q = 256
ATT_K = 256
CONV_HALO = 32
CONV_B_HALO = 8
POOL_HALO = 16
HEAD_LOOKAHEAD = 6


def _cparams(sem):
    return pltpu.CompilerParams(dimension_semantics=sem, vmem_limit_bytes=VMEM_LIMIT_BYTES)


def _const_spec(shape):
    n = len(shape)
    return pl.BlockSpec(shape, lambda *_: (0,) * n, pipeline_mode=pl.Buffered(1))


def _layer_norm(z, g, b):
    mu = jnp.mean(z, axis=-1, keepdims=True)
    zc = z - mu
    var = jnp.mean(zc * zc, axis=-1, keepdims=True)
    return zc * lax.rsqrt(var + LN_EPS) * g + b


def _silu(v):
    return v * jax.nn.sigmoid(v)


def _dot(a, b):
    return jnp.dot(a, b, preferred_element_type=F32)


def _dot_nt(a, b):
    return lax.dot_general(a, b, (((1,), (1,)), ((), ())), preferred_element_type=F32)


def _mod_kernel(c_ref, w_ref, b_ref, o_ref):
    cond = _silu(c_ref[...])
    o_ref[0] = _dot(cond.astype(BF16), w_ref[0].astype(BF16)) + b_ref[0]


def _modulation(c, ada_w, ada_b):
    depth, d, n = ada_w.shape
    bsz = c.shape[0]
    tn = 1152
    return pl.pallas_call(
        _mod_kernel,
        grid=(depth, n // tn),
        in_specs=[
            pl.BlockSpec((bsz, d), lambda l, j: (0, 0)),
            pl.BlockSpec((1, d, tn), lambda l, j: (l, 0, j)),
            pl.BlockSpec((1, 1, tn), lambda l, j: (l, 0, j)),
        ],
        out_specs=pl.BlockSpec((1, bsz, tn), lambda l, j: (l, 0, j)),
        out_shape=jax.ShapeDtypeStruct((depth, bsz, n), F32),
        compiler_params=_cparams(("arbitrary", "arbitrary")),
        name="modulation",
    )(c, ada_w, ada_b.reshape(depth, 1, n))


def _ffn_kernel(*refs, cast_next):
    if cast_next:
        (x_ref, mod_ref, wg_ref, wu_ref, wd_ref, g_ref, b_ref, ng_ref, nu_ref, nd_ref,
         o_ref, og_ref, ou_ref, od_ref) = refs
        og_ref[...] = ng_ref[0, 0].astype(BF16)
        ou_ref[...] = nu_ref[0, 0].astype(BF16)
        od_ref[...] = nd_ref[0, 0].astype(BF16)
    else:
        x_ref, mod_ref, wg_ref, wu_ref, wd_ref, g_ref, b_ref, o_ref = refs
    mod = mod_ref[0, 0, 0]
    tm = x_ref.shape[1]
    for r0 in range(0, tm, FFN_SUB_ROWS):
        x = x_ref[0, r0:r0 + FFN_SUB_ROWS, :]
        h = (x * (1.0 + mod[1:2]) + mod[0:1]).astype(BF16)
        a = (_silu(_dot(h, wg_ref[...])) * _dot(h, wu_ref[...])).astype(BF16)
        y = _dot(a, wd_ref[...])
        z = DN_ALPHA * x + (FFN_RES_WEIGHT * (1.0 + mod[2:3])) * y
        o_ref[0, r0:r0 + FFN_SUB_ROWS, :] = _layer_norm(z, g_ref[...], b_ref[...])


def _ffn(x, mod, layer, sub, weights, g, b, nxt=None):
    bsz, s, d = x.shape
    wg, wu, wd = weights
    ff = wg.shape[1]
    tm = min(FFN_ROWS, s)
    nt = s // tm
    steps = bsz * nt
    in_specs = [
        pl.BlockSpec((1, tm, d), lambda i, t: (i, t, 0)),
        pl.BlockSpec((1, 1, 1, 3, d), lambda i, t: (layer, i, sub, 0, 0)),
        _const_spec((d, ff)),
        _const_spec((d, ff)),
        _const_spec((ff, d)),
        _const_spec((1, d)),
        _const_spec((1, d)),
    ]
    out_specs = [pl.BlockSpec((1, tm, d), lambda i, t: (i, t, 0))]
    out_shape = [jax.ShapeDtypeStruct(x.shape, F32)]
    args = [x, mod, wg, wu, wd, g.reshape(1, d), b.reshape(1, d)]
    if nxt is not None:
        nl, nh, w_gate, w_up, w_down = nxt
        up_rows = d // steps
        down_steps = steps // 2
        down_rows = ff // down_steps
        assert d % steps == 0 and up_rows % 16 == 0 and ff % down_steps == 0 and down_rows % 16 == 0
        in_specs += [
            pl.BlockSpec((1, 1, up_rows, ff), lambda i, t: (nl, nh, i * nt + t, 0)),
            pl.BlockSpec((1, 1, up_rows, ff), lambda i, t: (nl, nh, i * nt + t, 0)),
            pl.BlockSpec((1, 1, down_rows, d), lambda i, t: (nl, nh, (i * nt + t) // 2, 0)),
        ]
        out_specs += [
            pl.BlockSpec((up_rows, ff), lambda i, t: (i * nt + t, 0)),
            pl.BlockSpec((up_rows, ff), lambda i, t: (i * nt + t, 0)),
            pl.BlockSpec((down_rows, d), lambda i, t: ((i * nt + t) // 2, 0)),
        ]
        out_shape += [jax.ShapeDtypeStruct((d, ff), BF16), jax.ShapeDtypeStruct((d, ff), BF16),
                      jax.ShapeDtypeStruct((ff, d), BF16)]
        args += [w_gate, w_up, w_down]
    out = pl.pallas_call(
        functools.partial(_ffn_kernel, cast_next=nxt is not None),
        grid=(bsz, nt),
        in_specs=in_specs,
        out_specs=out_specs,
        out_shape=out_shape,
        compiler_params=_cparams(("arbitrary", "arbitrary")),
        name="ffn",
    )(*args)
    return out[0], (tuple(out[1:]) if nxt is not None else None)


def _even_kernel(x_ref, mod_ref, win_ref, caw_ref, cab_ref, nag_ref, nab_ref, cbw_ref, wout_ref,
                 g_ref, b_ref, o_ref, abuf, ash, bbuf, cat):
    tm = x_ref.shape[1]
    ca, cb = CONV_A_CH, CONV_B_CH

    @pl.when(pl.program_id(1) == 0)
    def _():
        abuf[0:CONV_HALO, :] = jnp.zeros((CONV_HALO, ca), F32)
        bbuf[0:CONV_B_HALO, :] = jnp.zeros((CONV_B_HALO, cb), F32)

    x = x_ref[0]
    mod = mod_ref[0, 0, 0]
    h = (x * (1.0 + mod[1:2]) + mod[0:1]).astype(BF16)
    p = _dot(h, win_ref[...])
    abuf[CONV_HALO:CONV_HALO + tm, :] = p[:, 0:ca] * jax.nn.sigmoid(p[:, ca:2 * ca])
    gate_b = p[:, 2 * ca:2 * ca + cb]
    bbuf[CONV_B_HALO:CONV_B_HALO + tm, :] = p[:, 2 * ca + cb:2 * ca + 2 * cb] * p[:, 2 * ca + 2 * cb:]

    span = tm + CONV_HALO - SUBLANES
    for r in range(1, SUBLANES):
        ash[r - 1] = abuf[r:r + span, :]

    rows = 64
    for r0 in range(0, tm, rows):
        acc = jnp.zeros((rows // SUBLANES, SUBLANES, ca), F32)
        for k in range(CONV_A_WIDTH):
            off = CONV_HALO + r0 - (CONV_A_WIDTH - 1) + k
            r = off % SUBLANES
            tap = abuf[off:off + rows, :] if r == 0 else ash[r - 1, off - r:off - r + rows, :]
            acc = acc + caw_ref[k][None] * tap.reshape(rows // SUBLANES, SUBLANES, ca)
        acc = acc.reshape(rows, ca)
        u = _layer_norm(acc + cab_ref[...], nag_ref[...], nab_ref[...])
        cat[r0:r0 + rows, 0:ca] = _silu(u).astype(BF16)
        accb = jnp.zeros((rows, cb), F32)
        for k in range(CONV_B_WIDTH):
            off = CONV_B_HALO + r0 - (CONV_B_WIDTH - 1) + k
            accb = accb + cbw_ref[k:k + 1, :] * bbuf[off:off + rows, :]
        cat[r0:r0 + rows, ca:ca + cb] = (gate_b[r0:r0 + rows] * accb).astype(BF16)

        s1 = r0 + rows
        if s1 % EVEN_OUT_ROWS == 0:
            s0 = s1 - EVEN_OUT_ROWS
            y = _dot(cat[s0:s1, :], wout_ref[...])
            z = DN_ALPHA * x[s0:s1] + (1.0 + mod[2:3]) * y
            o_ref[0, s0:s1, :] = _layer_norm(z, g_ref[...], b_ref[...])

    abuf[0:CONV_HALO, :] = abuf[tm:tm + CONV_HALO, :]
    bbuf[0:CONV_B_HALO, :] = bbuf[tm:tm + CONV_B_HALO, :]


def _even_mixer(x, mod, layer, win, caw, cab, nag, nab, cbw, wout, g, b):
    bsz, s, d = x.shape
    tm = min(MIX_ROWS, s)
    ca, cb = CONV_A_CH, CONV_B_CH
    return pl.pallas_call(
        _even_kernel,
        grid=(bsz, s // tm),
        in_specs=[
            pl.BlockSpec((1, tm, d), lambda i, t: (i, t, 0)),
            pl.BlockSpec((1, 1, 1, 3, d), lambda i, t: (layer, i, 1, 0, 0)),
            _const_spec(win.shape),
            _const_spec(caw.shape),
            _const_spec((1, ca)),
            _const_spec((1, ca)),
            _const_spec((1, ca)),
            _const_spec(cbw.shape),
            _const_spec(wout.shape),
            _const_spec((1, d)),
            _const_spec((1, d)),
        ],
        out_specs=pl.BlockSpec((1, tm, d), lambda i, t: (i, t, 0)),
        out_shape=jax.ShapeDtypeStruct(x.shape, F32),
        scratch_shapes=[
            pltpu.VMEM((CONV_HALO + tm, ca), F32),
            pltpu.VMEM((SUBLANES - 1, CONV_HALO + tm - SUBLANES, ca), F32),
            pltpu.VMEM((CONV_B_HALO + tm, cb), F32),
            pltpu.VMEM((tm, ca + cb), BF16),
        ],
        compiler_params=_cparams(("arbitrary", "arbitrary")),
        name="even_mixer",
    )(x, mod, win, caw, cab.reshape(1, ca), nag.reshape(1, ca), nab.reshape(1, ca), cbw, wout,
      g.reshape(1, d), b.reshape(1, d))


def _sublane_replicated(w):
    return jnp.broadcast_to(w[:, None, :], (w.shape[0], SUBLANES, w.shape[1]))


KX_COLS = 256
KX_WIN = 128
MASK_ROWS = 16
QX_ROWS = HEAD_DIM + 2 * MASK_ROWS
VX_ROWS = HEAD_DIM + 16
LOG2E = math.log2(math.e)


def _odd_proj_kernel(x_ref, mod_ref, w_ref, wt_ref, kx_ref, kvc_ref, u_ref, qt_ref, vx_ref, gt_ref):
    tm = x_ref.shape[1]
    x = x_ref[0]
    mod = mod_ref[0, 0, 0]
    h = (x * (1.0 + mod[1:2]) + mod[0:1]).astype(BF16)
    p = _dot(h, w_ref[...])
    c1 = KX_COLS
    c2 = c1 + 2 * HEAD_DIM
    row = lax.broadcasted_iota(jnp.int32, (tm, KX_COLS), 0)
    col = lax.broadcasted_iota(jnp.int32, (tm, KX_COLS), 1)
    first = HEAD_DIM + MASK_ROWS * ((row // ATT_K) % 2)
    member = (col - first == (row % ATT_K) // SLC_BLOCK) & (col >= first)
    kx_ref[0] = jnp.where(member, 1.0, p[:, 0:c1]).astype(BF16)
    kvc_ref[0] = p[:, c1:c2]
    u_ref[0] = p[:, c2:]
    pt = _dot_nt(wt_ref[...], h)
    r1 = NSA_Q
    r2 = r1 + 2 * VX_ROWS
    qt_ref[0] = (pt[0:r1] * (HEAD_DIM ** -0.5 * LOG2E)).astype(BF16)
    vrow = lax.broadcasted_iota(jnp.int32, (2 * VX_ROWS, tm), 0)
    vx_ref[0] = jnp.where(vrow % VX_ROWS == HEAD_DIM, 1.0, pt[r1:r2]).astype(BF16)
    gt_ref[0] = jax.nn.sigmoid(pt[r2:])


def _odd_proj(x, mod, layer, w, wt):
    bsz, s, d = x.shape
    tm = min(MIX_ROWS, s)

    def blk(c):
        return pl.BlockSpec((1, tm, c), lambda i, t: (i, t, 0))

    def blk_t(r):
        return pl.BlockSpec((1, r, tm), lambda i, t: (i, 0, t))

    return pl.pallas_call(
        _odd_proj_kernel,
        grid=(bsz, s // tm),
        in_specs=[
            blk(d),
            pl.BlockSpec((1, 1, 1, 3, d), lambda i, t: (layer, i, 1, 0, 0)),
            _const_spec(w.shape),
            _const_spec(wt.shape),
        ],
        out_specs=[blk(KX_COLS), blk(2 * HEAD_DIM), blk(POOL_CH),
                   blk_t(NSA_Q), blk_t(2 * VX_ROWS), blk_t(3 * NSA_HEADS)],
        out_shape=[
            jax.ShapeDtypeStruct((bsz, s, KX_COLS), BF16),
            jax.ShapeDtypeStruct((bsz, s, 2 * HEAD_DIM), F32),
            jax.ShapeDtypeStruct((bsz, s, POOL_CH), F32),
            jax.ShapeDtypeStruct((bsz, NSA_Q, s), BF16),
            jax.ShapeDtypeStruct((bsz, 2 * VX_ROWS, s), BF16),
            jax.ShapeDtypeStruct((bsz, 3 * NSA_HEADS, s), F32),
        ],
        compiler_params=_cparams(("arbitrary", "arbitrary")),
        name="odd_proj",
    )(x, mod, w, wt)


def _compress_kernel(x_ref, pe_ref, wt_ref, wb_ref, w2k_ref, w2vt_ref, kc_ref, vct_ref):
    x = x_ref[0]
    top = _dot((x + pe_ref[0:1]).astype(BF16), wt_ref[...])
    bot = _dot((x + pe_ref[1:2]).astype(BF16), wb_ref[...])
    hid = top + pltpu.roll(bot, bot.shape[0] - 1, axis=0)
    act = _silu(hid).astype(BF16)
    kc_ref[0] = _dot(act[:, 0:CMP_HIDDEN], w2k_ref[...]).astype(BF16)
    vct_ref[0] = _dot_nt(w2vt_ref[...], act[:, CMP_HIDDEN:]).astype(BF16)


def _compress(kvc, pe, wt, wb, w2k, w2vt):
    bsz, s, _ = kvc.shape
    nrow = s // CMP_STRIDE
    width = CMP_STRIDE * 2 * HEAD_DIM
    x = kvc.reshape(bsz, nrow, width)
    return pl.pallas_call(
        _compress_kernel,
        grid=(bsz,),
        in_specs=[
            pl.BlockSpec((1, nrow, width), lambda i: (i, 0, 0)),
            _const_spec(pe.shape),
            _const_spec(wt.shape),
            _const_spec(wb.shape),
            _const_spec(w2k.shape),
            _const_spec(w2vt.shape),
        ],
        out_specs=[pl.BlockSpec((1, nrow, HEAD_DIM), lambda i: (i, 0, 0)),
                   pl.BlockSpec((1, HEAD_DIM, nrow), lambda i: (i, 0, 0))],
        out_shape=[jax.ShapeDtypeStruct((bsz, nrow, HEAD_DIM), BF16),
                   jax.ShapeDtypeStruct((bsz, HEAD_DIM, nrow), BF16)],
        compiler_params=_cparams(("arbitrary",)),
        name="compress",
    )(x, pe, wt, wb, w2k, w2vt)


CMP_TAB_ROWS = 240


def _t5_bucket_table(max_dist):
    n = np.arange(max_dist, dtype=np.int64)
    exact = REL_BUCKETS // 2
    nf = np.maximum(n, 1).astype(np.float32)
    large = exact + (np.log(nf / np.float32(exact)) / np.float32(math.log(REL_MAX_DIST / exact))
                     * np.float32(REL_BUCKETS - exact)).astype(np.int32)
    return np.where(n < exact, n, np.minimum(large, REL_BUCKETS - 1)).astype(np.int32)


def _bias_tables(rel_bias):
    far = REL_BUCKETS - 1
    table = _t5_bucket_table(1024)

    def buckets(dist):
        return np.where(dist >= 0, table[np.clip(dist, 0, 1023)], REL_BUCKETS).reshape(-1)

    r = np.arange(128)[:, None]
    c = np.arange(ATT_Q)[None, :]
    jj = np.arange(CMP_TAB_ROWS)[:, None]
    bucket = np.concatenate([buckets(c - r), buckets(c - CMP_STRIDE * (jj - 112) - (CMP_BLOCK - 1))])
    onehot = (jnp.asarray(bucket)[None, :] == jnp.arange(REL_BUCKETS + 1)[:, None]).astype(F32)
    rb = (rel_bias.astype(F32) - rel_bias[far:far + 1].astype(F32)).T * LOG2E
    rb = jnp.concatenate([rb, jnp.full((rb.shape[0], 1), NEG_INF, F32)], axis=1)
    flat = jnp.dot(rb, onehot, precision=lax.Precision.HIGHEST)
    nh = rel_bias.shape[1]
    tile = flat[:, :128 * ATT_Q].reshape(nh, 128, ATT_Q)
    cmp_tab = flat[:, 128 * ATT_Q:].reshape(nh, CMP_TAB_ROWS, ATT_Q)
    return tile, cmp_tab


def _attn_kernel(qt_ref, gt_ref, kc_ref, vct_ref, kx_ref, vx_ref, tile_ref, cmpb_ref, ovt_ref, pm_ref,
                 o_ref, qx, selb, old_mask, acc_s, m_sel, acc_w, m_win, ot):
    nh, hd = NSA_HEADS, HEAD_DIM
    tq, tk = ATT_Q, ATT_K
    i = pl.program_id(1)
    t0 = i * tq
    ncmp = kc_ref.shape[1]

    @pl.when(i == 0)
    def _():
        r = lax.broadcasted_iota(jnp.int32, (tk, tq), 0)
        c = lax.broadcasted_iota(jnp.int32, (tk, tq), 1)
        old_mask[...] = jnp.where(r > c, 0.0, NEG_INF)
        for h in range(nh):
            qx[h, hd:QX_ROWS, :] = jnp.zeros((QX_ROWS - hd, tq), BF16)

    for h in range(nh):
        qx[h, 0:hd, :] = qt_ref[0, h * hd:(h + 1) * hd, :]

    def gate(branch, h):
        r = branch * nh + h
        return gt_ref[0, r:r + 1, :]

    def run_pipelined(steps):
        work, before = [], {}
        for st in steps:
            if callable(st):
                before.setdefault(len(work), []).append(st)
            else:
                work.append(st)
        pending = {}

        def issue(j):
            for f in before.get(j, ()):
                f()
            pending[j] = work[j][0]()

        for j in range(min(HEAD_LOOKAHEAD, len(work))):
            issue(j)
        for j in range(len(work)):
            if j + HEAD_LOOKAHEAD < len(work):
                issue(j + HEAD_LOOKAHEAD)
            work[j][1](pending.pop(j))

    kc = kc_ref[0]
    vct = vct_ref[0]
    start = pl.multiple_of(112 - (tq // CMP_STRIDE) * i, 16)
    tok = t0 + lax.broadcasted_iota(jnp.int32, (1, tq), 1)
    any_valid = jnp.where(tok >= CMP_BLOCK - 1, 1.0, 0.0)
    psum = []

    def cmp_logits(h):
        return _dot(kc, qx[h, 0:hd, :]) + cmpb_ref[h, pl.ds(start, ncmp), :]

    def cmp_softmax(h, s):
        e = jnp.exp2(s - jnp.max(s, axis=0, keepdims=True))
        prob = e * (any_valid / jnp.sum(e, axis=0, keepdims=True))
        psum[-1] = psum[-1] + prob
        ot[h] = gate(0, h) * _dot(vct, prob.astype(BF16))

    def cmp_steps():
        psum.append(jnp.zeros((ncmp, tq), F32))
        return [(functools.partial(cmp_logits, h), functools.partial(cmp_softmax, h)) for h in range(nh)]

    def select_blocks():
        p_hi = psum[-1].astype(BF16)
        p_lo = (psum[-1] - p_hi.astype(F32)).astype(BF16)
        ovt = ovt_ref[...]
        imp = _dot(ovt, p_hi) + _dot(ovt, p_lo)
        nslc = ovt.shape[0]
        blk = lax.broadcasted_iota(jnp.int32, (nslc, tq), 0)
        cur = (t0 + lax.broadcasted_iota(jnp.int32, (nslc, tq), 1)) // SLC_BLOCK
        forced = (blk == 0) | (blk == cur) | (blk == cur - 1)
        val = jnp.where(forced, FORCE, jnp.where(blk <= cur, imp, -FORCE))
        rank = jnp.zeros((nslc, tq), F32)
        for k in range(nslc):
            vk = val[k:k + 1, :]
            rank = rank + jnp.where(blk > k, jnp.where(vk >= val, 1.0, 0.0), jnp.where(vk > val, 1.0, 0.0))
        selbias = jnp.where(rank < float(min(SLC_TOPK, nslc)), 0.0, NEG_INF).astype(BF16)
        selb[...] = _dot(pm_ref[...], selbias).reshape(selb.shape).astype(BF16)

    def set_sel_rows(c):
        tile = selb[c]
        off = hd + MASK_ROWS * (c % 2)
        if not isinstance(off, int):
            off = pl.multiple_of(off, MASK_ROWS)
        for h in range(nh):
            qx[h, pl.ds(off, MASK_ROWS), :] = tile

    def chunk_steps(row0, selected, mode, first):
        row0 = pl.multiple_of(row0, tk)
        half = tk // 2
        assert first == (mode == "cur")
        acc, m_run = (acc_s, m_sel) if selected else (acc_w, m_win)

        def logits(h):
            kcols = slice(0, QX_ROWS) if selected else slice(KX_WIN, KX_WIN + hd)
            q = qx[h] if selected else qx[h, 0:hd, :]
            if mode == "cur":
                t = tile_ref[h]
                s_top = _dot(kx_ref[0, pl.ds(row0, half), kcols], q) + t
                s_br = _dot(kx_ref[0, pl.ds(row0 + half, half), kcols], q[:, half:]) + t[:, 0:half]
                return s_top, s_br
            s = _dot(kx_ref[0, pl.ds(row0, tk), kcols], q)
            if mode == "prev":
                t = tile_ref[h]
                bot = jnp.concatenate([s[128:256, 0:128] + t[:, 128:256], s[128:256, 128:256]], axis=1)
                s = jnp.concatenate([s[0:128], bot], axis=0)
            elif mode == "old":
                s = s + old_mask[...]
            return s

        def update(h, s):
            if selected:
                vx = vx_ref[0, 0:VX_ROWS, pl.ds(row0, tk)]
            else:
                vx = vx_ref[0, VX_ROWS:2 * VX_ROWS, pl.ds(row0, tk)]
            if mode == "cur":
                s_top, s_br = s
                m_left = jnp.max(s_top[:, 0:half], axis=0, keepdims=True)
                m_right = jnp.maximum(jnp.max(s_top[:, half:], axis=0, keepdims=True),
                                      jnp.max(s_br, axis=0, keepdims=True))
                m_new = jnp.concatenate([m_left, m_right], axis=1)
                p_bot = jnp.concatenate([jnp.zeros((half, half), F32), jnp.exp2(s_br - m_right)], axis=1)
                p = jnp.concatenate([jnp.exp2(s_top - m_new), p_bot], axis=0)
                acc[h] = _dot(vx, p.astype(BF16))
            else:
                m_old = m_run[h]
                m_new = jnp.maximum(m_old, jnp.max(s, axis=0, keepdims=True))
                pv = _dot(vx, jnp.exp2(s - m_new).astype(BF16))
                acc[h] = jnp.exp2(m_old - m_new) * acc[h] + pv
            m_run[h] = m_new

        return [(functools.partial(logits, h), functools.partial(update, h)) for h in range(nh)]

    def main_steps():
        return (cmp_steps()
                + chunk_steps(t0, False, "cur", True)
                + [select_blocks, functools.partial(set_sel_rows, i)]
                + chunk_steps(t0, True, "cur", True))

    def prev_steps():
        return ([functools.partial(set_sel_rows, i - 1)]
                + chunk_steps(t0 - tk, True, "prev", False)
                + chunk_steps(t0 - tk, False, "prev", False))

    @pl.when(i == 0)
    def _():
        run_pipelined(main_steps())

    @pl.when(i == 1)
    def _():
        run_pipelined(main_steps() + prev_steps())

    @pl.when(i >= 2)
    def _():
        run_pipelined(main_steps() + prev_steps() + chunk_steps(t0 - 2 * tk, False, "old", False))

    def far_steps(c):
        return chunk_steps(c * tk, True, "far", False)

    n_far = jnp.maximum(i - 1, 0)

    def far_pair(j, carry):
        set_sel_rows(2 * j)
        set_sel_rows(2 * j + 1)
        run_pipelined(far_steps(2 * j) + far_steps(2 * j + 1))
        return carry

    lax.fori_loop(0, n_far // 2, far_pair, 0)

    @pl.when(n_far % 2 == 1)
    def _():
        set_sel_rows(n_far - 1)
        run_pipelined(far_steps(n_far - 1))

    for h in range(nh):
        a_s = acc_s[h]
        a_w = acc_w[h]
        ot[h] = (ot[h] + gate(1, h) * (a_s[0:hd] / a_s[hd:hd + 1])
                 + gate(2, h) * (a_w[0:hd] / a_w[hd:hd + 1]))
    o_ref[0] = ot[...].reshape(nh * hd, tq).T.astype(BF16)


def _attention(qt, gt, kc, vct, kx, vx, tile, cmp_tab, ovt, pm):
    bsz, _, s = qt.shape
    nh, hd, tq, tk = NSA_HEADS, HEAD_DIM, ATT_Q, ATT_K
    ncmp = kc.shape[1]
    nchunk = s // tk
    return pl.pallas_call(
        _attn_kernel,
        grid=(bsz, s // tq),
        in_specs=[
            pl.BlockSpec((1, NSA_Q, tq), lambda b, i: (b, 0, i)),
            pl.BlockSpec((1, 3 * nh, tq), lambda b, i: (b, 0, i)),
            pl.BlockSpec((1, ncmp, hd), lambda b, i: (b, 0, 0)),
            pl.BlockSpec((1, hd, ncmp), lambda b, i: (b, 0, 0)),
            pl.BlockSpec((1, kx.shape[1], KX_COLS), lambda b, i: (b, 0, 0)),
            pl.BlockSpec((1, 2 * VX_ROWS, vx.shape[2]), lambda b, i: (b, 0, 0)),
            _const_spec(tile.shape),
            _const_spec(cmp_tab.shape),
            _const_spec(ovt.shape),
            _const_spec(pm.shape),
        ],
        out_specs=pl.BlockSpec((1, tq, NSA_Q), lambda b, i: (b, i, 0)),
        out_shape=jax.ShapeDtypeStruct((bsz, s, NSA_Q), BF16),
        scratch_shapes=[
            pltpu.VMEM((nh, QX_ROWS, tq), BF16),
            pltpu.VMEM((nchunk, 16, tq), BF16),
            pltpu.VMEM((tk, tq), F32),
            pltpu.VMEM((nh, VX_ROWS, tq), F32),
            pltpu.VMEM((nh, 1, tq), F32),
            pltpu.VMEM((nh, VX_ROWS, tq), F32),
            pltpu.VMEM((nh, 1, tq), F32),
            pltpu.VMEM((nh, hd, tq), F32),
        ],
        compiler_params=_cparams(("arbitrary", "arbitrary")),
        name="nsa_attention",
    )(qt, gt, kc, vct, kx, vx, tile, cmp_tab, ovt, pm)


def _odd_out_kernel(x_ref, mod_ref, o_ref_in, u_ref, pw_ref, ps_ref, wo_ref, wp_ref, g_ref, b_ref, o_ref, ubuf):
    tm = x_ref.shape[1]
    t = pl.program_id(1)

    @pl.when(t == 0)
    def _():
        ubuf[0:POOL_HALO, :] = jnp.zeros((POOL_HALO, POOL_CH), F32)

    u = u_ref[0]
    ubuf[POOL_HALO:POOL_HALO + tm, :] = u

    lane_group = lax.broadcasted_iota(jnp.int32, (tm, POOL_CH), 1) // POOL_GROUP_CH
    pos = t * tm + lax.broadcasted_iota(jnp.int32, (tm, POOL_CH), 0)
    total = jnp.zeros((tm, POOL_CH), F32)
    width = jnp.ones((tm, POOL_CH), F32)
    sw = ubuf[...]
    span = 1
    for gi, w in enumerate(POOL_WINDOWS):
        while span < w:
            sw = sw + pltpu.roll(sw, span, axis=0)
            span *= 2
        total = jnp.where(lane_group == gi, sw[POOL_HALO:], total)
        width = jnp.where(lane_group == gi, float(w), width)
    cnt = jnp.minimum((pos + 1).astype(F32), width)
    dlt = (total / cnt - u).astype(BF16)
    ubuf[0:POOL_HALO, :] = ubuf[tm:tm + POOL_HALO, :]

    o_pool = (_dot(dlt, pw_ref[...]) * ps_ref[...]).astype(BF16)
    y = _dot(o_ref_in[0], wo_ref[...]) + _dot(o_pool, wp_ref[...])
    x = x_ref[0]
    mod = mod_ref[0, 0, 0]
    z = DN_ALPHA * x + (1.0 + mod[2:3]) * y
    o_ref[0] = _layer_norm(z, g_ref[...], b_ref[...])


def _odd_out(x, mod, layer, o_nsa, u, pw, ps, wo, wp, g, b):
    bsz, s, d = x.shape
    tm = min(MIX_ROWS, s)
    return pl.pallas_call(
        _odd_out_kernel,
        grid=(bsz, s // tm),
        in_specs=[
            pl.BlockSpec((1, tm, d), lambda i, t: (i, t, 0)),
            pl.BlockSpec((1, 1, 1, 3, d), lambda i, t: (layer, i, 1, 0, 0)),
            pl.BlockSpec((1, tm, NSA_Q), lambda i, t: (i, t, 0)),
            pl.BlockSpec((1, tm, POOL_CH), lambda i, t: (i, t, 0)),
            _const_spec(pw.shape),
            _const_spec((1, POOL_CH)),
            _const_spec(wo.shape),
            _const_spec(wp.shape),
            _const_spec((1, d)),
            _const_spec((1, d)),
        ],
        out_specs=pl.BlockSpec((1, tm, d), lambda i, t: (i, t, 0)),
        out_shape=jax.ShapeDtypeStruct(x.shape, F32),
        scratch_shapes=[pltpu.VMEM((POOL_HALO + tm, POOL_CH), F32)],
        compiler_params=_cparams(("arbitrary", "arbitrary")),
        name="odd_out",
    )(x, mod, o_nsa, u, pw, ps.reshape(1, POOL_CH), wo, wp, g.reshape(1, d), b.reshape(1, d))


def _odd_in_weights(w_in):
    hd, nh = HEAD_DIM, NSA_HEADS
    c = NSA_Q
    col = lambda k: w_in[:, c + k * hd:c + (k + 1) * hd]
    kc, vc, ks, vs, kw, vw = (col(k) for k in range(6))
    gates = w_in[:, c + 6 * hd:c + 6 * hd + 3 * nh]
    u = w_in[:, c + 6 * hd + 3 * nh:]
    perm = np.array([3 * h + j for j in range(3) for h in range(nh)])
    zc = lambda n: jnp.zeros((w_in.shape[0], n), w_in.dtype)
    w_std = jnp.concatenate([ks, zc(KX_WIN - hd), kw, zc(KX_COLS - KX_WIN - hd), kc, vc, u], axis=1)
    w_t = jnp.concatenate([w_in[:, 0:c], vs, zc(VX_ROWS - hd), vw, zc(VX_ROWS - hd), gates[:, perm]], axis=1)
    return w_std.astype(BF16), w_t.T.astype(BF16)


def _compress_weights(pe_k, pe_v, w1_k, w1_v, w2_k, w2_v):
    hd, hid, half = HEAD_DIM, CMP_HIDDEN, CMP_STRIDE

    def halves(w1):
        w = w1.reshape(CMP_BLOCK, hd, hid)
        return w[:half], w[half:]

    kt, kb = halves(w1_k)
    vt, vb = halves(w1_v)
    z = jnp.zeros((half, hd, hid), F32)

    def assemble(k_part, v_part):
        k_rows = jnp.concatenate([k_part, z], axis=2)
        v_rows = jnp.concatenate([z, v_part], axis=2)
        return jnp.concatenate([k_rows, v_rows], axis=1).reshape(half * 2 * hd, 2 * hid).astype(BF16)

    wt = assemble(kt, vt)
    wb = assemble(kb, vb)
    pe = jnp.concatenate([pe_k.reshape(2, half, hd), pe_v.reshape(2, half, hd)], axis=2)
    pe = pe.reshape(2, half * 2 * hd)
    return pe, wt, wb, w2_k.astype(BF16), w2_v.T.astype(BF16)


def _overlap_matrix_t(n_cols, n_slc):
    n_cmp = n_slc * SLC_BLOCK // CMP_STRIDE - 1
    cmp_start = np.arange(n_cmp) * CMP_STRIDE
    slc_start = np.arange(n_slc) * SLC_BLOCK
    ov = np.clip(np.minimum(cmp_start[:, None] + CMP_BLOCK, slc_start[None, :] + SLC_BLOCK)
                 - np.maximum(cmp_start[:, None], slc_start[None, :]), 0, None) / CMP_BLOCK
    out = np.zeros((n_slc, n_cols), np.float32)
    out[:, :n_cmp] = ov.T
    return jnp.asarray(out, BF16)


def _chunk_row_placement(n_chunk, n_slc):
    per = ATT_K // SLC_BLOCK
    pm = np.zeros((n_chunk * 16, n_slc), np.float32)
    for c in range(n_chunk):
        for b in range(per):
            pm[16 * c + b, per * c + b] = 1.0
    return jnp.asarray(pm, BF16)


def _pool_weight(pool_w):
    ng, gc = len(POOL_WINDOWS), POOL_GROUP_CH
    w = jnp.zeros((ng * gc, ng * gc), F32)
    for gi in range(ng):
        w = w.at[gi * gc:(gi + 1) * gc, gi * gc:(gi + 1) * gc].set(pool_w[gi])
    return w.astype(BF16)


def _odd_mixer(x, mod, layer, w_in, pe_k, pe_v, w1_k, w2_k, w1_v, w2_v, pool_w, pool_scale, w_out,
               bias_tables, g, b):
    bsz, s, _ = x.shape
    assert s % ATT_Q == 0 and s // CMP_STRIDE == LANES and ATT_Q == ATT_K and MIX_ROWS % ATT_K == 0
    w_std, w_t = _odd_in_weights(w_in)
    kx, kvc, u, qt, vx, gt = _odd_proj(x, mod, layer, w_std, w_t)
    kc, vct = _compress(kvc, *_compress_weights(pe_k, pe_v, w1_k, w1_v, w2_k, w2_v))
    tile, cmp_tab = bias_tables
    ovt = _overlap_matrix_t(s // CMP_STRIDE, s // SLC_BLOCK)
    pm = _chunk_row_placement(s // ATT_K, s // SLC_BLOCK)
    o_nsa = _attention(qt, gt, kc, vct, kx, vx, tile, cmp_tab, ovt, pm)
    return _odd_out(x, mod, layer, o_nsa, u, _pool_weight(pool_w), pool_scale,
                    w_out[:NSA_Q].astype(BF16), w_out[NSA_Q:].astype(BF16), g, b)


def kernel(x, c, ada_w, ada_b, ln_g, ln_b, ffn_w_gate, ffn_w_up, ffn_w_down, ev_w_in, ev_conv_a_w,
           ev_conv_a_b, ev_norm_a_g, ev_norm_a_b, ev_conv_b_w, ev_w_out, od_w_in, od_cmp_pe_k,
           od_cmp_pe_v, od_cmp_w1_k, od_cmp_w2_k, od_cmp_w1_v, od_cmp_w2_v, od_pool_w, od_pool_scale,
           od_w_out, rel_bias):
    bsz, s, d = x.shape
    depth = ada_w.shape[0]
    mod = _modulation(c, ada_w, ada_b).reshape(depth, bsz, 3, 3, d)
    bias_tables = _bias_tables(rel_bias)
    weights = tuple(w[0, 0].astype(BF16) for w in (ffn_w_gate, ffn_w_up, ffn_w_down))
    f32_weights = (ffn_w_gate, ffn_w_up, ffn_w_down)
    for layer in range(depth):
        j = layer // 2
        x, weights = _ffn(x, mod, layer, 0, weights, ln_g[layer, 0], ln_b[layer, 0],
                          nxt=(layer, 1) + f32_weights)
        if layer % 2 == 0:
            x = _even_mixer(x, mod, layer, ev_w_in[j].astype(BF16), _sublane_replicated(ev_conv_a_w[j]),
                            ev_conv_a_b[j], ev_norm_a_g[j], ev_norm_a_b[j], ev_conv_b_w[j],
                            ev_w_out[j].astype(BF16), ln_g[layer, 1], ln_b[layer, 1])
        else:
            x = _odd_mixer(x, mod, layer, od_w_in[j], od_cmp_pe_k[j], od_cmp_pe_v[j], od_cmp_w1_k[j],
                           od_cmp_w2_k[j], od_cmp_w1_v[j], od_cmp_w2_v[j], od_pool_w[j], od_pool_scale[j],
                           od_w_out[j], bias_tables, ln_g[layer, 1], ln_b[layer, 1])
        x, weights = _ffn(x, mod, layer, 2, weights, ln_g[layer, 2], ln_b[layer, 2],
                          nxt=(layer + 1, 0) + f32_weights if layer + 1 < depth else None)
    return x
```

```python
import functools
import math

import numpy as np
import jax
import jax.numpy as jnp
from jax import lax
from jax.experimental import pallas as pl
from jax.experimental.pallas import tpu as pltpu

F32 = jnp.float32
BF16 = jnp.bfloat16

D_MODEL = 1024
DEPTH = 4
D_FF = 2816
FFN_RES_WEIGHT = 0.5
CONV_A_CH = 512
CONV_A_WIDTH = 31
CONV_B_CH = 512
CONV_B_WIDTH = 3
NSA_HEADS = 16
HEAD_DIM = 64
CMP_BLOCK = 32
CMP_STRIDE = 16
CMP_HIDDEN = 128
SLC_BLOCK = 64
SLC_TOPK = 16
WINDOW = 512
POOL_WINDOWS = (2, 4, 8, 16)
POOL_GROUP_CH = 64
POOL_CH = len(POOL_WINDOWS) * POOL_GROUP_CH
REL_BUCKETS = 32
REL_MAX_DIST = 128
DN_ALPHA = (2 * DEPTH) ** 0.25
LN_EPS = 1e-5
NSA_Q = NSA_HEADS * HEAD_DIM
NEG_INF = -1e30
FORCE = 1e30

LANES = 128
SUBLANES = 8
VMEM_LIMIT_BYTES = 56 * 1024 * 1024

FFN_ROWS = 1024
FFN_SUB_ROWS = 256
MIX_ROWS = 512
EVEN_OUT_ROWS = 256
ODD_OUT_ROWS = 256
ATT_Q = 256
ATT_K = 256
CONV_HALO = 32
CONV_B_HALO = 8
POOL_HALO = 16
HEAD_LOOKAHEAD = 6


def _cparams(sem):
    return pltpu.CompilerParams(dimension_semantics=sem, vmem_limit_bytes=VMEM_LIMIT_BYTES)


def _const_spec(shape):
    n = len(shape)
    return pl.BlockSpec(shape, lambda *_: (0,) * n, pipeline_mode=pl.Buffered(1))


def _layer_norm(z, g, b):
    mu = jnp.mean(z, axis=-1, keepdims=True)
    zc = z - mu
    var = jnp.mean(zc * zc, axis=-1, keepdims=True)
    return zc * lax.rsqrt(var + LN_EPS) * g + b


def _silu(v):
    return v * jax.nn.sigmoid(v)


def _dot(a, b):
    return jnp.dot(a, b, preferred_element_type=F32)


def _dot_nt(a, b):
    return lax.dot_general(a, b, (((1,), (1,)), ((), ())), preferred_element_type=F32)


def _mod_kernel(c_ref, w_ref, b_ref, o_ref):
    cond = _silu(c_ref[...])
    o_ref[0] = _dot(cond.astype(BF16), w_ref[0].astype(BF16)) + b_ref[0]


def _modulation(c, ada_w, ada_b):
    depth, d, n = ada_w.shape
    bsz = c.shape[0]
    tn = 1152
    return pl.pallas_call(
        _mod_kernel,
        grid=(depth, n // tn),
        in_specs=[
            pl.BlockSpec((bsz, d), lambda l, j: (0, 0)),
            pl.BlockSpec((1, d, tn), lambda l, j: (l, 0, j)),
            pl.BlockSpec((1, 1, tn), lambda l, j: (l, 0, j)),
        ],
        out_specs=pl.BlockSpec((1, bsz, tn), lambda l, j: (l, 0, j)),
        out_shape=jax.ShapeDtypeStruct((depth, bsz, n), F32),
        compiler_params=_cparams(("arbitrary", "arbitrary")),
        name="modulation",
    )(c, ada_w, ada_b.reshape(depth, 1, n))


def _ffn_kernel(*refs, cast_next):
    if cast_next:
        (x_ref, mod_ref, wg_ref, wu_ref, wd_ref, g_ref, b_ref, ng_ref, nu_ref, nd_ref,
         o_ref, og_ref, ou_ref, od_ref) = refs
        og_ref[...] = ng_ref[0, 0].astype(BF16)
        ou_ref[...] = nu_ref[0, 0].astype(BF16)
        od_ref[...] = nd_ref[0, 0].astype(BF16)
    else:
        x_ref, mod_ref, wg_ref, wu_ref, wd_ref, g_ref, b_ref, o_ref = refs
    mod = mod_ref[0, 0, 0]
    tm = x_ref.shape[1]
    for r0 in range(0, tm, FFN_SUB_ROWS):
        x = x_ref[0, r0:r0 + FFN_SUB_ROWS, :]
        h = (x * (1.0 + mod[1:2]) + mod[0:1]).astype(BF16)
        a = (_silu(_dot(h, wg_ref[...])) * _dot(h, wu_ref[...])).astype(BF16)
        y = _dot(a, wd_ref[...])
        z = DN_ALPHA * x + (FFN_RES_WEIGHT * (1.0 + mod[2:3])) * y
        o_ref[0, r0:r0 + FFN_SUB_ROWS, :] = _layer_norm(z, g_ref[...], b_ref[...])


def _ffn(x, mod, layer, sub, weights, g, b, nxt=None):
    bsz, s, d = x.shape
    wg, wu, wd = weights
    ff = wg.shape[1]
    tm = min(FFN_ROWS, s)
    nt = s // tm
    steps = bsz * nt
    in_specs = [
        pl.BlockSpec((1, tm, d), lambda i, t: (i, t, 0)),
        pl.BlockSpec((1, 1, 1, 3, d), lambda i, t: (layer, i, sub, 0, 0)),
        _const_spec((d, ff)),
        _const_spec((d, ff)),
        _const_spec((ff, d)),
        _const_spec((1, d)),
        _const_spec((1, d)),
    ]
    out_specs = [pl.BlockSpec((1, tm, d), lambda i, t: (i, t, 0))]
    out_shape = [jax.ShapeDtypeStruct(x.shape, F32)]
    args = [x, mod, wg, wu, wd, g.reshape(1, d), b.reshape(1, d)]
    if nxt is not None:
        nl, nh, w_gate, w_up, w_down = nxt
        up_rows = d // steps
        down_steps = steps // 2
        down_rows = ff // down_steps
        assert d % steps == 0 and up_rows % 16 == 0 and ff % down_steps == 0 and down_rows % 16 == 0
        in_specs += [
            pl.BlockSpec((1, 1, up_rows, ff), lambda i, t: (nl, nh, i * nt + t, 0)),
            pl.BlockSpec((1, 1, up_rows, ff), lambda i, t: (nl, nh, i * nt + t, 0)),
            pl.BlockSpec((1, 1, down_rows, d), lambda i, t: (nl, nh, (i * nt + t) // 2, 0)),
        ]
        out_specs += [
            pl.BlockSpec((up_rows, ff), lambda i, t: (i * nt + t, 0)),
            pl.BlockSpec((up_rows, ff), lambda i, t: (i * nt + t, 0)),
            pl.BlockSpec((down_rows, d), lambda i, t: ((i * nt + t) // 2, 0)),
        ]
        out_shape += [jax.ShapeDtypeStruct((d, ff), BF16), jax.ShapeDtypeStruct((d, ff), BF16),
                      jax.ShapeDtypeStruct((ff, d), BF16)]
        args += [w_gate, w_up, w_down]
    out = pl.pallas_call(
        functools.partial(_ffn_kernel, cast_next=nxt is not None),
        grid=(bsz, nt),
        in_specs=in_specs,
        out_specs=out_specs,
        out_shape=out_shape,
        compiler_params=_cparams(("arbitrary", "arbitrary")),
        name="ffn",
    )(*args)
    return out[0], (tuple(out[1:]) if nxt is not None else None)


def _even_kernel(x_ref, mod_ref, win_ref, caw_ref, cab_ref, nag_ref, nab_ref, cbw_ref, wout_ref,
                 g_ref, b_ref, o_ref, abuf, ash, bbuf, cat):
    tm = x_ref.shape[1]
    ca, cb = CONV_A_CH, CONV_B_CH

    @pl.when(pl.program_id(1) == 0)
    def _():
        abuf[0:CONV_HALO, :] = jnp.zeros((CONV_HALO, ca), F32)
        bbuf[0:CONV_B_HALO, :] = jnp.zeros((CONV_B_HALO, cb), F32)

    x = x_ref[0]
    mod = mod_ref[0, 0, 0]
    h = (x * (1.0 + mod[1:2]) + mod[0:1]).astype(BF16)
    p = _dot(h, win_ref[...])
    abuf[CONV_HALO:CONV_HALO + tm, :] = p[:, 0:ca] * jax.nn.sigmoid(p[:, ca:2 * ca])
    gate_b = p[:, 2 * ca:2 * ca + cb]
    bbuf[CONV_B_HALO:CONV_B_HALO + tm, :] = p[:, 2 * ca + cb:2 * ca + 2 * cb] * p[:, 2 * ca + 2 * cb:]

    span = tm + CONV_HALO - SUBLANES
    for r in range(1, SUBLANES):
        ash[r - 1] = abuf[r:r + span, :]

    rows = 64
    for r0 in range(0, tm, rows):
        acc = jnp.zeros((rows // SUBLANES, SUBLANES, ca), F32)
        for k in range(CONV_A_WIDTH):
            off = CONV_HALO + r0 - (CONV_A_WIDTH - 1) + k
            r = off % SUBLANES
            tap = abuf[off:off + rows, :] if r == 0 else ash[r - 1, off - r:off - r + rows, :]
            acc = acc + caw_ref[k][None] * tap.reshape(rows // SUBLANES, SUBLANES, ca)
        acc = acc.reshape(rows, ca)
        u = _layer_norm(acc + cab_ref[...], nag_ref[...], nab_ref[...])
        cat[r0:r0 + rows, 0:ca] = _silu(u).astype(BF16)
        accb = jnp.zeros((rows, cb), F32)
        for k in range(CONV_B_WIDTH):
            off = CONV_B_HALO + r0 - (CONV_B_WIDTH - 1) + k
            accb = accb + cbw_ref[k:k + 1, :] * bbuf[off:off + rows, :]
        cat[r0:r0 + rows, ca:ca + cb] = (gate_b[r0:r0 + rows] * accb).astype(BF16)

        s1 = r0 + rows
        if s1 % EVEN_OUT_ROWS == 0:
            s0 = s1 - EVEN_OUT_ROWS
            y = _dot(cat[s0:s1, :], wout_ref[...])
            z = DN_ALPHA * x[s0:s1] + (1.0 + mod[2:3]) * y
            o_ref[0, s0:s1, :] = _layer_norm(z, g_ref[...], b_ref[...])

    abuf[0:CONV_HALO, :] = abuf[tm:tm + CONV_HALO, :]
    bbuf[0:CONV_B_HALO, :] = bbuf[tm:tm + CONV_B_HALO, :]


def _even_mixer(x, mod, layer, win, caw, cab, nag, nab, cbw, wout, g, b):
    bsz, s, d = x.shape
    tm = min(MIX_ROWS, s)
    ca, cb = CONV_A_CH, CONV_B_CH
    return pl.pallas_call(
        _even_kernel,
        grid=(bsz, s // tm),
        in_specs=[
            pl.BlockSpec((1, tm, d), lambda i, t: (i, t, 0)),
            pl.BlockSpec((1, 1, 1, 3, d), lambda i, t: (layer, i, 1, 0, 0)),
            _const_spec(win.shape),
            _const_spec(caw.shape),
            _const_spec((1, ca)),
            _const_spec((1, ca)),
            _const_spec((1, ca)),
            _const_spec(cbw.shape),
            _const_spec(wout.shape),
            _const_spec((1, d)),
            _const_spec((1, d)),
        ],
        out_specs=pl.BlockSpec((1, tm, d), lambda i, t: (i, t, 0)),
        out_shape=jax.ShapeDtypeStruct(x.shape, F32),
        scratch_shapes=[
            pltpu.VMEM((CONV_HALO + tm, ca), F32),
            pltpu.VMEM((SUBLANES - 1, CONV_HALO + tm - SUBLANES, ca), F32),
            pltpu.VMEM((CONV_B_HALO + tm, cb), F32),
            pltpu.VMEM((tm, ca + cb), BF16),
        ],
        compiler_params=_cparams(("arbitrary", "arbitrary")),
        name="even_mixer",
    )(x, mod, win, caw, cab.reshape(1, ca), nag.reshape(1, ca), nab.reshape(1, ca), cbw, wout,
      g.reshape(1, d), b.reshape(1, d))


def _sublane_replicated(w):
    return jnp.broadcast_to(w[:, None, :], (w.shape[0], SUBLANES, w.shape[1]))


KX_COLS = 256
KX_WIN = 128
MASK_ROWS = 16
QX_ROWS = HEAD_DIM + 2 * MASK_ROWS
VX_ROWS = HEAD_DIM + 16
LOG2E = math.log2(math.e)


def _odd_proj_kernel(x_ref, mod_ref, w_ref, wt_ref, kx_ref, kvc_ref, u_ref, qt_ref, vx_ref, gt_ref):
    tm = x_ref.shape[1]
    x = x_ref[0]
    mod = mod_ref[0, 0, 0]
    h = (x * (1.0 + mod[1:2]) + mod[0:1]).astype(BF16)
    p = _dot(h, w_ref[...])
    c1 = KX_COLS
    c2 = c1 + 2 * HEAD_DIM
    row = lax.broadcasted_iota(jnp.int32, (tm, KX_COLS), 0)
    col = lax.broadcasted_iota(jnp.int32, (tm, KX_COLS), 1)
    first = HEAD_DIM + MASK_ROWS * ((row // ATT_K) % 2)
    member = (col - first == (row % ATT_K) // SLC_BLOCK) & (col >= first)
    kx_ref[0] = jnp.where(member, 1.0, p[:, 0:c1]).astype(BF16)
    kvc_ref[0] = p[:, c1:c2]
    u_ref[0] = p[:, c2:]
    pt = _dot_nt(wt_ref[...], h)
    r1 = NSA_Q
    r2 = r1 + 2 * VX_ROWS
    qt_ref[0] = (pt[0:r1] * (HEAD_DIM ** -0.5 * LOG2E)).astype(BF16)
    vrow = lax.broadcasted_iota(jnp.int32, (2 * VX_ROWS, tm), 0)
    vx_ref[0] = jnp.where(vrow % VX_ROWS == HEAD_DIM, 1.0, pt[r1:r2]).astype(BF16)
    gt_ref[0] = jax.nn.sigmoid(pt[r2:])


def _odd_proj(x, mod, layer, w, wt):
    bsz, s, d = x.shape
    tm = min(MIX_ROWS, s)

    def blk(c):
        return pl.BlockSpec((1, tm, c), lambda i, t: (i, t, 0))

    def blk_t(r):
        return pl.BlockSpec((1, r, tm), lambda i, t: (i, 0, t))

    return pl.pallas_call(
        _odd_proj_kernel,
        grid=(bsz, s // tm),
        in_specs=[
            blk(d),
            pl.BlockSpec((1, 1, 1, 3, d), lambda i, t: (layer, i, 1, 0, 0)),
            _const_spec(w.shape),
            _const_spec(wt.shape),
        ],
        out_specs=[blk(KX_COLS), blk(2 * HEAD_DIM), blk(POOL_CH),
                   blk_t(NSA_Q), blk_t(2 * VX_ROWS), blk_t(3 * NSA_HEADS)],
        out_shape=[
            jax.ShapeDtypeStruct((bsz, s, KX_COLS), BF16),
            jax.ShapeDtypeStruct((bsz, s, 2 * HEAD_DIM), F32),
            jax.ShapeDtypeStruct((bsz, s, POOL_CH), F32),
            jax.ShapeDtypeStruct((bsz, NSA_Q, s), BF16),
            jax.ShapeDtypeStruct((bsz, 2 * VX_ROWS, s), BF16),
            jax.ShapeDtypeStruct((bsz, 3 * NSA_HEADS, s), F32),
        ],
        compiler_params=_cparams(("arbitrary", "arbitrary")),
        name="odd_proj",
    )(x, mod, w, wt)


def _compress_kernel(x_ref, pe_ref, wt_ref, wb_ref, w2k_ref, w2vt_ref, kc_ref, vct_ref):
    nrow = kc_ref.shape[1]
    width = x_ref.shape[2]
    top = jnp.zeros((nrow, wt_ref.shape[1]), F32)
    bot = jnp.zeros((nrow, wb_ref.shape[1]), F32)
    for j in range(CMP_STRIDE):
        xj = x_ref[0, pl.ds(j, nrow, stride=CMP_STRIDE), :]
        cols = slice(j * width, (j + 1) * width)
        top = top + _dot((xj + pe_ref[0:1, cols]).astype(BF16), wt_ref[cols, :])
        bot = bot + _dot((xj + pe_ref[1:2, cols]).astype(BF16), wb_ref[cols, :])
    hid = top + pltpu.roll(bot, bot.shape[0] - 1, axis=0)
    act = _silu(hid).astype(BF16)
    kc_ref[0] = _dot(act[:, 0:CMP_HIDDEN], w2k_ref[...]).astype(BF16)
    vct_ref[0] = _dot_nt(w2vt_ref[...], act[:, CMP_HIDDEN:]).astype(BF16)


def _compress(kvc, pe, wt, wb, w2k, w2vt):
    bsz, s, _ = kvc.shape
    nrow = s // CMP_STRIDE
    return pl.pallas_call(
        _compress_kernel,
        grid=(bsz,),
        in_specs=[
            pl.BlockSpec((1, s, 2 * HEAD_DIM), lambda i: (i, 0, 0)),
            _const_spec(pe.shape),
            _const_spec(wt.shape),
            _const_spec(wb.shape),
            _const_spec(w2k.shape),
            _const_spec(w2vt.shape),
        ],
        out_specs=[pl.BlockSpec((1, nrow, HEAD_DIM), lambda i: (i, 0, 0)),
                   pl.BlockSpec((1, HEAD_DIM, nrow), lambda i: (i, 0, 0))],
        out_shape=[jax.ShapeDtypeStruct((bsz, nrow, HEAD_DIM), BF16),
                   jax.ShapeDtypeStruct((bsz, HEAD_DIM, nrow), BF16)],
        compiler_params=_cparams(("arbitrary",)),
        name="compress",
    )(kvc, pe, wt, wb, w2k, w2vt)


CMP_TAB_ROWS = 240


def _t5_bucket_table(max_dist):
    n = np.arange(max_dist, dtype=np.int64)
    exact = REL_BUCKETS // 2
    nf = np.maximum(n, 1).astype(np.float32)
    large = exact + (np.log(nf / np.float32(exact)) / np.float32(math.log(REL_MAX_DIST / exact))
                     * np.float32(REL_BUCKETS - exact)).astype(np.int32)
    return np.where(n < exact, n, np.minimum(large, REL_BUCKETS - 1)).astype(np.int32)


def _bias_tables(rel_bias):
    far = REL_BUCKETS - 1
    table = _t5_bucket_table(1024)

    def buckets(dist):
        return np.where(dist >= 0, table[np.clip(dist, 0, 1023)], REL_BUCKETS).reshape(-1)

    r = np.arange(128)[:, None]
    c = np.arange(ATT_Q)[None, :]
    jj = np.arange(CMP_TAB_ROWS)[:, None]
    bucket = np.concatenate([buckets(c - r), buckets(c - CMP_STRIDE * (jj - 112) - (CMP_BLOCK - 1))])
    onehot = (jnp.asarray(bucket)[None, :] == jnp.arange(REL_BUCKETS + 1)[:, None]).astype(F32)
    rb = (rel_bias.astype(F32) - rel_bias[far:far + 1].astype(F32)).T * LOG2E
    rb = jnp.concatenate([rb, jnp.full((rb.shape[0], 1), NEG_INF, F32)], axis=1)
    flat = jnp.dot(rb, onehot, precision=lax.Precision.HIGHEST)
    nh = rel_bias.shape[1]
    tile = flat[:, :128 * ATT_Q].reshape(nh, 128, ATT_Q)
    cmp_tab = flat[:, 128 * ATT_Q:].reshape(nh, CMP_TAB_ROWS, ATT_Q)
    return tile, cmp_tab


def _attn_kernel(qt_ref, gt_ref, kc_ref, vct_ref, kx_ref, vx_ref, tile_ref, cmpb_ref, ovt_ref, pm_ref,
                 o_ref, qx, selb, old_mask, acc_s, m_sel, acc_w, m_win, ot):
    nh, hd = NSA_HEADS, HEAD_DIM
    tq, tk = ATT_Q, ATT_K
    i = pl.program_id(1)
    t0 = i * tq
    ncmp = kc_ref.shape[1]

    @pl.when(i == 0)
    def _():
        r = lax.broadcasted_iota(jnp.int32, (tk, tq), 0)
        c = lax.broadcasted_iota(jnp.int32, (tk, tq), 1)
        old_mask[...] = jnp.where(r > c, 0.0, NEG_INF)
        for h in range(nh):
            qx[h, hd:QX_ROWS, :] = jnp.zeros((QX_ROWS - hd, tq), BF16)

    for h in range(nh):
        qx[h, 0:hd, :] = qt_ref[0, h * hd:(h + 1) * hd, :]

    def gate(branch, h):
        r = branch * nh + h
        return gt_ref[0, r:r + 1, :]

    def run_pipelined(steps):
        work, before = [], {}
        for st in steps:
            if callable(st):
                before.setdefault(len(work), []).append(st)
            else:
                work.append(st)
        pending = {}

        def issue(j):
            for f in before.get(j, ()):
                f()
            pending[j] = work[j][0]()

        for j in range(min(HEAD_LOOKAHEAD, len(work))):
            issue(j)
        for j in range(len(work)):
            if j + HEAD_LOOKAHEAD < len(work):
                issue(j + HEAD_LOOKAHEAD)
            work[j][1](pending.pop(j))

    kc = kc_ref[0]
    vct = vct_ref[0]
    start = pl.multiple_of(112 - (tq // CMP_STRIDE) * i, 16)
    tok = t0 + lax.broadcasted_iota(jnp.int32, (1, tq), 1)
    any_valid = jnp.where(tok >= CMP_BLOCK - 1, 1.0, 0.0)
    psum = []

    def cmp_logits(h):
        return _dot(kc, qx[h, 0:hd, :]) + cmpb_ref[h, pl.ds(start, ncmp), :]

    def cmp_softmax(h, s):
        e = jnp.exp2(s - jnp.max(s, axis=0, keepdims=True))
        prob = e * (any_valid / jnp.sum(e, axis=0, keepdims=True))
        psum[-1] = psum[-1] + prob
        ot[h] = gate(0, h) * _dot(vct, prob.astype(BF16))

    def cmp_steps():
        psum.append(jnp.zeros((ncmp, tq), F32))
        return [(functools.partial(cmp_logits, h), functools.partial(cmp_softmax, h)) for h in range(nh)]

    def select_blocks():
        p_hi = psum[-1].astype(BF16)
        p_lo = (psum[-1] - p_hi.astype(F32)).astype(BF16)
        ovt = ovt_ref[...]
        imp = _dot(ovt, p_hi) + _dot(ovt, p_lo)
        nslc = ovt.shape[0]
        blk = lax.broadcasted_iota(jnp.int32, (nslc, tq), 0)
        cur = (t0 + lax.broadcasted_iota(jnp.int32, (nslc, tq), 1)) // SLC_BLOCK
        forced = (blk == 0) | (blk == cur) | (blk == cur - 1)
        val = jnp.where(forced, FORCE, jnp.where(blk <= cur, imp, -FORCE))
        rank = jnp.zeros((nslc, tq), F32)
        for k in range(nslc):
            vk = val[k:k + 1, :]
            rank = rank + jnp.where(blk > k, jnp.where(vk >= val, 1.0, 0.0), jnp.where(vk > val, 1.0, 0.0))
        selbias = jnp.where(rank < float(min(SLC_TOPK, nslc)), 0.0, NEG_INF).astype(BF16)
        selb[...] = _dot(pm_ref[...], selbias).reshape(selb.shape).astype(BF16)

    def set_sel_rows(c):
        tile = selb[c]
        off = hd + MASK_ROWS * (c % 2)
        if not isinstance(off, int):
            off = pl.multiple_of(off, MASK_ROWS)
        for h in range(nh):
            qx[h, pl.ds(off, MASK_ROWS), :] = tile

    def chunk_steps(row0, selected, mode, first):
        row0 = pl.multiple_of(row0, tk)
        half = tk // 2
        assert first == (mode == "cur")
        acc, m_run = (acc_s, m_sel) if selected else (acc_w, m_win)

        def logits(h):
            kcols = slice(0, QX_ROWS) if selected else slice(KX_WIN, KX_WIN + hd)
            q = qx[h] if selected else qx[h, 0:hd, :]
            if mode == "cur":
                t = tile_ref[h]
                s_top = _dot(kx_ref[0, pl.ds(row0, half), kcols], q) + t
                s_br = _dot(kx_ref[0, pl.ds(row0 + half, half), kcols], q[:, half:]) + t[:, 0:half]
                return s_top, s_br
            s = _dot(kx_ref[0, pl.ds(row0, tk), kcols], q)
            if mode == "prev":
                t = tile_ref[h]
                bot = jnp.concatenate([s[128:256, 0:128] + t[:, 128:256], s[128:256, 128:256]], axis=1)
                s = jnp.concatenate([s[0:128], bot], axis=0)
            elif mode == "old":
                s = s + old_mask[...]
            return s

        def update(h, s):
            if selected:
                vx = vx_ref[0, 0:VX_ROWS, pl.ds(row0, tk)]
            else:
                vx = vx_ref[0, VX_ROWS:2 * VX_ROWS, pl.ds(row0, tk)]
            if mode == "cur":
                s_top, s_br = s
                m_left = jnp.max(s_top[:, 0:half], axis=0, keepdims=True)
                m_right = jnp.maximum(jnp.max(s_top[:, half:], axis=0, keepdims=True),
                                      jnp.max(s_br, axis=0, keepdims=True))
                m_new = jnp.concatenate([m_left, m_right], axis=1)
                p_bot = jnp.concatenate([jnp.zeros((half, half), F32), jnp.exp2(s_br - m_right)], axis=1)
                p = jnp.concatenate([jnp.exp2(s_top - m_new), p_bot], axis=0)
                acc[h] = _dot(vx, p.astype(BF16))
            else:
                m_old = m_run[h]
                m_new = jnp.maximum(m_old, jnp.max(s, axis=0, keepdims=True))
                pv = _dot(vx, jnp.exp2(s - m_new).astype(BF16))
                acc[h] = jnp.exp2(m_old - m_new) * acc[h] + pv
            m_run[h] = m_new

        return [(functools.partial(logits, h), functools.partial(update, h)) for h in range(nh)]

    def main_steps():
        return (cmp_steps()
                + chunk_steps(t0, False, "cur", True)
                + [select_blocks, functools.partial(set_sel_rows, i)]
                + chunk_steps(t0, True, "cur", True))

    def prev_steps():
        return ([functools.partial(set_sel_rows, i - 1)]
                + chunk_steps(t0 - tk, True, "prev", False)
                + chunk_steps(t0 - tk, False, "prev", False))

    @pl.when(i == 0)
    def _():
        run_pipelined(main_steps())

    @pl.when(i == 1)
    def _():
        run_pipelined(main_steps() + prev_steps())

    @pl.when(i >= 2)
    def _():
        run_pipelined(main_steps() + prev_steps() + chunk_steps(t0 - 2 * tk, False, "old", False))

    def far_steps(c):
        return chunk_steps(c * tk, True, "far", False)

    n_far = jnp.maximum(i - 1, 0)

    def far_pair(j, carry):
        set_sel_rows(2 * j)
        set_sel_rows(2 * j + 1)
        run_pipelined(far_steps(2 * j) + far_steps(2 * j + 1))
        return carry

    lax.fori_loop(0, n_far // 2, far_pair, 0)

    @pl.when(n_far % 2 == 1)
    def _():
        set_sel_rows(n_far - 1)
        run_pipelined(far_steps(n_far - 1))

    for h in range(nh):
        a_s = acc_s[h]
        a_w = acc_w[h]
        ot[h] = (ot[h] + gate(1, h) * (a_s[0:hd] / a_s[hd:hd + 1])
                 + gate(2, h) * (a_w[0:hd] / a_w[hd:hd + 1]))
    o_ref[0] = ot[...].reshape(nh * hd, tq).T.astype(BF16)


def _attention(qt, gt, kc, vct, kx, vx, tile, cmp_tab, ovt, pm):
    bsz, _, s = qt.shape
    nh, hd, tq, tk = NSA_HEADS, HEAD_DIM, ATT_Q, ATT_K
    ncmp = kc.shape[1]
    nchunk = s // tk
    return pl.pallas_call(
        _attn_kernel,
        grid=(bsz, s // tq),
        in_specs=[
            pl.BlockSpec((1, NSA_Q, tq), lambda b, i: (b, 0, i)),
            pl.BlockSpec((1, 3 * nh, tq), lambda b, i: (b, 0, i)),
            pl.BlockSpec((1, ncmp, hd), lambda b, i: (b, 0, 0)),
            pl.BlockSpec((1, hd, ncmp), lambda b, i: (b, 0, 0)),
            pl.BlockSpec((1, kx.shape[1], KX_COLS), lambda b, i: (b, 0, 0)),
            pl.BlockSpec((1, 2 * VX_ROWS, vx.shape[2]), lambda b, i: (b, 0, 0)),
            _const_spec(tile.shape),
            _const_spec(cmp_tab.shape),
            _const_spec(ovt.shape),
            _const_spec(pm.shape),
        ],
        out_specs=pl.BlockSpec((1, tq, NSA_Q), lambda b, i: (b, i, 0)),
        out_shape=jax.ShapeDtypeStruct((bsz, s, NSA_Q), BF16),
        scratch_shapes=[
            pltpu.VMEM((nh, QX_ROWS, tq), BF16),
            pltpu.VMEM((nchunk, 16, tq), BF16),
            pltpu.VMEM((tk, tq), F32),
            pltpu.VMEM((nh, VX_ROWS, tq), F32),
            pltpu.VMEM((nh, 1, tq), F32),
            pltpu.VMEM((nh, VX_ROWS, tq), F32),
            pltpu.VMEM((nh, 1, tq), F32),
            pltpu.VMEM((nh, hd, tq), F32),
        ],
        compiler_params=_cparams(("arbitrary", "arbitrary")),
        name="nsa_attention",
    )(qt, gt, kc, vct, kx, vx, tile, cmp_tab, ovt, pm)


def _odd_out_kernel(x_ref, mod_ref, o_ref_in, u_ref, pw_ref, ps_ref, wo_ref, wp_ref, g_ref, b_ref, o_ref, ubuf):
    tm = x_ref.shape[1]
    t = pl.program_id(1)

    @pl.when(t == 0)
    def _():
        ubuf[0:POOL_HALO, :] = jnp.zeros((POOL_HALO, POOL_CH), F32)

    u = u_ref[0]
    ubuf[POOL_HALO:POOL_HALO + tm, :] = u

    lane_group = lax.broadcasted_iota(jnp.int32, (tm, POOL_CH), 1) // POOL_GROUP_CH
    pos = t * tm + lax.broadcasted_iota(jnp.int32, (tm, POOL_CH), 0)
    total = jnp.zeros((tm, POOL_CH), F32)
    width = jnp.ones((tm, POOL_CH), F32)
    sw = ubuf[...]
    span = 1
    for gi, w in enumerate(POOL_WINDOWS):
        while span < w:
            sw = sw + pltpu.roll(sw, span, axis=0)
            span *= 2
        total = jnp.where(lane_group == gi, sw[POOL_HALO:], total)
        width = jnp.where(lane_group == gi, float(w), width)
    cnt = jnp.minimum((pos + 1).astype(F32), width)
    dlt = (total / cnt - u).astype(BF16)
    ubuf[0:POOL_HALO, :] = ubuf[tm:tm + POOL_HALO, :]

    mod = mod_ref[0, 0, 0]
    for r0 in range(0, tm, ODD_OUT_ROWS):
        r1 = r0 + ODD_OUT_ROWS
        o_pool = (_dot(dlt[r0:r1], pw_ref[...]) * ps_ref[...]).astype(BF16)
        y = _dot(o_ref_in[0, r0:r1, :], wo_ref[...]) + _dot(o_pool, wp_ref[...])
        z = DN_ALPHA * x_ref[0, r0:r1, :] + (1.0 + mod[2:3]) * y
        o_ref[0, r0:r1, :] = _layer_norm(z, g_ref[...], b_ref[...])


def _odd_out(x, mod, layer, o_nsa, u, pw, ps, wo, wp, g, b):
    bsz, s, d = x.shape
    tm = min(MIX_ROWS, s)
    return pl.pallas_call(
        _odd_out_kernel,
        grid=(bsz, s // tm),
        in_specs=[
            pl.BlockSpec((1, tm, d), lambda i, t: (i, t, 0)),
            pl.BlockSpec((1, 1, 1, 3, d), lambda i, t: (layer, i, 1, 0, 0)),
            pl.BlockSpec((1, tm, NSA_Q), lambda i, t: (i, t, 0)),
            pl.BlockSpec((1, tm, POOL_CH), lambda i, t: (i, t, 0)),
            _const_spec(pw.shape),
            _const_spec((1, POOL_CH)),
            _const_spec(wo.shape),
            _const_spec(wp.shape),
            _const_spec((1, d)),
            _const_spec((1, d)),
        ],
        out_specs=pl.BlockSpec((1, tm, d), lambda i, t: (i, t, 0)),
        out_shape=jax.ShapeDtypeStruct(x.shape, F32),
        scratch_shapes=[pltpu.VMEM((POOL_HALO + tm, POOL_CH), F32)],
        compiler_params=_cparams(("arbitrary", "arbitrary")),
        name="odd_out",
    )(x, mod, o_nsa, u, pw, ps.reshape(1, POOL_CH), wo, wp, g.reshape(1, d), b.reshape(1, d))


def _odd_in_weights(w_in):
    hd, nh = HEAD_DIM, NSA_HEADS
    c = NSA_Q
    col = lambda k: w_in[:, c + k * hd:c + (k + 1) * hd]
    kc, vc, ks, vs, kw, vw = (col(k) for k in range(6))
    gates = w_in[:, c + 6 * hd:c + 6 * hd + 3 * nh]
    u = w_in[:, c + 6 * hd + 3 * nh:]
    perm = np.array([3 * h + j for j in range(3) for h in range(nh)])
    zc = lambda n: jnp.zeros((w_in.shape[0], n), w_in.dtype)
    w_std = jnp.concatenate([ks, zc(KX_WIN - hd), kw, zc(KX_COLS - KX_WIN - hd), kc, vc, u], axis=1)
    w_t = jnp.concatenate([w_in[:, 0:c], vs, zc(VX_ROWS - hd), vw, zc(VX_ROWS - hd), gates[:, perm]], axis=1)
    return w_std.astype(BF16), w_t.T.astype(BF16)


def _compress_weights(pe_k, pe_v, w1_k, w1_v, w2_k, w2_v):
    hd, hid, half = HEAD_DIM, CMP_HIDDEN, CMP_STRIDE

    def halves(w1):
        w = w1.reshape(CMP_BLOCK, hd, hid)
        return w[:half], w[half:]

    kt, kb = halves(w1_k)
    vt, vb = halves(w1_v)
    z = jnp.zeros((half, hd, hid), F32)

    def assemble(k_part, v_part):
        k_rows = jnp.concatenate([k_part, z], axis=2)
        v_rows = jnp.concatenate([z, v_part], axis=2)
        return jnp.concatenate([k_rows, v_rows], axis=1).reshape(half * 2 * hd, 2 * hid).astype(BF16)

    wt = assemble(kt, vt)
    wb = assemble(kb, vb)
    pe = jnp.concatenate([pe_k.reshape(2, half, hd), pe_v.reshape(2, half, hd)], axis=2)
    pe = pe.reshape(2, half * 2 * hd)
    return pe, wt, wb, w2_k.astype(BF16), w2_v.T.astype(BF16)


def _overlap_matrix_t(n_cols, n_slc):
    n_cmp = n_slc * SLC_BLOCK // CMP_STRIDE - 1
    cmp_start = np.arange(n_cmp) * CMP_STRIDE
    slc_start = np.arange(n_slc) * SLC_BLOCK
    ov = np.clip(np.minimum(cmp_start[:, None] + CMP_BLOCK, slc_start[None, :] + SLC_BLOCK)
                 - np.maximum(cmp_start[:, None], slc_start[None, :]), 0, None) / CMP_BLOCK
    out = np.zeros((n_slc, n_cols), np.float32)
    out[:, :n_cmp] = ov.T
    return jnp.asarray(out, BF16)


def _chunk_row_placement(n_chunk, n_slc):
    per = ATT_K // SLC_BLOCK
    pm = np.zeros((n_chunk * 16, n_slc), np.float32)
    for c in range(n_chunk):
        for b in range(per):
            pm[16 * c + b, per * c + b] = 1.0
    return jnp.asarray(pm, BF16)


def _pool_weight(pool_w):
    ng, gc = len(POOL_WINDOWS), POOL_GROUP_CH
    w = jnp.zeros((ng * gc, ng * gc), F32)
    for gi in range(ng):
        w = w.at[gi * gc:(gi + 1) * gc, gi * gc:(gi + 1) * gc].set(pool_w[gi])
    return w.astype(BF16)


def _odd_mixer(x, mod, layer, w_in, pe_k, pe_v, w1_k, w2_k, w1_v, w2_v, pool_w, pool_scale, w_out,
               bias_tables, g, b):
    bsz, s, _ = x.shape
    assert s % ATT_Q == 0 and s // CMP_STRIDE == LANES and ATT_Q == ATT_K and MIX_ROWS % ATT_K == 0
    w_std, w_t = _odd_in_weights(w_in)
    kx, kvc, u, qt, vx, gt = _odd_proj(x, mod, layer, w_std, w_t)
    kc, vct = _compress(kvc, *_compress_weights(pe_k, pe_v, w1_k, w1_v, w2_k, w2_v))
    tile, cmp_tab = bias_tables
    ovt = _overlap_matrix_t(s // CMP_STRIDE, s // SLC_BLOCK)
    pm = _chunk_row_placement(s // ATT_K, s // SLC_BLOCK)
    o_nsa = _attention(qt, gt, kc, vct, kx, vx, tile, cmp_tab, ovt, pm)
    return _odd_out(x, mod, layer, o_nsa, u, _pool_weight(pool_w), pool_scale,
                    w_out[:NSA_Q].astype(BF16), w_out[NSA_Q:].astype(BF16), g, b)


def kernel(x, c, ada_w, ada_b, ln_g, ln_b, ffn_w_gate, ffn_w_up, ffn_w_down, ev_w_in, ev_conv_a_w,
           ev_conv_a_b, ev_norm_a_g, ev_norm_a_b, ev_conv_b_w, ev_w_out, od_w_in, od_cmp_pe_k,
           od_cmp_pe_v, od_cmp_w1_k, od_cmp_w2_k, od_cmp_w1_v, od_cmp_w2_v, od_pool_w, od_pool_scale,
           od_w_out, rel_bias):
    bsz, s, d = x.shape
    depth = ada_w.shape[0]
    mod = _modulation(c, ada_w, ada_b).reshape(depth, bsz, 3, 3, d)
    bias_tables = _bias_tables(rel_bias)
    weights = tuple(w[0, 0].astype(BF16) for w in (ffn_w_gate, ffn_w_up, ffn_w_down))
    f32_weights = (ffn_w_gate, ffn_w_up, ffn_w_down)
    for layer in range(depth):
        j = layer // 2
        x, weights = _ffn(x, mod, layer, 0, weights, ln_g[layer, 0], ln_b[layer, 0],
                          nxt=(layer, 1) + f32_weights)
        if layer % 2 == 0:
            x = _even_mixer(x, mod, layer, ev_w_in[j].astype(BF16), _sublane_replicated(ev_conv_a_w[j]),
                            ev_conv_a_b[j], ev_norm_a_g[j], ev_norm_a_b[j], ev_conv_b_w[j],
                            ev_w_out[j].astype(BF16), ln_g[layer, 1], ln_b[layer, 1])
        else:
            x = _odd_mixer(x, mod, layer, od_w_in[j], od_cmp_pe_k[j], od_cmp_pe_v[j], od_cmp_w1_k[j],
                           od_cmp_w2_k[j], od_cmp_w1_v[j], od_cmp_w2_v[j], od_pool_w[j], od_pool_scale[j],
                           od_w_out[j], bias_tables, ln_g[layer, 1], ln_b[layer, 1])
        x, weights = _ffn(x, mod, layer, 2, weights, ln_g[layer, 2], ln_b[layer, 2],
                          nxt=(layer + 1, 0) + f32_weights if layer + 1 < depth else None)
    return x
```

```python
import functools
import math

import numpy as np
import jax
import jax.numpy as jnp
from jax import lax
from jax.experimental import pallas as pl
from jax.experimental.pallas import tpu as pltpu

F32 = jnp.float32
BF16 = jnp.bfloat16

DEPTH = 4
FFN_RES_WEIGHT = 0.5
CONV_A_CH = 512
CONV_A_WIDTH = 31
CONV_B_CH = 512
CONV_B_WIDTH = 3
NSA_HEADS = 16
HEAD_DIM = 64
CMP_BLOCK = 32
CMP_STRIDE = 16
CMP_HIDDEN = 128
SLC_BLOCK = 64
SLC_TOPK = 16
WINDOW = 512
POOL_WINDOWS = (2, 4, 8, 16)
POOL_GROUP_CH = 64
POOL_CH = len(POOL_WINDOWS) * POOL_GROUP_CH
REL_BUCKETS = 32
REL_MAX_DIST = 128
DN_ALPHA = (2 * DEPTH) ** 0.25
LN_EPS = 1e-5
NSA_Q = NSA_HEADS * HEAD_DIM
NEG_INF = -1e30
FORCE = 1e30

LANES = 128
SUBLANES = 8
VMEM_LIMIT_BYTES = 56 * 1024 * 1024

FFN_ROWS = 1024
FFN_SUB_ROWS = 256
MIX_ROWS = 512
EVEN_OUT_ROWS = 256
ODD_OUT_ROWS = 256
ATT_Q = 256
ATT_K = 256
CONV_HALO = 32
CONV_B_HALO = 8
POOL_HALO = 16
HEAD_LOOKAHEAD = 5


def _cparams(sem):
    return pltpu.CompilerParams(dimension_semantics=sem, vmem_limit_bytes=VMEM_LIMIT_BYTES)


def _const_spec(shape):
    n = len(shape)
    return pl.BlockSpec(shape, lambda *_: (0,) * n, pipeline_mode=pl.Buffered(1))


def _layer_norm(z, g, b):
    mu = jnp.mean(z, axis=-1, keepdims=True)
    zc = z - mu
    var = jnp.mean(zc * zc, axis=-1, keepdims=True)
    return zc * lax.rsqrt(var + LN_EPS) * g + b


def _silu(v):
    return v * jax.nn.sigmoid(v)


def _dot(a, b):
    return jnp.dot(a, b, preferred_element_type=F32)


def _dot_nt(a, b):
    return lax.dot_general(a, b, (((1,), (1,)), ((), ())), preferred_element_type=F32)


def _mod_kernel(c_ref, w_ref, b_ref, o_ref):
    cond = _silu(c_ref[...])
    o_ref[0] = _dot(cond.astype(BF16), w_ref[0].astype(BF16)) + b_ref[0]


def _modulation(c, ada_w, ada_b):
    depth, d, n = ada_w.shape
    bsz = c.shape[0]
    tn = 1152
    return pl.pallas_call(
        _mod_kernel,
        grid=(depth, n // tn),
        in_specs=[
            pl.BlockSpec((bsz, d), lambda l, j: (0, 0)),
            pl.BlockSpec((1, d, tn), lambda l, j: (l, 0, j)),
            pl.BlockSpec((1, 1, tn), lambda l, j: (l, 0, j)),
        ],
        out_specs=pl.BlockSpec((1, bsz, tn), lambda l, j: (l, 0, j)),
        out_shape=jax.ShapeDtypeStruct((depth, bsz, n), F32),
        compiler_params=_cparams(("arbitrary", "arbitrary")),
        name="modulation",
    )(c, ada_w, ada_b.reshape(depth, 1, n))


def _ffn_kernel(*refs, cast_next):
    if cast_next:
        (x_ref, mod_ref, wg_ref, wu_ref, wd_ref, g_ref, b_ref, ng_ref, nu_ref, nd_ref,
         o_ref, og_ref, ou_ref, od_ref) = refs
        og_ref[...] = ng_ref[0, 0].astype(BF16)
        ou_ref[...] = nu_ref[0, 0].astype(BF16)
        od_ref[...] = nd_ref[0, 0].astype(BF16)
    else:
        x_ref, mod_ref, wg_ref, wu_ref, wd_ref, g_ref, b_ref, o_ref = refs
    mod = mod_ref[0, 0, 0]
    tm = x_ref.shape[1]
    for r0 in range(0, tm, FFN_SUB_ROWS):
        x = x_ref[0, r0:r0 + FFN_SUB_ROWS, :]
        h = (x * (1.0 + mod[1:2]) + mod[0:1]).astype(BF16)
        a = (_silu(_dot(h, wg_ref[...])) * _dot(h, wu_ref[...])).astype(BF16)
        y = _dot(a, wd_ref[...])
        z = DN_ALPHA * x + (FFN_RES_WEIGHT * (1.0 + mod[2:3])) * y
        o_ref[0, r0:r0 + FFN_SUB_ROWS, :] = _layer_norm(z, g_ref[...], b_ref[...])


def _ffn(x, mod, layer, sub, weights, g, b, nxt=None):
    bsz, s, d = x.shape
    wg, wu, wd = weights
    ff = wg.shape[1]
    tm = min(FFN_ROWS, s)
    nt = s // tm
    steps = bsz * nt
    in_specs = [
        pl.BlockSpec((1, tm, d), lambda i, t: (i, t, 0)),
        pl.BlockSpec((1, 1, 1, 3, d), lambda i, t: (layer, i, sub, 0, 0)),
        _const_spec((d, ff)),
        _const_spec((d, ff)),
        _const_spec((ff, d)),
        _const_spec((1, d)),
        _const_spec((1, d)),
    ]
    out_specs = [pl.BlockSpec((1, tm, d), lambda i, t: (i, t, 0))]
    out_shape = [jax.ShapeDtypeStruct(x.shape, F32)]
    args = [x, mod, wg, wu, wd, g.reshape(1, d), b.reshape(1, d)]
    if nxt is not None:
        nl, nh, w_gate, w_up, w_down = nxt
        up_rows = d // steps
        down_steps = steps // 2
        down_rows = ff // down_steps
        assert d % steps == 0 and up_rows % 16 == 0 and ff % down_steps == 0 and down_rows % 16 == 0
        in_specs += [
            pl.BlockSpec((1, 1, up_rows, ff), lambda i, t: (nl, nh, i * nt + t, 0)),
            pl.BlockSpec((1, 1, up_rows, ff), lambda i, t: (nl, nh, i * nt + t, 0)),
            pl.BlockSpec((1, 1, down_rows, d), lambda i, t: (nl, nh, (i * nt + t) // 2, 0)),
        ]
        out_specs += [
            pl.BlockSpec((up_rows, ff), lambda i, t: (i * nt + t, 0)),
            pl.BlockSpec((up_rows, ff), lambda i, t: (i * nt + t, 0)),
            pl.BlockSpec((down_rows, d), lambda i, t: ((i * nt + t) // 2, 0)),
        ]
        out_shape += [jax.ShapeDtypeStruct((d, ff), BF16), jax.ShapeDtypeStruct((d, ff), BF16),
                      jax.ShapeDtypeStruct((ff, d), BF16)]
        args += [w_gate, w_up, w_down]
    out = pl.pallas_call(
        functools.partial(_ffn_kernel, cast_next=nxt is not None),
        grid=(bsz, nt),
        in_specs=in_specs,
        out_specs=out_specs,
        out_shape=out_shape,
        compiler_params=_cparams(("arbitrary", "arbitrary")),
        name="ffn",
    )(*args)
    return out[0], (tuple(out[1:]) if nxt is not None else None)


def _even_kernel(x_ref, mod_ref, win_ref, caw_ref, cab_ref, nag_ref, nab_ref, cbw_ref, wout_ref,
                 g_ref, b_ref, o_ref, abuf, ash, bbuf, cat):
    tm = x_ref.shape[1]
    ca, cb = CONV_A_CH, CONV_B_CH

    @pl.when(pl.program_id(1) == 0)
    def _():
        abuf[0:CONV_HALO, :] = jnp.zeros((CONV_HALO, ca), F32)
        bbuf[0:CONV_B_HALO, :] = jnp.zeros((CONV_B_HALO, cb), F32)

    x = x_ref[0]
    mod = mod_ref[0, 0, 0]
    h = (x * (1.0 + mod[1:2]) + mod[0:1]).astype(BF16)
    p = _dot(h, win_ref[...])
    abuf[CONV_HALO:CONV_HALO + tm, :] = p[:, 0:ca] * jax.nn.sigmoid(p[:, ca:2 * ca])
    gate_b = p[:, 2 * ca:2 * ca + cb]
    bbuf[CONV_B_HALO:CONV_B_HALO + tm, :] = p[:, 2 * ca + cb:2 * ca + 2 * cb] * p[:, 2 * ca + 2 * cb:]

    span = tm + CONV_HALO - SUBLANES
    for r in range(1, SUBLANES):
        ash[r - 1] = abuf[r:r + span, :]

    rows = 64
    for r0 in range(0, tm, rows):
        acc = jnp.zeros((rows // SUBLANES, SUBLANES, ca), F32)
        for k in range(CONV_A_WIDTH):
            off = CONV_HALO + r0 - (CONV_A_WIDTH - 1) + k
            r = off % SUBLANES
            tap = abuf[off:off + rows, :] if r == 0 else ash[r - 1, off - r:off - r + rows, :]
            acc = acc + caw_ref[k][None] * tap.reshape(rows // SUBLANES, SUBLANES, ca)
        acc = acc.reshape(rows, ca)
        u = _layer_norm(acc + cab_ref[...], nag_ref[...], nab_ref[...])
        cat[r0:r0 + rows, 0:ca] = _silu(u).astype(BF16)
        accb = jnp.zeros((rows, cb), F32)
        for k in range(CONV_B_WIDTH):
            off = CONV_B_HALO + r0 - (CONV_B_WIDTH - 1) + k
            accb = accb + cbw_ref[k:k + 1, :] * bbuf[off:off + rows, :]
        cat[r0:r0 + rows, ca:ca + cb] = (gate_b[r0:r0 + rows] * accb).astype(BF16)

        s1 = r0 + rows
        if s1 % EVEN_OUT_ROWS == 0:
            s0 = s1 - EVEN_OUT_ROWS
            y = _dot(cat[s0:s1, :], wout_ref[...])
            z = DN_ALPHA * x[s0:s1] + (1.0 + mod[2:3]) * y
            o_ref[0, s0:s1, :] = _layer_norm(z, g_ref[...], b_ref[...])

    abuf[0:CONV_HALO, :] = abuf[tm:tm + CONV_HALO, :]
    bbuf[0:CONV_B_HALO, :] = bbuf[tm:tm + CONV_B_HALO, :]


def _even_mixer(x, mod, layer, win, caw, cab, nag, nab, cbw, wout, g, b):
    bsz, s, d = x.shape
    tm = min(MIX_ROWS, s)
    ca, cb = CONV_A_CH, CONV_B_CH
    return pl.pallas_call(
        _even_kernel,
        grid=(bsz, s // tm),
        in_specs=[
            pl.BlockSpec((1, tm, d), lambda i, t: (i, t, 0)),
            pl.BlockSpec((1, 1, 1, 3, d), lambda i, t: (layer, i, 1, 0, 0)),
            _const_spec(win.shape),
            _const_spec(caw.shape),
            _const_spec((1, ca)),
            _const_spec((1, ca)),
            _const_spec((1, ca)),
            _const_spec(cbw.shape),
            _const_spec(wout.shape),
            _const_spec((1, d)),
            _const_spec((1, d)),
        ],
        out_specs=pl.BlockSpec((1, tm, d), lambda i, t: (i, t, 0)),
        out_shape=jax.ShapeDtypeStruct(x.shape, F32),
        scratch_shapes=[
            pltpu.VMEM((CONV_HALO + tm, ca), F32),
            pltpu.VMEM((SUBLANES - 1, CONV_HALO + tm - SUBLANES, ca), F32),
            pltpu.VMEM((CONV_B_HALO + tm, cb), F32),
            pltpu.VMEM((tm, ca + cb), BF16),
        ],
        compiler_params=_cparams(("arbitrary", "arbitrary")),
        name="even_mixer",
    )(x, mod, win, caw, cab.reshape(1, ca), nag.reshape(1, ca), nab.reshape(1, ca), cbw, wout,
      g.reshape(1, d), b.reshape(1, d))


def _sublane_replicated(w):
    return jnp.broadcast_to(w[:, None, :], (w.shape[0], SUBLANES, w.shape[1]))


KX_COLS = 256
KX_WIN = 128
MASK_ROWS = 16
QX_ROWS = HEAD_DIM + 2 * MASK_ROWS
VX_ROWS = HEAD_DIM + 16
LOG2E = math.log2(math.e)


def _odd_proj_kernel(x_ref, mod_ref, w_ref, wt_ref, kx_ref, kvc_ref, u_ref, qt_ref, vx_ref, gt_ref):
    tm = x_ref.shape[1]
    x = x_ref[0]
    mod = mod_ref[0, 0, 0]
    h = (x * (1.0 + mod[1:2]) + mod[0:1]).astype(BF16)
    p = _dot(h, w_ref[...])
    c1 = KX_COLS
    c2 = c1 + 2 * HEAD_DIM
    row = lax.broadcasted_iota(jnp.int32, (tm, KX_COLS), 0)
    col = lax.broadcasted_iota(jnp.int32, (tm, KX_COLS), 1)
    first = HEAD_DIM + MASK_ROWS * ((row // ATT_K) % 2)
    member = (col - first == (row % ATT_K) // SLC_BLOCK) & (col >= first)
    kx_ref[0] = jnp.where(member, 1.0, p[:, 0:c1]).astype(BF16)
    kvc_ref[0] = p[:, c1:c2]
    u_ref[0] = p[:, c2:]
    pt = _dot_nt(wt_ref[...], h)
    r1 = NSA_Q
    r2 = r1 + 2 * VX_ROWS
    qt_ref[0] = (pt[0:r1] * (HEAD_DIM ** -0.5 * LOG2E)).astype(BF16)
    vrow = lax.broadcasted_iota(jnp.int32, (2 * VX_ROWS, tm), 0)
    vx_ref[0] = jnp.where(vrow % VX_ROWS == HEAD_DIM, 1.0, pt[r1:r2]).astype(BF16)
    gt_ref[0] = jax.nn.sigmoid(pt[r2:])


def _odd_proj(x, mod, layer, w, wt):
    bsz, s, d = x.shape
    tm = min(MIX_ROWS, s)

    def blk(c):
        return pl.BlockSpec((1, tm, c), lambda i, t: (i, t, 0))

    def blk_t(r):
        return pl.BlockSpec((1, r, tm), lambda i, t: (i, 0, t))

    return pl.pallas_call(
        _odd_proj_kernel,
        grid=(bsz, s // tm),
        in_specs=[
            blk(d),
            pl.BlockSpec((1, 1, 1, 3, d), lambda i, t: (layer, i, 1, 0, 0)),
            _const_spec(w.shape),
            _const_spec(wt.shape),
        ],
        out_specs=[blk(KX_COLS), blk(2 * HEAD_DIM), blk(POOL_CH),
                   blk_t(NSA_Q), blk_t(2 * VX_ROWS), blk_t(3 * NSA_HEADS)],
        out_shape=[
            jax.ShapeDtypeStruct((bsz, s, KX_COLS), BF16),
            jax.ShapeDtypeStruct((bsz, s, 2 * HEAD_DIM), F32),
            jax.ShapeDtypeStruct((bsz, s, POOL_CH), F32),
            jax.ShapeDtypeStruct((bsz, NSA_Q, s), BF16),
            jax.ShapeDtypeStruct((bsz, 2 * VX_ROWS, s), BF16),
            jax.ShapeDtypeStruct((bsz, 3 * NSA_HEADS, s), F32),
        ],
        compiler_params=_cparams(("arbitrary", "arbitrary")),
        name="odd_proj",
    )(x, mod, w, wt)


def _compress_kernel(x_ref, pe_ref, wt_ref, wb_ref, w2k_ref, w2vt_ref, kc_ref, vct_ref):
    nrow = kc_ref.shape[1]
    width = x_ref.shape[2]
    top = jnp.zeros((nrow, wt_ref.shape[1]), F32)
    bot = jnp.zeros((nrow, wb_ref.shape[1]), F32)
    for j in range(CMP_STRIDE):
        xj = x_ref[0, pl.ds(j, nrow, stride=CMP_STRIDE), :]
        cols = slice(j * width, (j + 1) * width)
        top = top + _dot((xj + pe_ref[0:1, cols]).astype(BF16), wt_ref[cols, :])
        bot = bot + _dot((xj + pe_ref[1:2, cols]).astype(BF16), wb_ref[cols, :])
    hid = top + pltpu.roll(bot, bot.shape[0] - 1, axis=0)
    act = _silu(hid).astype(BF16)
    kc_ref[0] = _dot(act[:, 0:CMP_HIDDEN], w2k_ref[...]).astype(BF16)
    vct_ref[0] = _dot_nt(w2vt_ref[...], act[:, CMP_HIDDEN:]).astype(BF16)


def _compress(kvc, pe, wt, wb, w2k, w2vt):
    bsz, s, _ = kvc.shape
    nrow = s // CMP_STRIDE
    return pl.pallas_call(
        _compress_kernel,
        grid=(bsz,),
        in_specs=[
            pl.BlockSpec((1, s, 2 * HEAD_DIM), lambda i: (i, 0, 0)),
            _const_spec(pe.shape),
            _const_spec(wt.shape),
            _const_spec(wb.shape),
            _const_spec(w2k.shape),
            _const_spec(w2vt.shape),
        ],
        out_specs=[pl.BlockSpec((1, nrow, HEAD_DIM), lambda i: (i, 0, 0)),
                   pl.BlockSpec((1, HEAD_DIM, nrow), lambda i: (i, 0, 0))],
        out_shape=[jax.ShapeDtypeStruct((bsz, nrow, HEAD_DIM), BF16),
                   jax.ShapeDtypeStruct((bsz, HEAD_DIM, nrow), BF16)],
        compiler_params=_cparams(("arbitrary",)),
        name="compress",
    )(kvc, pe, wt, wb, w2k, w2vt)


CMP_TAB_SHIFT = LANES - ATT_Q // CMP_STRIDE
CMP_TAB_ROWS = CMP_TAB_SHIFT + LANES


def _t5_bucket_table(max_dist):
    n = np.arange(max_dist, dtype=np.int64)
    exact = REL_BUCKETS // 2
    nf = np.maximum(n, 1).astype(np.float32)
    large = exact + (np.log(nf / np.float32(exact)) / np.float32(math.log(REL_MAX_DIST / exact))
                     * np.float32(REL_BUCKETS - exact)).astype(np.int32)
    return np.where(n < exact, n, np.minimum(large, REL_BUCKETS - 1)).astype(np.int32)


def _bias_tables(rel_bias):
    far = REL_BUCKETS - 1
    table = _t5_bucket_table(1024)

    def buckets(dist):
        return np.where(dist >= 0, table[np.clip(dist, 0, 1023)], REL_BUCKETS).reshape(-1)

    r = np.arange(ATT_K // 2)[:, None]
    c = np.arange(ATT_Q)[None, :]
    jj = np.arange(CMP_TAB_ROWS)[:, None]
    bucket = np.concatenate([buckets(c - r), buckets(c - CMP_STRIDE * (jj - CMP_TAB_SHIFT) - (CMP_BLOCK - 1))])
    onehot = (jnp.asarray(bucket)[None, :] == jnp.arange(REL_BUCKETS + 1)[:, None]).astype(F32)
    rb = (rel_bias.astype(F32) - rel_bias[far:far + 1].astype(F32)).T * LOG2E
    rb = jnp.concatenate([rb, jnp.full((rb.shape[0], 1), NEG_INF, F32)], axis=1)
    flat = jnp.dot(rb, onehot, precision=lax.Precision.HIGHEST)
    nh = rel_bias.shape[1]
    n_tile = (ATT_K // 2) * ATT_Q
    tile = flat[:, :n_tile].reshape(nh, ATT_K // 2, ATT_Q)
    cmp_tab = flat[:, n_tile:].reshape(nh, CMP_TAB_ROWS, ATT_Q)
    return tile, cmp_tab


def _attn_kernel(qt_ref, gt_ref, kc_ref, vct_ref, kx_ref, vx_ref, tile_ref, cmpb_ref, ovt_ref, pm_ref,
                 o_ref, qx, selb, old_mask, acc_s, m_sel, acc_w, m_win, ot):
    nh, hd = NSA_HEADS, HEAD_DIM
    tq, tk = ATT_Q, ATT_K
    i = pl.program_id(1)
    t0 = i * tq
    ncmp = kc_ref.shape[1]

    @pl.when(i == 0)
    def _():
        r = lax.broadcasted_iota(jnp.int32, (tk // 2, tq // 2), 0)
        c = lax.broadcasted_iota(jnp.int32, (tk // 2, tq // 2), 1)
        old_mask[...] = jnp.where(r > c, 0.0, NEG_INF)
        for h in range(nh):
            qx[h, hd:QX_ROWS, :] = jnp.zeros((QX_ROWS - hd, tq), BF16)

    for h in range(nh):
        qx[h, 0:hd, :] = qt_ref[0, h * hd:(h + 1) * hd, :]

    def gate(branch, h):
        r = branch * nh + h
        return gt_ref[0, r:r + 1, :]

    def run_pipelined(steps):
        work, before = [], {}
        for st in steps:
            if callable(st):
                before.setdefault(len(work), []).append(st)
            else:
                work.append(st)
        pending = {}

        def issue(j):
            for f in before.get(j, ()):
                f()
            pending[j] = work[j][0]()

        for j in range(min(HEAD_LOOKAHEAD, len(work))):
            issue(j)
        for j in range(len(work)):
            if j + HEAD_LOOKAHEAD < len(work):
                issue(j + HEAD_LOOKAHEAD)
            work[j][1](pending.pop(j))

    kc = kc_ref[0]
    vct = vct_ref[0]
    start = pl.multiple_of(CMP_TAB_SHIFT - (tq // CMP_STRIDE) * i, 16)
    tok = t0 + lax.broadcasted_iota(jnp.int32, (1, tq), 1)
    any_valid = jnp.where(tok >= CMP_BLOCK - 1, 1.0, 0.0)
    psum = []

    def cmp_logits(h):
        return _dot(kc, qx[h, 0:hd, :]) + cmpb_ref[h, pl.ds(start, ncmp), :]

    def cmp_softmax(h, s):
        e = jnp.exp2(s - jnp.max(s, axis=0, keepdims=True))
        prob = e * (any_valid / jnp.sum(e, axis=0, keepdims=True))
        psum[-1] = psum[-1] + prob
        ot[h] = gate(0, h) * _dot(vct, prob.astype(BF16))

    def cmp_steps():
        psum.append(jnp.zeros((ncmp, tq), F32))
        return [(functools.partial(cmp_logits, h), functools.partial(cmp_softmax, h)) for h in range(nh)]

    def select_blocks():
        p_hi = psum[-1].astype(BF16)
        p_lo = (psum[-1] - p_hi.astype(F32)).astype(BF16)
        ovt = ovt_ref[...]
        imp = _dot(ovt, p_hi) + _dot(ovt, p_lo)
        nslc = ovt.shape[0]
        blk = lax.broadcasted_iota(jnp.int32, (nslc, tq), 0)
        cur = (t0 + lax.broadcasted_iota(jnp.int32, (nslc, tq), 1)) // SLC_BLOCK
        forced = (blk == 0) | (blk == cur) | (blk == cur - 1)
        val = jnp.where(forced, FORCE, jnp.where(blk <= cur, imp, -FORCE))
        rank = jnp.zeros((nslc, tq), F32)
        for k in range(nslc):
            vk = val[k:k + 1, :]
            rank = rank + jnp.where(blk > k, jnp.where(vk >= val, 1.0, 0.0), jnp.where(vk > val, 1.0, 0.0))
        selbias = jnp.where(rank < float(min(SLC_TOPK, nslc)), 0.0, NEG_INF).astype(BF16)
        selb[...] = _dot(pm_ref[...], selbias).reshape(selb.shape).astype(BF16)

    def set_sel_rows(c):
        tile = selb[c]
        off = hd + MASK_ROWS * (c % 2)
        if not isinstance(off, int):
            off = pl.multiple_of(off, MASK_ROWS)
        for h in range(nh):
            qx[h, pl.ds(off, MASK_ROWS), :] = tile

    def chunk_steps(row0, selected, mode, first):
        row0 = pl.multiple_of(row0, tk)
        half = tk // 2
        assert first == (mode == "cur")
        acc, m_run = (acc_s, m_sel) if selected else (acc_w, m_win)

        def logits(h):
            kcols = slice(0, QX_ROWS) if selected else slice(KX_WIN, KX_WIN + hd)
            q = qx[h] if selected else qx[h, 0:hd, :]
            if mode == "cur":
                t = tile_ref[h]
                s_top = _dot(kx_ref[0, pl.ds(row0, half), kcols], q) + t
                s_br = _dot(kx_ref[0, pl.ds(row0 + half, half), kcols], q[:, half:]) + t[:, 0:half]
                return s_top, s_br
            if mode == "old":
                tri = old_mask[...]
                s_tl = _dot(kx_ref[0, pl.ds(row0, half), kcols], q[:, 0:half]) + tri
                s_bot = _dot(kx_ref[0, pl.ds(row0 + half, half), kcols], q)
                s_bot = jnp.concatenate([s_bot[:, 0:half], s_bot[:, half:] + tri], axis=1)
                return s_tl, s_bot
            s = _dot(kx_ref[0, pl.ds(row0, tk), kcols], q)
            if mode == "prev":
                t = tile_ref[h]
                bot = jnp.concatenate([s[half:, 0:half] + t[:, half:], s[half:, half:]], axis=1)
                s = jnp.concatenate([s[0:half], bot], axis=0)
            return s

        def update(h, s):
            if selected:
                vx = vx_ref[0, 0:VX_ROWS, pl.ds(row0, tk)]
            else:
                vx = vx_ref[0, VX_ROWS:2 * VX_ROWS, pl.ds(row0, tk)]
            if mode == "cur":
                s_top, s_br = s
                m_left = jnp.max(s_top[:, 0:half], axis=0, keepdims=True)
                m_right = jnp.maximum(jnp.max(s_top[:, half:], axis=0, keepdims=True),
                                      jnp.max(s_br, axis=0, keepdims=True))
                m_new = jnp.concatenate([m_left, m_right], axis=1)
                p_bot = jnp.concatenate([jnp.zeros((half, half), F32), jnp.exp2(s_br - m_right)], axis=1)
                p = jnp.concatenate([jnp.exp2(s_top - m_new), p_bot], axis=0)
                acc[h] = _dot(vx, p.astype(BF16))
            elif mode == "old":
                s_tl, s_bot = s
                m_old = m_run[h]
                m_left = jnp.maximum(jnp.max(s_tl, axis=0, keepdims=True),
                                     jnp.max(s_bot[:, 0:half], axis=0, keepdims=True))
                m_chunk = jnp.concatenate([m_left, jnp.max(s_bot[:, half:], axis=0, keepdims=True)], axis=1)
                m_new = jnp.maximum(m_old, m_chunk)
                p_top = jnp.concatenate([jnp.exp2(s_tl - m_new[:, 0:half]), jnp.zeros((half, half), F32)], axis=1)
                p = jnp.concatenate([p_top, jnp.exp2(s_bot - m_new)], axis=0)
                acc[h] = jnp.exp2(m_old - m_new) * acc[h] + _dot(vx, p.astype(BF16))
            else:
                m_old = m_run[h]
                m_new = jnp.maximum(m_old, jnp.max(s, axis=0, keepdims=True))
                pv = _dot(vx, jnp.exp2(s - m_new).astype(BF16))
                acc[h] = jnp.exp2(m_old - m_new) * acc[h] + pv
            m_run[h] = m_new

        return [(functools.partial(logits, h), functools.partial(update, h)) for h in range(nh)]

    def main_steps():
        return (cmp_steps()
                + chunk_steps(t0, False, "cur", True)
                + [select_blocks, functools.partial(set_sel_rows, i)]
                + chunk_steps(t0, True, "cur", True))

    def prev_steps():
        return ([functools.partial(set_sel_rows, i - 1)]
                + chunk_steps(t0 - tk, True, "prev", False)
                + chunk_steps(t0 - tk, False, "prev", False))

    @pl.when(i == 0)
    def _():
        run_pipelined(main_steps())

    @pl.when(i == 1)
    def _():
        run_pipelined(main_steps() + prev_steps())

    @pl.when(i >= 2)
    def _():
        run_pipelined(main_steps() + prev_steps() + chunk_steps(t0 - 2 * tk, False, "old", False))

    def far_steps(c):
        return chunk_steps(c * tk, True, "far", False)

    n_far = jnp.maximum(i - 1, 0)

    def far_pair(j, carry):
        set_sel_rows(2 * j)
        set_sel_rows(2 * j + 1)
        run_pipelined(far_steps(2 * j) + far_steps(2 * j + 1))
        return carry

    lax.fori_loop(0, n_far // 2, far_pair, 0)

    @pl.when(n_far % 2 == 1)
    def _():
        set_sel_rows(n_far - 1)
        run_pipelined(far_steps(n_far - 1))

    for h in range(nh):
        a_s = acc_s[h]
        a_w = acc_w[h]
        ot[h] = (ot[h] + gate(1, h) * (a_s[0:hd] / a_s[hd:hd + 1])
                 + gate(2, h) * (a_w[0:hd] / a_w[hd:hd + 1]))
    o_ref[0] = ot[...].reshape(nh * hd, tq).T.astype(BF16)


def _attention(qt, gt, kc, vct, kx, vx, tile, cmp_tab, ovt, pm):
    bsz, _, s = qt.shape
    nh, hd, tq, tk = NSA_HEADS, HEAD_DIM, ATT_Q, ATT_K
    ncmp = kc.shape[1]
    nchunk = s // tk
    return pl.pallas_call(
        _attn_kernel,
        grid=(bsz, s // tq),
        in_specs=[
            pl.BlockSpec((1, NSA_Q, tq), lambda b, i: (b, 0, i)),
            pl.BlockSpec((1, 3 * nh, tq), lambda b, i: (b, 0, i)),
            pl.BlockSpec((1, ncmp, hd), lambda b, i: (b, 0, 0)),
            pl.BlockSpec((1, hd, ncmp), lambda b, i: (b, 0, 0)),
            pl.BlockSpec((1, kx.shape[1], KX_COLS), lambda b, i: (b, 0, 0)),
            pl.BlockSpec((1, 2 * VX_ROWS, vx.shape[2]), lambda b, i: (b, 0, 0)),
            _const_spec(tile.shape),
            _const_spec(cmp_tab.shape),
            _const_spec(ovt.shape),
            _const_spec(pm.shape),
        ],
        out_specs=pl.BlockSpec((1, tq, NSA_Q), lambda b, i: (b, i, 0)),
        out_shape=jax.ShapeDtypeStruct((bsz, s, NSA_Q), BF16),
        scratch_shapes=[
            pltpu.VMEM((nh, QX_ROWS, tq), BF16),
            pltpu.VMEM((nchunk, MASK_ROWS, tq), BF16),
            pltpu.VMEM((tk // 2, tq // 2), F32),
            pltpu.VMEM((nh, VX_ROWS, tq), F32),
            pltpu.VMEM((nh, 1, tq), F32),
            pltpu.VMEM((nh, VX_ROWS, tq), F32),
            pltpu.VMEM((nh, 1, tq), F32),
            pltpu.VMEM((nh, hd, tq), F32),
        ],
        compiler_params=_cparams(("arbitrary", "arbitrary")),
        name="nsa_attention",
    )(qt, gt, kc, vct, kx, vx, tile, cmp_tab, ovt, pm)


def _odd_out_kernel(x_ref, mod_ref, o_ref_in, u_ref, pw_ref, ps_ref, wo_ref, wp_ref, g_ref, b_ref, o_ref, ubuf):
    tm = x_ref.shape[1]
    t = pl.program_id(1)

    @pl.when(t == 0)
    def _():
        ubuf[0:POOL_HALO, :] = jnp.zeros((POOL_HALO, POOL_CH), F32)

    u = u_ref[0]
    ubuf[POOL_HALO:POOL_HALO + tm, :] = u

    lane_group = lax.broadcasted_iota(jnp.int32, (tm, POOL_CH), 1) // POOL_GROUP_CH
    pos = t * tm + lax.broadcasted_iota(jnp.int32, (tm, POOL_CH), 0)
    total = jnp.zeros((tm, POOL_CH), F32)
    width = jnp.ones((tm, POOL_CH), F32)
    sw = ubuf[...]
    span = 1
    for gi, w in enumerate(POOL_WINDOWS):
        while span < w:
            sw = sw + pltpu.roll(sw, span, axis=0)
            span *= 2
        total = jnp.where(lane_group == gi, sw[POOL_HALO:], total)
        width = jnp.where(lane_group == gi, float(w), width)
    cnt = jnp.minimum((pos + 1).astype(F32), width)
    dlt = (total / cnt - u).astype(BF16)
    ubuf[0:POOL_HALO, :] = ubuf[tm:tm + POOL_HALO, :]

    mod = mod_ref[0, 0, 0]
    for r0 in range(0, tm, ODD_OUT_ROWS):
        r1 = r0 + ODD_OUT_ROWS
        o_pool = (_dot(dlt[r0:r1], pw_ref[...]) * ps_ref[...]).astype(BF16)
        y = _dot(o_ref_in[0, r0:r1, :], wo_ref[...]) + _dot(o_pool, wp_ref[...])
        z = DN_ALPHA * x_ref[0, r0:r1, :] + (1.0 + mod[2:3]) * y
        o_ref[0, r0:r1, :] = _layer_norm(z, g_ref[...], b_ref[...])


def _odd_out(x, mod, layer, o_nsa, u, pw, ps, wo, wp, g, b):
    bsz, s, d = x.shape
    tm = min(MIX_ROWS, s)
    return pl.pallas_call(
        _odd_out_kernel,
        grid=(bsz, s // tm),
        in_specs=[
            pl.BlockSpec((1, tm, d), lambda i, t: (i, t, 0)),
            pl.BlockSpec((1, 1, 1, 3, d), lambda i, t: (layer, i, 1, 0, 0)),
            pl.BlockSpec((1, tm, NSA_Q), lambda i, t: (i, t, 0)),
            pl.BlockSpec((1, tm, POOL_CH), lambda i, t: (i, t, 0)),
            _const_spec(pw.shape),
            _const_spec((1, POOL_CH)),
            _const_spec(wo.shape),
            _const_spec(wp.shape),
            _const_spec((1, d)),
            _const_spec((1, d)),
        ],
        out_specs=pl.BlockSpec((1, tm, d), lambda i, t: (i, t, 0)),
        out_shape=jax.ShapeDtypeStruct(x.shape, F32),
        scratch_shapes=[pltpu.VMEM((POOL_HALO + tm, POOL_CH), F32)],
        compiler_params=_cparams(("arbitrary", "arbitrary")),
        name="odd_out",
    )(x, mod, o_nsa, u, pw, ps.reshape(1, POOL_CH), wo, wp, g.reshape(1, d), b.reshape(1, d))


def _odd_in_weights(w_in):
    hd, nh = HEAD_DIM, NSA_HEADS
    c = NSA_Q
    col = lambda k: w_in[:, c + k * hd:c + (k + 1) * hd]
    kc, vc, ks, vs, kw, vw = (col(k) for k in range(6))
    gates = w_in[:, c + 6 * hd:c + 6 * hd + 3 * nh]
    u = w_in[:, c + 6 * hd + 3 * nh:]
    perm = np.array([3 * h + j for j in range(3) for h in range(nh)])
    zc = lambda n: jnp.zeros((w_in.shape[0], n), w_in.dtype)
    w_std = jnp.concatenate([ks, zc(KX_WIN - hd), kw, zc(KX_COLS - KX_WIN - hd), kc, vc, u], axis=1)
    w_t = jnp.concatenate([w_in[:, 0:c], vs, zc(VX_ROWS - hd), vw, zc(VX_ROWS - hd), gates[:, perm]], axis=1)
    return w_std.astype(BF16), w_t.T.astype(BF16)


def _compress_weights(pe_k, pe_v, w1_k, w1_v, w2_k, w2_v):
    hd, hid, half = HEAD_DIM, CMP_HIDDEN, CMP_STRIDE

    def halves(w1):
        w = w1.reshape(CMP_BLOCK, hd, hid)
        return w[:half], w[half:]

    kt, kb = halves(w1_k)
    vt, vb = halves(w1_v)
    z = jnp.zeros((half, hd, hid), F32)

    def assemble(k_part, v_part):
        k_rows = jnp.concatenate([k_part, z], axis=2)
        v_rows = jnp.concatenate([z, v_part], axis=2)
        return jnp.concatenate([k_rows, v_rows], axis=1).reshape(half * 2 * hd, 2 * hid).astype(BF16)

    wt = assemble(kt, vt)
    wb = assemble(kb, vb)
    pe = jnp.concatenate([pe_k.reshape(2, half, hd), pe_v.reshape(2, half, hd)], axis=2)
    pe = pe.reshape(2, half * 2 * hd)
    return pe, wt, wb, w2_k.astype(BF16), w2_v.T.astype(BF16)


def _overlap_matrix_t(n_cols, n_slc):
    n_cmp = n_slc * SLC_BLOCK // CMP_STRIDE - 1
    cmp_start = np.arange(n_cmp) * CMP_STRIDE
    slc_start = np.arange(n_slc) * SLC_BLOCK
    ov = np.clip(np.minimum(cmp_start[:, None] + CMP_BLOCK, slc_start[None, :] + SLC_BLOCK)
                 - np.maximum(cmp_start[:, None], slc_start[None, :]), 0, None) / CMP_BLOCK
    out = np.zeros((n_slc, n_cols), np.float32)
    out[:, :n_cmp] = ov.T
    return jnp.asarray(out, BF16)


def _chunk_row_placement(n_chunk, n_slc):
    per = ATT_K // SLC_BLOCK
    pm = np.zeros((n_chunk * MASK_ROWS, n_slc), np.float32)
    for c in range(n_chunk):
        for b in range(per):
            pm[MASK_ROWS * c + b, per * c + b] = 1.0
    return jnp.asarray(pm, BF16)


def _pool_weight(pool_w):
    ng, gc = len(POOL_WINDOWS), POOL_GROUP_CH
    w = jnp.zeros((ng * gc, ng * gc), F32)
    for gi in range(ng):
        w = w.at[gi * gc:(gi + 1) * gc, gi * gc:(gi + 1) * gc].set(pool_w[gi])
    return w.astype(BF16)


def _odd_mixer(x, mod, layer, w_in, pe_k, pe_v, w1_k, w2_k, w1_v, w2_v, pool_w, pool_scale, w_out,
               bias_tables, g, b):
    bsz, s, _ = x.shape
    assert s % ATT_Q == 0 and s // CMP_STRIDE == LANES and ATT_Q == ATT_K and MIX_ROWS % ATT_K == 0
    w_std, w_t = _odd_in_weights(w_in)
    kx, kvc, u, qt, vx, gt = _odd_proj(x, mod, layer, w_std, w_t)
    kc, vct = _compress(kvc, *_compress_weights(pe_k, pe_v, w1_k, w1_v, w2_k, w2_v))
    tile, cmp_tab = bias_tables
    ovt = _overlap_matrix_t(s // CMP_STRIDE, s // SLC_BLOCK)
    pm = _chunk_row_placement(s // ATT_K, s // SLC_BLOCK)
    o_nsa = _attention(qt, gt, kc, vct, kx, vx, tile, cmp_tab, ovt, pm)
    return _odd_out(x, mod, layer, o_nsa, u, _pool_weight(pool_w), pool_scale,
                    w_out[:NSA_Q].astype(BF16), w_out[NSA_Q:].astype(BF16), g, b)


def kernel(x, c, ada_w, ada_b, ln_g, ln_b, ffn_w_gate, ffn_w_up, ffn_w_down, ev_w_in, ev_conv_a_w,
           ev_conv_a_b, ev_norm_a_g, ev_norm_a_b, ev_conv_b_w, ev_w_out, od_w_in, od_cmp_pe_k,
           od_cmp_pe_v, od_cmp_w1_k, od_cmp_w2_k, od_cmp_w1_v, od_cmp_w2_v, od_pool_w, od_pool_scale,
           od_w_out, rel_bias):
    bsz, s, d = x.shape
    depth = ada_w.shape[0]
    mod = _modulation(c, ada_w, ada_b).reshape(depth, bsz, 3, 3, d)
    bias_tables = _bias_tables(rel_bias)
    weights = tuple(w[0, 0].astype(BF16) for w in (ffn_w_gate, ffn_w_up, ffn_w_down))
    f32_weights = (ffn_w_gate, ffn_w_up, ffn_w_down)
    for layer in range(depth):
        j = layer // 2
        x, weights = _ffn(x, mod, layer, 0, weights, ln_g[layer, 0], ln_b[layer, 0],
                          nxt=(layer, 1) + f32_weights)
        if layer % 2 == 0:
            x = _even_mixer(x, mod, layer, ev_w_in[j].astype(BF16), _sublane_replicated(ev_conv_a_w[j]),
                            ev_conv_a_b[j], ev_norm_a_g[j], ev_norm_a_b[j], ev_conv_b_w[j],
                            ev_w_out[j].astype(BF16), ln_g[layer, 1], ln_b[layer, 1])
        else:
            x = _odd_mixer(x, mod, layer, od_w_in[j], od_cmp_pe_k[j], od_cmp_pe_v[j], od_cmp_w1_k[j],
                           od_cmp_w2_k[j], od_cmp_w1_v[j], od_cmp_w2_v[j], od_pool_w[j], od_pool_scale[j],
                           od_w_out[j], bias_tables, ln_g[layer, 1], ln_b[layer, 1])
        x, weights = _ffn(x, mod, layer, 2, weights, ln_g[layer, 2], ln_b[layer, 2],
                          nxt=(layer + 1, 0) + f32_weights if layer + 1 < depth else None)
    return x
```

```python
import functools
import math

import numpy as np
import jax
import jax.numpy as jnp
from jax import lax
from jax.experimental import pallas as pl
from jax.experimental.pallas import tpu as pltpu

F32 = jnp.float32
BF16 = jnp.bfloat16

DEPTH = 4
FFN_RES_WEIGHT = 0.5
CONV_A_CH = 512
CONV_A_WIDTH = 31
CONV_B_CH = 512
CONV_B_WIDTH = 3
NSA_HEADS = 16
HEAD_DIM = 64
CMP_BLOCK = 32
CMP_STRIDE = 16
CMP_HIDDEN = 128
SLC_BLOCK = 64
SLC_TOPK = 16
WINDOW = 512
POOL_WINDOWS = (2, 4, 8, 16)
POOL_GROUP_CH = 64
POOL_CH = len(POOL_WINDOWS) * POOL_GROUP_CH
REL_BUCKETS = 32
REL_MAX_DIST = 128
DN_ALPHA = (2 * DEPTH) ** 0.25
LN_EPS = 1e-5
NSA_Q = NSA_HEADS * HEAD_DIM
NEG_INF = -1e30
FORCE = 1e30

LANES = 128
SUBLANES = 8
VMEM_LIMIT_BYTES = 56 * 1024 * 1024

FFN_ROWS = 1024
FFN_SUB_ROWS = 256
MIX_ROWS = 512
EVEN_OUT_ROWS = 256
ODD_OUT_ROWS = 256
ATT_Q = 256
ATT_K = 256
CONV_HALO = 32
CONV_B_HALO = 8
POOL_HALO = 16
HEAD_LOOKAHEAD = 5


def _cparams(sem):
    return pltpu.CompilerParams(dimension_semantics=sem, vmem_limit_bytes=VMEM_LIMIT_BYTES)


def _const_spec(shape):
    n = len(shape)
    return pl.BlockSpec(shape, lambda *_: (0,) * n, pipeline_mode=pl.Buffered(1))


def _layer_norm(z, g, b):
    mu = jnp.mean(z, axis=-1, keepdims=True)
    zc = z - mu
    var = jnp.mean(zc * zc, axis=-1, keepdims=True)
    return zc * lax.rsqrt(var + LN_EPS) * g + b


def _silu(v):
    return v * jax.nn.sigmoid(v)


def _dot(a, b):
    return jnp.dot(a, b, preferred_element_type=F32)


def _dot_nt(a, b):
    return lax.dot_general(a, b, (((1,), (1,)), ((), ())), preferred_element_type=F32)


def _mod_kernel(c_ref, w_ref, b_ref, o_ref):
    cond = _silu(c_ref[...])
    o_ref[0] = _dot(cond.astype(BF16), w_ref[0].astype(BF16)) + b_ref[0]


def _modulation(c, ada_w, ada_b):
    depth, d, n = ada_w.shape
    bsz = c.shape[0]
    tn = 1152
    return pl.pallas_call(
        _mod_kernel,
        grid=(depth, n // tn),
        in_specs=[
            pl.BlockSpec((bsz, d), lambda l, j: (0, 0)),
            pl.BlockSpec((1, d, tn), lambda l, j: (l, 0, j)),
            pl.BlockSpec((1, 1, tn), lambda l, j: (l, 0, j)),
        ],
        out_specs=pl.BlockSpec((1, bsz, tn), lambda l, j: (l, 0, j)),
        out_shape=jax.ShapeDtypeStruct((depth, bsz, n), F32),
        compiler_params=_cparams(("arbitrary", "arbitrary")),
        name="modulation",
    )(c, ada_w, ada_b.reshape(depth, 1, n))


def _ffn_kernel(*refs, cast_next):
    if cast_next:
        (x_ref, mod_ref, wg_ref, wu_ref, wd_ref, g_ref, b_ref, ng_ref, nu_ref, nd_ref,
         o_ref, og_ref, ou_ref, od_ref) = refs
        og_ref[...] = ng_ref[0, 0].astype(BF16)
        ou_ref[...] = nu_ref[0, 0].astype(BF16)
        od_ref[...] = nd_ref[0, 0].astype(BF16)
    else:
        x_ref, mod_ref, wg_ref, wu_ref, wd_ref, g_ref, b_ref, o_ref = refs
    mod = mod_ref[0, 0, 0]
    tm = x_ref.shape[1]
    for r0 in range(0, tm, FFN_SUB_ROWS):
        x = x_ref[0, r0:r0 + FFN_SUB_ROWS, :]
        h = (x * (1.0 + mod[1:2]) + mod[0:1]).astype(BF16)
        a = (_silu(_dot(h, wg_ref[...])) * _dot(h, wu_ref[...])).astype(BF16)
        y = _dot(a, wd_ref[...])
        z = DN_ALPHA * x + (FFN_RES_WEIGHT * (1.0 + mod[2:3])) * y
        o_ref[0, r0:r0 + FFN_SUB_ROWS, :] = _layer_norm(z, g_ref[...], b_ref[...])


def _ffn(x, mod, layer, sub, weights, g, b, nxt=None):
    bsz, s, d = x.shape
    wg, wu, wd = weights
    ff = wg.shape[1]
    tm = min(FFN_ROWS, s)
    nt = s // tm
    steps = bsz * nt
    in_specs = [
        pl.BlockSpec((1, tm, d), lambda i, t: (i, t, 0)),
        pl.BlockSpec((1, 1, 1, 3, d), lambda i, t: (layer, i, sub, 0, 0)),
        _const_spec((d, ff)),
        _const_spec((d, ff)),
        _const_spec((ff, d)),
        _const_spec((1, d)),
        _const_spec((1, d)),
    ]
    out_specs = [pl.BlockSpec((1, tm, d), lambda i, t: (i, t, 0))]
    out_shape = [jax.ShapeDtypeStruct(x.shape, F32)]
    args = [x, mod, wg, wu, wd, g.reshape(1, d), b.reshape(1, d)]
    if nxt is not None:
        nl, nh, w_gate, w_up, w_down = nxt
        up_rows = d // steps
        down_steps = steps // 2
        down_rows = ff // down_steps
        assert d % steps == 0 and up_rows % 16 == 0 and ff % down_steps == 0 and down_rows % 16 == 0
        in_specs += [
            pl.BlockSpec((1, 1, up_rows, ff), lambda i, t: (nl, nh, i * nt + t, 0)),
            pl.BlockSpec((1, 1, up_rows, ff), lambda i, t: (nl, nh, i * nt + t, 0)),
            pl.BlockSpec((1, 1, down_rows, d), lambda i, t: (nl, nh, (i * nt + t) // 2, 0)),
        ]
        out_specs += [
            pl.BlockSpec((up_rows, ff), lambda i, t: (i * nt + t, 0)),
            pl.BlockSpec((up_rows, ff), lambda i, t: (i * nt + t, 0)),
            pl.BlockSpec((down_rows, d), lambda i, t: ((i * nt + t) // 2, 0)),
        ]
        out_shape += [jax.ShapeDtypeStruct((d, ff), BF16), jax.ShapeDtypeStruct((d, ff), BF16),
                      jax.ShapeDtypeStruct((ff, d), BF16)]
        args += [w_gate, w_up, w_down]
    out = pl.pallas_call(
        functools.partial(_ffn_kernel, cast_next=nxt is not None),
        grid=(bsz, nt),
        in_specs=in_specs,
        out_specs=out_specs,
        out_shape=out_shape,
        compiler_params=_cparams(("arbitrary", "arbitrary")),
        name="ffn",
    )(*args)
    return out[0], (tuple(out[1:]) if nxt is not None else None)


def _even_kernel(x_ref, mod_ref, win_ref, caw_ref, cab_ref, nag_ref, nab_ref, cbw_ref, wout_ref,
                 g_ref, b_ref, o_ref, abuf, ash, bbuf, cat):
    tm = x_ref.shape[1]
    ca, cb = CONV_A_CH, CONV_B_CH

    @pl.when(pl.program_id(1) == 0)
    def _():
        abuf[0:CONV_HALO, :] = jnp.zeros((CONV_HALO, ca), F32)
        bbuf[0:CONV_B_HALO, :] = jnp.zeros((CONV_B_HALO, cb), F32)

    x = x_ref[0]
    mod = mod_ref[0, 0, 0]
    h = (x * (1.0 + mod[1:2]) + mod[0:1]).astype(BF16)
    p = _dot(h, win_ref[...])
    abuf[CONV_HALO:CONV_HALO + tm, :] = p[:, 0:ca] * jax.nn.sigmoid(p[:, ca:2 * ca])
    gate_b = p[:, 2 * ca:2 * ca + cb]
    bbuf[CONV_B_HALO:CONV_B_HALO + tm, :] = p[:, 2 * ca + cb:2 * ca + 2 * cb] * p[:, 2 * ca + 2 * cb:]

    span = tm + CONV_HALO - SUBLANES
    for r in range(1, SUBLANES):
        ash[r - 1] = abuf[r:r + span, :]

    rows = 64
    for r0 in range(0, tm, rows):
        acc = jnp.zeros((rows // SUBLANES, SUBLANES, ca), F32)
        for k in range(CONV_A_WIDTH):
            off = CONV_HALO + r0 - (CONV_A_WIDTH - 1) + k
            r = off % SUBLANES
            tap = abuf[off:off + rows, :] if r == 0 else ash[r - 1, off - r:off - r + rows, :]
            acc = acc + caw_ref[k][None] * tap.reshape(rows // SUBLANES, SUBLANES, ca)
        acc = acc.reshape(rows, ca)
        u = _layer_norm(acc + cab_ref[...], nag_ref[...], nab_ref[...])
        cat[r0:r0 + rows, 0:ca] = _silu(u).astype(BF16)
        accb = jnp.zeros((rows, cb), F32)
        for k in range(CONV_B_WIDTH):
            off = CONV_B_HALO + r0 - (CONV_B_WIDTH - 1) + k
            accb = accb + cbw_ref[k:k + 1, :] * bbuf[off:off + rows, :]
        cat[r0:r0 + rows, ca:ca + cb] = (gate_b[r0:r0 + rows] * accb).astype(BF16)

        s1 = r0 + rows
        if s1 % EVEN_OUT_ROWS == 0:
            s0 = s1 - EVEN_OUT_ROWS
            y = _dot(cat[s0:s1, :], wout_ref[...])
            z = DN_ALPHA * x[s0:s1] + (1.0 + mod[2:3]) * y
            o_ref[0, s0:s1, :] = _layer_norm(z, g_ref[...], b_ref[...])

    abuf[0:CONV_HALO, :] = abuf[tm:tm + CONV_HALO, :]
    bbuf[0:CONV_B_HALO, :] = bbuf[tm:tm + CONV_B_HALO, :]


def _even_mixer(x, mod, layer, win, caw, cab, nag, nab, cbw, wout, g, b):
    bsz, s, d = x.shape
    tm = min(MIX_ROWS, s)
    ca, cb = CONV_A_CH, CONV_B_CH
    return pl.pallas_call(
        _even_kernel,
        grid=(bsz, s // tm),
        in_specs=[
            pl.BlockSpec((1, tm, d), lambda i, t: (i, t, 0)),
            pl.BlockSpec((1, 1, 1, 3, d), lambda i, t: (layer, i, 1, 0, 0)),
            _const_spec(win.shape),
            _const_spec(caw.shape),
            _const_spec((1, ca)),
            _const_spec((1, ca)),
            _const_spec((1, ca)),
            _const_spec(cbw.shape),
            _const_spec(wout.shape),
            _const_spec((1, d)),
            _const_spec((1, d)),
        ],
        out_specs=pl.BlockSpec((1, tm, d), lambda i, t: (i, t, 0)),
        out_shape=jax.ShapeDtypeStruct(x.shape, F32),
        scratch_shapes=[
            pltpu.VMEM((CONV_HALO + tm, ca), F32),
            pltpu.VMEM((SUBLANES - 1, CONV_HALO + tm - SUBLANES, ca), F32),
            pltpu.VMEM((CONV_B_HALO + tm, cb), F32),
            pltpu.VMEM((tm, ca + cb), BF16),
        ],
        compiler_params=_cparams(("arbitrary", "arbitrary")),
        name="even_mixer",
    )(x, mod, win, caw, cab.reshape(1, ca), nag.reshape(1, ca), nab.reshape(1, ca), cbw, wout,
      g.reshape(1, d), b.reshape(1, d))


def _sublane_replicated(w):
    return jnp.broadcast_to(w[:, None, :], (w.shape[0], SUBLANES, w.shape[1]))


KX_COLS = 256
KX_WIN = 128
MASK_ROWS = 16
QX_ROWS = HEAD_DIM + 2 * MASK_ROWS
VX_ROWS = HEAD_DIM + 16
LOG2E = math.log2(math.e)


def _odd_proj_kernel(x_ref, mod_ref, w_ref, wt_ref, kx_ref, kvc_ref, u_ref, qt_ref, vx_ref, gt_ref):
    tm = x_ref.shape[1]
    x = x_ref[0]
    mod = mod_ref[0, 0, 0]
    h = (x * (1.0 + mod[1:2]) + mod[0:1]).astype(BF16)
    p = _dot(h, w_ref[...])
    c1 = KX_COLS
    c2 = c1 + 2 * HEAD_DIM
    row = lax.broadcasted_iota(jnp.int32, (tm, KX_COLS), 0)
    col = lax.broadcasted_iota(jnp.int32, (tm, KX_COLS), 1)
    first = HEAD_DIM + MASK_ROWS * ((row // ATT_K) % 2)
    member = (col - first == (row % ATT_K) // SLC_BLOCK) & (col >= first)
    kx_ref[0] = jnp.where(member, 1.0, p[:, 0:c1]).astype(BF16)
    kvc_ref[0] = p[:, c1:c2]
    u_ref[0] = p[:, c2:]
    pt = _dot_nt(wt_ref[...], h)
    r1 = NSA_Q
    r2 = r1 + 2 * VX_ROWS
    qt_ref[0] = (pt[0:r1] * (HEAD_DIM ** -0.5 * LOG2E)).astype(BF16)
    vrow = lax.broadcasted_iota(jnp.int32, (2 * VX_ROWS, tm), 0)
    vx_ref[0] = jnp.where(vrow % VX_ROWS == HEAD_DIM, 1.0, pt[r1:r2]).astype(BF16)
    gt_ref[0] = jax.nn.sigmoid(pt[r2:])


def _odd_proj(x, mod, layer, w, wt):
    bsz, s, d = x.shape
    tm = min(MIX_ROWS, s)

    def blk(c):
        return pl.BlockSpec((1, tm, c), lambda i, t: (i, t, 0))

    def blk_t(r):
        return pl.BlockSpec((1, r, tm), lambda i, t: (i, 0, t))

    return pl.pallas_call(
        _odd_proj_kernel,
        grid=(bsz, s // tm),
        in_specs=[
            blk(d),
            pl.BlockSpec((1, 1, 1, 3, d), lambda i, t: (layer, i, 1, 0, 0)),
            _const_spec(w.shape),
            _const_spec(wt.shape),
        ],
        out_specs=[blk(KX_COLS), blk(2 * HEAD_DIM), blk(POOL_CH),
                   blk_t(NSA_Q), blk_t(2 * VX_ROWS), blk_t(3 * NSA_HEADS)],
        out_shape=[
            jax.ShapeDtypeStruct((bsz, s, KX_COLS), BF16),
            jax.ShapeDtypeStruct((bsz, s, 2 * HEAD_DIM), F32),
            jax.ShapeDtypeStruct((bsz, s, POOL_CH), F32),
            jax.ShapeDtypeStruct((bsz, NSA_Q, s), BF16),
            jax.ShapeDtypeStruct((bsz, 2 * VX_ROWS, s), BF16),
            jax.ShapeDtypeStruct((bsz, 3 * NSA_HEADS, s), F32),
        ],
        compiler_params=_cparams(("arbitrary", "arbitrary")),
        name="odd_proj",
    )(x, mod, w, wt)


def _compress_kernel(x_ref, pe_ref, wt_ref, wb_ref, w2k_ref, w2vt_ref, kc_ref, vct_ref):
    nrow = kc_ref.shape[1]
    width = x_ref.shape[2]
    top = jnp.zeros((nrow, wt_ref.shape[1]), F32)
    bot = jnp.zeros((nrow, wb_ref.shape[1]), F32)
    for j in range(CMP_STRIDE):
        xj = x_ref[0, pl.ds(j, nrow, stride=CMP_STRIDE), :]
        cols = slice(j * width, (j + 1) * width)
        top = top + _dot((xj + pe_ref[0:1, cols]).astype(BF16), wt_ref[cols, :])
        bot = bot + _dot((xj + pe_ref[1:2, cols]).astype(BF16), wb_ref[cols, :])
    hid = top + pltpu.roll(bot, bot.shape[0] - 1, axis=0)
    act = _silu(hid).astype(BF16)
    kc_ref[0] = _dot(act[:, 0:CMP_HIDDEN], w2k_ref[...]).astype(BF16)
    vct_ref[0] = _dot_nt(w2vt_ref[...], act[:, CMP_HIDDEN:]).astype(BF16)


def _compress(kvc, pe, wt, wb, w2k, w2vt):
    bsz, s, _ = kvc.shape
    nrow = s // CMP_STRIDE
    return pl.pallas_call(
        _compress_kernel,
        grid=(bsz,),
        in_specs=[
            pl.BlockSpec((1, s, 2 * HEAD_DIM), lambda i: (i, 0, 0)),
            _const_spec(pe.shape),
            _const_spec(wt.shape),
            _const_spec(wb.shape),
            _const_spec(w2k.shape),
            _const_spec(w2vt.shape),
        ],
        out_specs=[pl.BlockSpec((1, nrow, HEAD_DIM), lambda i: (i, 0, 0)),
                   pl.BlockSpec((1, HEAD_DIM, nrow), lambda i: (i, 0, 0))],
        out_shape=[jax.ShapeDtypeStruct((bsz, nrow, HEAD_DIM), BF16),
                   jax.ShapeDtypeStruct((bsz, HEAD_DIM, nrow), BF16)],
        compiler_params=_cparams(("arbitrary",)),
        name="compress",
    )(kvc, pe, wt, wb, w2k, w2vt)


CMP_TAB_SHIFT = LANES - ATT_Q // CMP_STRIDE
CMP_TAB_ROWS = CMP_TAB_SHIFT + LANES


def _t5_bucket_table(max_dist):
    n = np.arange(max_dist, dtype=np.int64)
    exact = REL_BUCKETS // 2
    nf = np.maximum(n, 1).astype(np.float32)
    large = exact + (np.log(nf / np.float32(exact)) / np.float32(math.log(REL_MAX_DIST / exact))
                     * np.float32(REL_BUCKETS - exact)).astype(np.int32)
    return np.where(n < exact, n, np.minimum(large, REL_BUCKETS - 1)).astype(np.int32)


def _bias_tables(rel_bias):
    far = REL_BUCKETS - 1
    table = _t5_bucket_table(1024)

    def buckets(dist):
        return np.where(dist >= 0, table[np.clip(dist, 0, 1023)], REL_BUCKETS).reshape(-1)

    r = np.arange(ATT_K // 2)[:, None]
    c = np.arange(ATT_Q)[None, :]
    jj = np.arange(CMP_TAB_ROWS)[:, None]
    bucket = np.concatenate([buckets(c - r), buckets(c - CMP_STRIDE * (jj - CMP_TAB_SHIFT) - (CMP_BLOCK - 1))])
    onehot = (jnp.asarray(bucket)[None, :] == jnp.arange(REL_BUCKETS + 1)[:, None]).astype(F32)
    rb = (rel_bias.astype(F32) - rel_bias[far:far + 1].astype(F32)).T * LOG2E
    rb = jnp.concatenate([rb, jnp.full((rb.shape[0], 1), NEG_INF, F32)], axis=1)
    flat = jnp.dot(rb, onehot, precision=lax.Precision.HIGHEST)
    nh = rel_bias.shape[1]
    n_tile = (ATT_K // 2) * ATT_Q
    tile = flat[:, :n_tile].reshape(nh, ATT_K // 2, ATT_Q)
    cmp_tab = flat[:, n_tile:].reshape(nh, CMP_TAB_ROWS, ATT_Q)
    return tile, cmp_tab


def _attn_kernel(qt_ref, gt_ref, kc_ref, vct_ref, kx_ref, vx_ref, tile_ref, cmpb_ref, ovt_ref, pm_ref,
                 o_ref, qx, selb, old_mask, acc_s, m_sel, acc_w, m_win, ot):
    nh, hd = NSA_HEADS, HEAD_DIM
    tq, tk = ATT_Q, ATT_K
    i = pl.program_id(1)
    t0 = i * tq
    ncmp = kc_ref.shape[1]

    @pl.when(i == 0)
    def _():
        r = lax.broadcasted_iota(jnp.int32, (tk // 2, tq // 2), 0)
        c = lax.broadcasted_iota(jnp.int32, (tk // 2, tq // 2), 1)
        old_mask[...] = jnp.where(r > c, 0.0, NEG_INF)
        for h in range(nh):
            qx[h, hd:QX_ROWS, :] = jnp.zeros((QX_ROWS - hd, tq), BF16)

    for h in range(nh):
        qx[h, 0:hd, :] = qt_ref[0, h * hd:(h + 1) * hd, :]

    def gate(branch, h):
        r = branch * nh + h
        return gt_ref[0, r:r + 1, :]

    def run_pipelined(steps):
        work, before = [], {}
        for st in steps:
            if callable(st):
                before.setdefault(len(work), []).append(st)
            else:
                work.append(st)
        pending = {}

        def issue(j):
            for f in before.get(j, ()):
                f()
            pending[j] = work[j][0]()

        for j in range(min(HEAD_LOOKAHEAD, len(work))):
            issue(j)
        for j in range(len(work)):
            if j + HEAD_LOOKAHEAD < len(work):
                issue(j + HEAD_LOOKAHEAD)
            work[j][1](pending.pop(j))

    kc = kc_ref[0]
    vct = vct_ref[0]
    start = pl.multiple_of(CMP_TAB_SHIFT - (tq // CMP_STRIDE) * i, 16)
    tok = t0 + lax.broadcasted_iota(jnp.int32, (1, tq), 1)
    any_valid = jnp.where(tok >= CMP_BLOCK - 1, 1.0, 0.0)
    psum = []

    def cmp_logits(h):
        return _dot(kc, qx[h, 0:hd, :]) + cmpb_ref[h, pl.ds(start, ncmp), :]

    def cmp_softmax(h, s):
        e = jnp.exp2(s - jnp.max(s, axis=0, keepdims=True))
        prob = e * (any_valid / jnp.sum(e, axis=0, keepdims=True))
        psum[-1] = psum[-1] + prob
        ot[h] = gate(0, h) * _dot(vct, prob.astype(BF16))

    def cmp_steps():
        psum.append(jnp.zeros((ncmp, tq), F32))
        return [(functools.partial(cmp_logits, h), functools.partial(cmp_softmax, h)) for h in range(nh)]

    def select_blocks():
        p_hi = psum[-1].astype(BF16)
        p_lo = (psum[-1] - p_hi.astype(F32)).astype(BF16)
        ovt = ovt_ref[...]
        imp = _dot(ovt, p_hi) + _dot(ovt, p_lo)
        nslc = ovt.shape[0]
        blk = lax.broadcasted_iota(jnp.int32, (nslc, tq), 0)
        cur = (t0 + lax.broadcasted_iota(jnp.int32, (nslc, tq), 1)) // SLC_BLOCK
        forced = (blk == 0) | (blk == cur) | (blk == cur - 1)
        val = jnp.where(forced, FORCE, jnp.where(blk <= cur, imp, -FORCE))
        rank = jnp.zeros((nslc, tq), F32)
        for k in range(nslc):
            vk = val[k:k + 1, :]
            rank = rank + jnp.where(blk > k, jnp.where(vk >= val, 1.0, 0.0), jnp.where(vk > val, 1.0, 0.0))
        selbias = jnp.where(rank < float(min(SLC_TOPK, nslc)), 0.0, NEG_INF).astype(BF16)
        selb[...] = _dot(pm_ref[...], selbias).reshape(selb.shape).astype(BF16)

    def set_sel_rows(c):
        tile = selb[c]
        off = hd + MASK_ROWS * (c % 2)
        if not isinstance(off, int):
            off = pl.multiple_of(off, MASK_ROWS)
        for h in range(nh):
            qx[h, pl.ds(off, MASK_ROWS), :] = tile

    def chunk_steps(row0, selected, mode, first):
        row0 = pl.multiple_of(row0, tk)
        half = tk // 2
        assert first == (mode == "cur")
        acc, m_run = (acc_s, m_sel) if selected else (acc_w, m_win)

        def logits(h):
            kcols = slice(0, QX_ROWS) if selected else slice(KX_WIN, KX_WIN + hd)
            q = qx[h] if selected else qx[h, 0:hd, :]
            if mode == "cur":
                t = tile_ref[h]
                s_top = _dot(kx_ref[0, pl.ds(row0, half), kcols], q) + t
                s_br = _dot(kx_ref[0, pl.ds(row0 + half, half), kcols], q[:, half:]) + t[:, 0:half]
                return s_top, s_br
            if mode == "old":
                tri = old_mask[...]
                s_tl = _dot(kx_ref[0, pl.ds(row0, half), kcols], q[:, 0:half]) + tri
                s_bot = _dot(kx_ref[0, pl.ds(row0 + half, half), kcols], q)
                s_bot = jnp.concatenate([s_bot[:, 0:half], s_bot[:, half:] + tri], axis=1)
                return s_tl, s_bot
            s = _dot(kx_ref[0, pl.ds(row0, tk), kcols], q)
            if mode == "prev":
                t = tile_ref[h]
                bot = jnp.concatenate([s[half:, 0:half] + t[:, half:], s[half:, half:]], axis=1)
                s = jnp.concatenate([s[0:half], bot], axis=0)
            return s

        def update(h, s):
            if selected:
                vx = vx_ref[0, 0:VX_ROWS, pl.ds(row0, tk)]
            else:
                vx = vx_ref[0, VX_ROWS:2 * VX_ROWS, pl.ds(row0, tk)]
            if mode == "cur":
                s_top, s_br = s
                m_left = jnp.max(s_top[:, 0:half], axis=0, keepdims=True)
                m_right = jnp.maximum(jnp.max(s_top[:, half:], axis=0, keepdims=True),
                                      jnp.max(s_br, axis=0, keepdims=True))
                m_new = jnp.concatenate([m_left, m_right], axis=1)
                p_bot = jnp.concatenate([jnp.zeros((half, half), F32), jnp.exp2(s_br - m_right)], axis=1)
                p = jnp.concatenate([jnp.exp2(s_top - m_new), p_bot], axis=0)
                acc[h] = _dot(vx, p.astype(BF16))
            elif mode == "old":
                s_tl, s_bot = s
                m_old = m_run[h]
                m_left = jnp.maximum(jnp.max(s_tl, axis=0, keepdims=True),
                                     jnp.max(s_bot[:, 0:half], axis=0, keepdims=True))
                m_chunk = jnp.concatenate([m_left, jnp.max(s_bot[:, half:], axis=0, keepdims=True)], axis=1)
                m_new = jnp.maximum(m_old, m_chunk)
                p_top = jnp.concatenate([jnp.exp2(s_tl - m_new[:, 0:half]), jnp.zeros((half, half), F32)], axis=1)
                p = jnp.concatenate([p_top, jnp.exp2(s_bot - m_new)], axis=0)
                acc[h] = jnp.exp2(m_old - m_new) * acc[h] + _dot(vx, p.astype(BF16))
            else:
                m_old = m_run[h]
                m_new = jnp.maximum(m_old, jnp.max(s, axis=0, keepdims=True))
                pv = _dot(vx, jnp.exp2(s - m_new).astype(BF16))
                acc[h] = jnp.exp2(m_old - m_new) * acc[h] + pv
            m_run[h] = m_new

        return [(functools.partial(logits, h), functools.partial(update, h)) for h in range(nh)]

    def main_steps():
        return (cmp_steps()
                + chunk_steps(t0, False, "cur", True)
                + [select_blocks, functools.partial(set_sel_rows, i)]
                + chunk_steps(t0, True, "cur", True))

    def prev_steps():
        return ([functools.partial(set_sel_rows, i - 1)]
                + chunk_steps(t0 - tk, True, "prev", False)
                + chunk_steps(t0 - tk, False, "prev", False))

    @pl.when(i == 0)
    def _():
        run_pipelined(main_steps())

    @pl.when(i == 1)
    def _():
        run_pipelined(main_steps() + prev_steps())

    @pl.when(i >= 2)
    def _():
        run_pipelined(main_steps() + prev_steps() + chunk_steps(t0 - 2 * tk, False, "old", False))

    def far_steps(c):
        return chunk_steps(c * tk, True, "far", False)

    n_far = jnp.maximum(i - 1, 0)

    def far_pair(j, carry):
        set_sel_rows(2 * j)
        set_sel_rows(2 * j + 1)
        run_pipelined(far_steps(2 * j) + far_steps(2 * j + 1))
        return carry

    lax.fori_loop(0, n_far // 2, far_pair, 0)

    @pl.when(n_far % 2 == 1)
    def _():
        set_sel_rows(n_far - 1)
        run_pipelined(far_steps(n_far - 1))

    for h in range(nh):
        a_s = acc_s[h]
        a_w = acc_w[h]
        ot[h] = (ot[h] + (gate(1, h) / a_s[hd:hd + 1]) * a_s[0:hd]
                 + (gate(2, h) / a_w[hd:hd + 1]) * a_w[0:hd])
    o_ref[0] = ot[...].reshape(nh * hd, tq).T.astype(BF16)


def _attention(qt, gt, kc, vct, kx, vx, tile, cmp_tab, ovt, pm):
    bsz, _, s = qt.shape
    nh, hd, tq, tk = NSA_HEADS, HEAD_DIM, ATT_Q, ATT_K
    ncmp = kc.shape[1]
    nchunk = s // tk
    return pl.pallas_call(
        _attn_kernel,
        grid=(bsz, s // tq),
        in_specs=[
            pl.BlockSpec((1, NSA_Q, tq), lambda b, i: (b, 0, i)),
            pl.BlockSpec((1, 3 * nh, tq), lambda b, i: (b, 0, i)),
            pl.BlockSpec((1, ncmp, hd), lambda b, i: (b, 0, 0)),
            pl.BlockSpec((1, hd, ncmp), lambda b, i: (b, 0, 0)),
            pl.BlockSpec((1, kx.shape[1], KX_COLS), lambda b, i: (b, 0, 0)),
            pl.BlockSpec((1, 2 * VX_ROWS, vx.shape[2]), lambda b, i: (b, 0, 0)),
            _const_spec(tile.shape),
            _const_spec(cmp_tab.shape),
            _const_spec(ovt.shape),
            _const_spec(pm.shape),
        ],
        out_specs=pl.BlockSpec((1, tq, NSA_Q), lambda b, i: (b, i, 0)),
        out_shape=jax.ShapeDtypeStruct((bsz, s, NSA_Q), BF16),
        scratch_shapes=[
            pltpu.VMEM((nh, QX_ROWS, tq), BF16),
            pltpu.VMEM((nchunk, MASK_ROWS, tq), BF16),
            pltpu.VMEM((tk // 2, tq // 2), F32),
            pltpu.VMEM((nh, VX_ROWS, tq), F32),
            pltpu.VMEM((nh, 1, tq), F32),
            pltpu.VMEM((nh, VX_ROWS, tq), F32),
            pltpu.VMEM((nh, 1, tq), F32),
            pltpu.VMEM((nh, hd, tq), F32),
        ],
        compiler_params=_cparams(("arbitrary", "arbitrary")),
        name="nsa_attention",
    )(qt, gt, kc, vct, kx, vx, tile, cmp_tab, ovt, pm)


def _odd_out_kernel(x_ref, mod_ref, o_ref_in, u_ref, pw_ref, ps_ref, wo_ref, wp_ref, g_ref, b_ref, o_ref, ubuf):
    tm = x_ref.shape[1]
    t = pl.program_id(1)

    @pl.when(t == 0)
    def _():
        ubuf[0:POOL_HALO, :] = jnp.zeros((POOL_HALO, POOL_CH), F32)

    u = u_ref[0]
    ubuf[POOL_HALO:POOL_HALO + tm, :] = u

    lane_group = lax.broadcasted_iota(jnp.int32, (tm, POOL_CH), 1) // POOL_GROUP_CH
    pos = t * tm + lax.broadcasted_iota(jnp.int32, (tm, POOL_CH), 0)
    total = jnp.zeros((tm, POOL_CH), F32)
    width = jnp.ones((tm, POOL_CH), F32)
    sw = ubuf[...]
    span = 1
    for gi, w in enumerate(POOL_WINDOWS):
        while span < w:
            sw = sw + pltpu.roll(sw, span, axis=0)
            span *= 2
        total = jnp.where(lane_group == gi, sw[POOL_HALO:], total)
        width = jnp.where(lane_group == gi, float(w), width)
    cnt = jnp.minimum((pos + 1).astype(F32), width)
    dlt = (total / cnt - u).astype(BF16)
    ubuf[0:POOL_HALO, :] = ubuf[tm:tm + POOL_HALO, :]

    mod = mod_ref[0, 0, 0]
    for r0 in range(0, tm, ODD_OUT_ROWS):
        r1 = r0 + ODD_OUT_ROWS
        o_pool = (_dot(dlt[r0:r1], pw_ref[...]) * ps_ref[...]).astype(BF16)
        y = _dot(o_ref_in[0, r0:r1, :], wo_ref[...]) + _dot(o_pool, wp_ref[...])
        z = DN_ALPHA * x_ref[0, r0:r1, :] + (1.0 + mod[2:3]) * y
        o_ref[0, r0:r1, :] = _layer_norm(z, g_ref[...], b_ref[...])


def _odd_out(x, mod, layer, o_nsa, u, pw, ps, wo, wp, g, b):
    bsz, s, d = x.shape
    tm = min(MIX_ROWS, s)
    return pl.pallas_call(
        _odd_out_kernel,
        grid=(bsz, s // tm),
        in_specs=[
            pl.BlockSpec((1, tm, d), lambda i, t: (i, t, 0)),
            pl.BlockSpec((1, 1, 1, 3, d), lambda i, t: (layer, i, 1, 0, 0)),
            pl.BlockSpec((1, tm, NSA_Q), lambda i, t: (i, t, 0)),
            pl.BlockSpec((1, tm, POOL_CH), lambda i, t: (i, t, 0)),
            _const_spec(pw.shape),
            _const_spec((1, POOL_CH)),
            _const_spec(wo.shape),
            _const_spec(wp.shape),
            _const_spec((1, d)),
            _const_spec((1, d)),
        ],
        out_specs=pl.BlockSpec((1, tm, d), lambda i, t: (i, t, 0)),
        out_shape=jax.ShapeDtypeStruct(x.shape, F32),
        scratch_shapes=[pltpu.VMEM((POOL_HALO + tm, POOL_CH), F32)],
        compiler_params=_cparams(("arbitrary", "arbitrary")),
        name="odd_out",
    )(x, mod, o_nsa, u, pw, ps.reshape(1, POOL_CH), wo, wp, g.reshape(1, d), b.reshape(1, d))


def _odd_in_weights(w_in):
    hd, nh = HEAD_DIM, NSA_HEADS
    c = NSA_Q
    col = lambda k: w_in[:, c + k * hd:c + (k + 1) * hd]
    kc, vc, ks, vs, kw, vw = (col(k) for k in range(6))
    gates = w_in[:, c + 6 * hd:c + 6 * hd + 3 * nh]
    u = w_in[:, c + 6 * hd + 3 * nh:]
    perm = np.array([3 * h + j for j in range(3) for h in range(nh)])
    zc = lambda n: jnp.zeros((w_in.shape[0], n), w_in.dtype)
    w_std = jnp.concatenate([ks, zc(KX_WIN - hd), kw, zc(KX_COLS - KX_WIN - hd), kc, vc, u], axis=1)
    w_t = jnp.concatenate([w_in[:, 0:c], vs, zc(VX_ROWS - hd), vw, zc(VX_ROWS - hd), gates[:, perm]], axis=1)
    return w_std.astype(BF16), w_t.T.astype(BF16)


def _compress_weights(pe_k, pe_v, w1_k, w1_v, w2_k, w2_v):
    hd, hid, half = HEAD_DIM, CMP_HIDDEN, CMP_STRIDE

    def halves(w1):
        w = w1.reshape(CMP_BLOCK, hd, hid)
        return w[:half], w[half:]

    kt, kb = halves(w1_k)
    vt, vb = halves(w1_v)
    z = jnp.zeros((half, hd, hid), F32)

    def assemble(k_part, v_part):
        k_rows = jnp.concatenate([k_part, z], axis=2)
        v_rows = jnp.concatenate([z, v_part], axis=2)
        return jnp.concatenate([k_rows, v_rows], axis=1).reshape(half * 2 * hd, 2 * hid).astype(BF16)

    wt = assemble(kt, vt)
    wb = assemble(kb, vb)
    pe = jnp.concatenate([pe_k.reshape(2, half, hd), pe_v.reshape(2, half, hd)], axis=2)
    pe = pe.reshape(2, half * 2 * hd)
    return pe, wt, wb, w2_k.astype(BF16), w2_v.T.astype(BF16)


def _overlap_matrix_t(n_cols, n_slc):
    n_cmp = n_slc * SLC_BLOCK // CMP_STRIDE - 1
    cmp_start = np.arange(n_cmp) * CMP_STRIDE
    slc_start = np.arange(n_slc) * SLC_BLOCK
    ov = np.clip(np.minimum(cmp_start[:, None] + CMP_BLOCK, slc_start[None, :] + SLC_BLOCK)
                 - np.maximum(cmp_start[:, None], slc_start[None, :]), 0, None) / CMP_BLOCK
    out = np.zeros((n_slc, n_cols), np.float32)
    out[:, :n_cmp] = ov.T
    return jnp.asarray(out, BF16)


def _chunk_row_placement(n_chunk, n_slc):
    per = ATT_K // SLC_BLOCK
    pm = np.zeros((n_chunk * MASK_ROWS, n_slc), np.float32)
    for c in range(n_chunk):
        for b in range(per):
            pm[MASK_ROWS * c + b, per * c + b] = 1.0
    return jnp.asarray(pm, BF16)


def _pool_weight(pool_w):
    ng, gc = len(POOL_WINDOWS), POOL_GROUP_CH
    w = jnp.zeros((ng * gc, ng * gc), F32)
    for gi in range(ng):
        w = w.at[gi * gc:(gi + 1) * gc, gi * gc:(gi + 1) * gc].set(pool_w[gi])
    return w.astype(BF16)


def _odd_mixer(x, mod, layer, w_in, pe_k, pe_v, w1_k, w2_k, w1_v, w2_v, pool_w, pool_scale, w_out,
               bias_tables, g, b):
    bsz, s, _ = x.shape
    assert s % ATT_Q == 0 and s // CMP_STRIDE == LANES and ATT_Q == ATT_K and MIX_ROWS % ATT_K == 0
    w_std, w_t = _odd_in_weights(w_in)
    kx, kvc, u, qt, vx, gt = _odd_proj(x, mod, layer, w_std, w_t)
    kc, vct = _compress(kvc, *_compress_weights(pe_k, pe_v, w1_k, w1_v, w2_k, w2_v))
    tile, cmp_tab = bias_tables
    ovt = _overlap_matrix_t(s // CMP_STRIDE, s // SLC_BLOCK)
    pm = _chunk_row_placement(s // ATT_K, s // SLC_BLOCK)
    o_nsa = _attention(qt, gt, kc, vct, kx, vx, tile, cmp_tab, ovt, pm)
    return _odd_out(x, mod, layer, o_nsa, u, _pool_weight(pool_w), pool_scale,
                    w_out[:NSA_Q].astype(BF16), w_out[NSA_Q:].astype(BF16), g, b)


def kernel(x, c, ada_w, ada_b, ln_g, ln_b, ffn_w_gate, ffn_w_up, ffn_w_down, ev_w_in, ev_conv_a_w,
           ev_conv_a_b, ev_norm_a_g, ev_norm_a_b, ev_conv_b_w, ev_w_out, od_w_in, od_cmp_pe_k,
           od_cmp_pe_v, od_cmp_w1_k, od_cmp_w2_k, od_cmp_w1_v, od_cmp_w2_v, od_pool_w, od_pool_scale,
           od_w_out, rel_bias):
    bsz, s, d = x.shape
    depth = ada_w.shape[0]
    mod = _modulation(c, ada_w, ada_b).reshape(depth, bsz, 3, 3, d)
    bias_tables = _bias_tables(rel_bias)
    weights = tuple(w[0, 0].astype(BF16) for w in (ffn_w_gate, ffn_w_up, ffn_w_down))
    f32_weights = (ffn_w_gate, ffn_w_up, ffn_w_down)
    for layer in range(depth):
        j = layer // 2
        x, weights = _ffn(x, mod, layer, 0, weights, ln_g[layer, 0], ln_b[layer, 0],
                          nxt=(layer, 1) + f32_weights)
        if layer % 2 == 0:
            x = _even_mixer(x, mod, layer, ev_w_in[j].astype(BF16), _sublane_replicated(ev_conv_a_w[j]),
                            ev_conv_a_b[j], ev_norm_a_g[j], ev_norm_a_b[j], ev_conv_b_w[j],
                            ev_w_out[j].astype(BF16), ln_g[layer, 1], ln_b[layer, 1])
        else:
            x = _odd_mixer(x, mod, layer, od_w_in[j], od_cmp_pe_k[j], od_cmp_pe_v[j], od_cmp_w1_k[j],
                           od_cmp_w2_k[j], od_cmp_w1_v[j], od_cmp_w2_v[j], od_pool_w[j], od_pool_scale[j],
                           od_w_out[j], bias_tables, ln_g[layer, 1], ln_b[layer, 1])
        x, weights = _ffn(x, mod, layer, 2, weights, ln_g[layer, 2], ln_b[layer, 2],
                          nxt=(layer + 1, 0) + f32_weights if layer + 1 < depth else None)
    return x
```

```python
import functools
import math

import numpy as np
import jax
import jax.numpy as jnp
from jax import lax
from jax.experimental import pallas as pl
from jax.experimental.pallas import tpu as pltpu

F32 = jnp.float32
BF16 = jnp.bfloat16

DEPTH = 4
FFN_RES_WEIGHT = 0.5
CONV_A_CH = 512
CONV_A_WIDTH = 31
CONV_B_CH = 512
CONV_B_WIDTH = 3
NSA_HEADS = 16
HEAD_DIM = 64
CMP_BLOCK = 32
CMP_STRIDE = 16
CMP_HIDDEN = 128
SLC_BLOCK = 64
SLC_TOPK = 16
WINDOW = 512
POOL_WINDOWS = (2, 4, 8, 16)
POOL_GROUP_CH = 64
POOL_CH = len(POOL_WINDOWS) * POOL_GROUP_CH
REL_BUCKETS = 32
REL_MAX_DIST = 128
DN_ALPHA = (2 * DEPTH) ** 0.25
LN_EPS = 1e-5
NSA_Q = NSA_HEADS * HEAD_DIM
NEG_INF = -1e30
FORCE = 1e30

LANES = 128
SUBLANES = 8
VMEM_LIMIT_BYTES = 56 * 1024 * 1024

FFN_ROWS = 1024
FFN_SUB_ROWS = 256
MIX_ROWS = 1024
EVEN_OUT_ROWS = 256
ODD_OUT_ROWS = 256
ATT_Q = 256
ATT_K = 256
CONV_HALO = 32
CONV_B_HALO = 8
POOL_HALO = 16
HEAD_LOOKAHEAD = 5


def _cparams(sem):
    return pltpu.CompilerParams(dimension_semantics=sem, vmem_limit_bytes=VMEM_LIMIT_BYTES)


def _const_spec(shape):
    n = len(shape)
    return pl.BlockSpec(shape, lambda *_: (0,) * n, pipeline_mode=pl.Buffered(1))


def _layer_norm(z, g, b):
    mu = jnp.mean(z, axis=-1, keepdims=True)
    zc = z - mu
    var = jnp.mean(zc * zc, axis=-1, keepdims=True)
    return zc * lax.rsqrt(var + LN_EPS) * g + b


def _silu(v):
    return v * jax.nn.sigmoid(v)


def _dot(a, b):
    return jnp.dot(a, b, preferred_element_type=F32)


def _dot_nt(a, b):
    return lax.dot_general(a, b, (((1,), (1,)), ((), ())), preferred_element_type=F32)


def _mod_kernel(c_ref, w_ref, b_ref, o_ref):
    cond = _silu(c_ref[...])
    o_ref[0] = _dot(cond.astype(BF16), w_ref[0].astype(BF16)) + b_ref[0]


def _modulation(c, ada_w, ada_b):
    depth, d, n = ada_w.shape
    bsz = c.shape[0]
    tn = 1152
    return pl.pallas_call(
        _mod_kernel,
        grid=(depth, n // tn),
        in_specs=[
            pl.BlockSpec((bsz, d), lambda l, j: (0, 0)),
            pl.BlockSpec((1, d, tn), lambda l, j: (l, 0, j)),
            pl.BlockSpec((1, 1, tn), lambda l, j: (l, 0, j)),
        ],
        out_specs=pl.BlockSpec((1, bsz, tn), lambda l, j: (l, 0, j)),
        out_shape=jax.ShapeDtypeStruct((depth, bsz, n), F32),
        compiler_params=_cparams(("arbitrary", "arbitrary")),
        name="modulation",
    )(c, ada_w, ada_b.reshape(depth, 1, n))


def _ffn_kernel(*refs, cast_next):
    if cast_next:
        (x_ref, mod_ref, wg_ref, wu_ref, wd_ref, g_ref, b_ref, ng_ref, nu_ref, nd_ref,
         o_ref, og_ref, ou_ref, od_ref) = refs
        og_ref[...] = ng_ref[0, 0].astype(BF16)
        ou_ref[...] = nu_ref[0, 0].astype(BF16)
        od_ref[...] = nd_ref[0, 0].astype(BF16)
    else:
        x_ref, mod_ref, wg_ref, wu_ref, wd_ref, g_ref, b_ref, o_ref = refs
    mod = mod_ref[0, 0, 0]
    tm = x_ref.shape[1]
    for r0 in range(0, tm, FFN_SUB_ROWS):
        x = x_ref[0, r0:r0 + FFN_SUB_ROWS, :]
        h = (x * (1.0 + mod[1:2]) + mod[0:1]).astype(BF16)
        a = (_silu(_dot(h, wg_ref[...])) * _dot(h, wu_ref[...])).astype(BF16)
        y = _dot(a, wd_ref[...])
        z = DN_ALPHA * x + (FFN_RES_WEIGHT * (1.0 + mod[2:3])) * y
        o_ref[0, r0:r0 + FFN_SUB_ROWS, :] = _layer_norm(z, g_ref[...], b_ref[...])


def _ffn(x, mod, layer, sub, weights, g, b, nxt=None):
    bsz, s, d = x.shape
    wg, wu, wd = weights
    ff = wg.shape[1]
    tm = min(FFN_ROWS, s)
    nt = s // tm
    steps = bsz * nt
    in_specs = [
        pl.BlockSpec((1, tm, d), lambda i, t: (i, t, 0)),
        pl.BlockSpec((1, 1, 1, 3, d), lambda i, t: (layer, i, sub, 0, 0)),
        _const_spec((d, ff)),
        _const_spec((d, ff)),
        _const_spec((ff, d)),
        _const_spec((1, d)),
        _const_spec((1, d)),
    ]
    out_specs = [pl.BlockSpec((1, tm, d), lambda i, t: (i, t, 0))]
    out_shape = [jax.ShapeDtypeStruct(x.shape, F32)]
    args = [x, mod, wg, wu, wd, g.reshape(1, d), b.reshape(1, d)]
    if nxt is not None:
        nl, nh, w_gate, w_up, w_down = nxt
        up_rows = d // steps
        down_steps = steps // 2
        down_rows = ff // down_steps
        assert d % steps == 0 and up_rows % 16 == 0 and ff % down_steps == 0 and down_rows % 16 == 0
        in_specs += [
            pl.BlockSpec((1, 1, up_rows, ff), lambda i, t: (nl, nh, i * nt + t, 0)),
            pl.BlockSpec((1, 1, up_rows, ff), lambda i, t: (nl, nh, i * nt + t, 0)),
            pl.BlockSpec((1, 1, down_rows, d), lambda i, t: (nl, nh, (i * nt + t) // 2, 0)),
        ]
        out_specs += [
            pl.BlockSpec((up_rows, ff), lambda i, t: (i * nt + t, 0)),
            pl.BlockSpec((up_rows, ff), lambda i, t: (i * nt + t, 0)),
            pl.BlockSpec((down_rows, d), lambda i, t: ((i * nt + t) // 2, 0)),
        ]
        out_shape += [jax.ShapeDtypeStruct((d, ff), BF16), jax.ShapeDtypeStruct((d, ff), BF16),
                      jax.ShapeDtypeStruct((ff, d), BF16)]
        args += [w_gate, w_up, w_down]
    out = pl.pallas_call(
        functools.partial(_ffn_kernel, cast_next=nxt is not None),
        grid=(bsz, nt),
        in_specs=in_specs,
        out_specs=out_specs,
        out_shape=out_shape,
        compiler_params=_cparams(("arbitrary", "arbitrary")),
        name="ffn",
    )(*args)
    return out[0], (tuple(out[1:]) if nxt is not None else None)


def _even_kernel(x_ref, mod_ref, win_ref, caw_ref, cab_ref, nag_ref, nab_ref, cbw_ref, wout_ref,
                 g_ref, b_ref, o_ref, abuf, ash, bbuf, cat):
    tm = x_ref.shape[1]
    ca, cb = CONV_A_CH, CONV_B_CH

    @pl.when(pl.program_id(1) == 0)
    def _():
        abuf[0:CONV_HALO, :] = jnp.zeros((CONV_HALO, ca), F32)
        bbuf[0:CONV_B_HALO, :] = jnp.zeros((CONV_B_HALO, cb), F32)

    x = x_ref[0]
    mod = mod_ref[0, 0, 0]
    h = (x * (1.0 + mod[1:2]) + mod[0:1]).astype(BF16)
    p = _dot(h, win_ref[...])
    abuf[CONV_HALO:CONV_HALO + tm, :] = p[:, 0:ca] * jax.nn.sigmoid(p[:, ca:2 * ca])
    gate_b = p[:, 2 * ca:2 * ca + cb]
    bbuf[CONV_B_HALO:CONV_B_HALO + tm, :] = p[:, 2 * ca + cb:2 * ca + 2 * cb] * p[:, 2 * ca + 2 * cb:]

    span = tm + CONV_HALO - SUBLANES
    for r in range(1, SUBLANES):
        ash[r - 1] = abuf[r:r + span, :]

    rows = 64
    for r0 in range(0, tm, rows):
        acc = jnp.zeros((rows // SUBLANES, SUBLANES, ca), F32)
        for k in range(CONV_A_WIDTH):
            off = CONV_HALO + r0 - (CONV_A_WIDTH - 1) + k
            r = off % SUBLANES
            tap = abuf[off:off + rows, :] if r == 0 else ash[r - 1, off - r:off - r + rows, :]
            acc = acc + caw_ref[k][None] * tap.reshape(rows // SUBLANES, SUBLANES, ca)
        acc = acc.reshape(rows, ca)
        u = _layer_norm(acc + cab_ref[...], nag_ref[...], nab_ref[...])
        cat[r0:r0 + rows, 0:ca] = _silu(u).astype(BF16)
        accb = jnp.zeros((rows, cb), F32)
        for k in range(CONV_B_WIDTH):
            off = CONV_B_HALO + r0 - (CONV_B_WIDTH - 1) + k
            accb = accb + cbw_ref[k:k + 1, :] * bbuf[off:off + rows, :]
        cat[r0:r0 + rows, ca:ca + cb] = (gate_b[r0:r0 + rows] * accb).astype(BF16)

        s1 = r0 + rows
        if s1 % EVEN_OUT_ROWS == 0:
            s0 = s1 - EVEN_OUT_ROWS
            y = _dot(cat[s0:s1, :], wout_ref[...])
            z = DN_ALPHA * x[s0:s1] + (1.0 + mod[2:3]) * y
            o_ref[0, s0:s1, :] = _layer_norm(z, g_ref[...], b_ref[...])

    abuf[0:CONV_HALO, :] = abuf[tm:tm + CONV_HALO, :]
    bbuf[0:CONV_B_HALO, :] = bbuf[tm:tm + CONV_B_HALO, :]


def _even_mixer(x, mod, layer, win, caw, cab, nag, nab, cbw, wout, g, b):
    bsz, s, d = x.shape
    tm = min(MIX_ROWS, s)
    ca, cb = CONV_A_CH, CONV_B_CH
    return pl.pallas_call(
        _even_kernel,
        grid=(bsz, s // tm),
        in_specs=[
            pl.BlockSpec((1, tm, d), lambda i, t: (i, t, 0)),
            pl.BlockSpec((1, 1, 1, 3, d), lambda i, t: (layer, i, 1, 0, 0)),
            _const_spec(win.shape),
            _const_spec(caw.shape),
            _const_spec((1, ca)),
            _const_spec((1, ca)),
            _const_spec((1, ca)),
            _const_spec(cbw.shape),
            _const_spec(wout.shape),
            _const_spec((1, d)),
            _const_spec((1, d)),
        ],
        out_specs=pl.BlockSpec((1, tm, d), lambda i, t: (i, t, 0)),
        out_shape=jax.ShapeDtypeStruct(x.shape, F32),
        scratch_shapes=[
            pltpu.VMEM((CONV_HALO + tm, ca), F32),
            pltpu.VMEM((SUBLANES - 1, CONV_HALO + tm - SUBLANES, ca), F32),
            pltpu.VMEM((CONV_B_HALO + tm, cb), F32),
            pltpu.VMEM((tm, ca + cb), BF16),
        ],
        compiler_params=_cparams(("arbitrary", "arbitrary")),
        name="even_mixer",
    )(x, mod, win, caw, cab.reshape(1, ca), nag.reshape(1, ca), nab.reshape(1, ca), cbw, wout,
      g.reshape(1, d), b.reshape(1, d))


def _sublane_replicated(w):
    return jnp.broadcast_to(w[:, None, :], (w.shape[0], SUBLANES, w.shape[1]))


KX_COLS = 256
KX_WIN = 128
MASK_ROWS = 16
QX_ROWS = HEAD_DIM + 2 * MASK_ROWS
VX_ROWS = HEAD_DIM + 16
LOG2E = math.log2(math.e)


def _odd_proj_kernel(x_ref, mod_ref, w_ref, wt_ref, kx_ref, kvc_ref, u_ref, qt_ref, vx_ref, gt_ref):
    tm = x_ref.shape[1]
    x = x_ref[0]
    mod = mod_ref[0, 0, 0]
    h = (x * (1.0 + mod[1:2]) + mod[0:1]).astype(BF16)
    p = _dot(h, w_ref[...])
    c1 = KX_COLS
    c2 = c1 + 2 * HEAD_DIM
    row = lax.broadcasted_iota(jnp.int32, (tm, KX_COLS), 0)
    col = lax.broadcasted_iota(jnp.int32, (tm, KX_COLS), 1)
    first = HEAD_DIM + MASK_ROWS * ((row // ATT_K) % 2)
    member = (col - first == (row % ATT_K) // SLC_BLOCK) & (col >= first)
    kx_ref[0] = jnp.where(member, 1.0, p[:, 0:c1]).astype(BF16)
    kvc_ref[0] = p[:, c1:c2]
    u_ref[0] = p[:, c2:]
    pt = _dot_nt(wt_ref[...], h)
    r1 = NSA_Q
    r2 = r1 + 2 * VX_ROWS
    qt_ref[0] = (pt[0:r1] * (HEAD_DIM ** -0.5 * LOG2E)).astype(BF16)
    vrow = lax.broadcasted_iota(jnp.int32, (2 * VX_ROWS, tm), 0)
    vx_ref[0] = jnp.where(vrow % VX_ROWS == HEAD_DIM, 1.0, pt[r1:r2]).astype(BF16)
    gt_ref[0] = jax.nn.sigmoid(pt[r2:])


def _odd_proj(x, mod, layer, w, wt):
    bsz, s, d = x.shape
    tm = min(MIX_ROWS, s)

    def blk(c):
        return pl.BlockSpec((1, tm, c), lambda i, t: (i, t, 0))

    def blk_t(r):
        return pl.BlockSpec((1, r, tm), lambda i, t: (i, 0, t))

    return pl.pallas_call(
        _odd_proj_kernel,
        grid=(bsz, s // tm),
        in_specs=[
            blk(d),
            pl.BlockSpec((1, 1, 1, 3, d), lambda i, t: (layer, i, 1, 0, 0)),
            _const_spec(w.shape),
            _const_spec(wt.shape),
        ],
        out_specs=[blk(KX_COLS), blk(2 * HEAD_DIM), blk(POOL_CH),
                   blk_t(NSA_Q), blk_t(2 * VX_ROWS), blk_t(3 * NSA_HEADS)],
        out_shape=[
            jax.ShapeDtypeStruct((bsz, s, KX_COLS), BF16),
            jax.ShapeDtypeStruct((bsz, s, 2 * HEAD_DIM), F32),
            jax.ShapeDtypeStruct((bsz, s, POOL_CH), F32),
            jax.ShapeDtypeStruct((bsz, NSA_Q, s), BF16),
            jax.ShapeDtypeStruct((bsz, 2 * VX_ROWS, s), BF16),
            jax.ShapeDtypeStruct((bsz, 3 * NSA_HEADS, s), F32),
        ],
        compiler_params=_cparams(("arbitrary", "arbitrary")),
        name="odd_proj",
    )(x, mod, w, wt)


def _compress_kernel(x_ref, pe_ref, wt_ref, wb_ref, w2k_ref, w2vt_ref, kc_ref, vct_ref):
    nrow = kc_ref.shape[1]
    width = x_ref.shape[2]
    top = jnp.zeros((nrow, wt_ref.shape[1]), F32)
    bot = jnp.zeros((nrow, wb_ref.shape[1]), F32)
    for j in range(CMP_STRIDE):
        xj = x_ref[0, pl.ds(j, nrow, stride=CMP_STRIDE), :]
        cols = slice(j * width, (j + 1) * width)
        top = top + _dot((xj + pe_ref[0:1, cols]).astype(BF16), wt_ref[cols, :])
        bot = bot + _dot((xj + pe_ref[1:2, cols]).astype(BF16), wb_ref[cols, :])
    hid = top + pltpu.roll(bot, bot.shape[0] - 1, axis=0)
    act = _silu(hid).astype(BF16)
    kc_ref[0] = _dot(act[:, 0:CMP_HIDDEN], w2k_ref[...]).astype(BF16)
    vct_ref[0] = _dot_nt(w2vt_ref[...], act[:, CMP_HIDDEN:]).astype(BF16)


def _compress(kvc, pe, wt, wb, w2k, w2vt):
    bsz, s, _ = kvc.shape
    nrow = s // CMP_STRIDE
    return pl.pallas_call(
        _compress_kernel,
        grid=(bsz,),
        in_specs=[
            pl.BlockSpec((1, s, 2 * HEAD_DIM), lambda i: (i, 0, 0)),
            _const_spec(pe.shape),
            _const_spec(wt.shape),
            _const_spec(wb.shape),
            _const_spec(w2k.shape),
            _const_spec(w2vt.shape),
        ],
        out_specs=[pl.BlockSpec((1, nrow, HEAD_DIM), lambda i: (i, 0, 0)),
                   pl.BlockSpec((1, HEAD_DIM, nrow), lambda i: (i, 0, 0))],
        out_shape=[jax.ShapeDtypeStruct((bsz, nrow, HEAD_DIM), BF16),
                   jax.ShapeDtypeStruct((bsz, HEAD_DIM, nrow), BF16)],
        compiler_params=_cparams(("arbitrary",)),
        name="compress",
    )(kvc, pe, wt, wb, w2k, w2vt)


CMP_TAB_SHIFT = LANES - ATT_Q // CMP_STRIDE
CMP_TAB_ROWS = CMP_TAB_SHIFT + LANES


def _t5_bucket_table(max_dist):
    n = np.arange(max_dist, dtype=np.int64)
    exact = REL_BUCKETS // 2
    nf = np.maximum(n, 1).astype(np.float32)
    large = exact + (np.log(nf / np.float32(exact)) / np.float32(math.log(REL_MAX_DIST / exact))
                     * np.float32(REL_BUCKETS - exact)).astype(np.int32)
    return np.where(n < exact, n, np.minimum(large, REL_BUCKETS - 1)).astype(np.int32)


def _bias_tables(rel_bias):
    far = REL_BUCKETS - 1
    table = _t5_bucket_table(1024)

    def buckets(dist):
        return np.where(dist >= 0, table[np.clip(dist, 0, 1023)], REL_BUCKETS).reshape(-1)

    r = np.arange(ATT_K // 2)[:, None]
    c = np.arange(ATT_Q)[None, :]
    jj = np.arange(CMP_TAB_ROWS)[:, None]
    bucket = np.concatenate([buckets(c - r), buckets(c - CMP_STRIDE * (jj - CMP_TAB_SHIFT) - (CMP_BLOCK - 1))])
    onehot = (jnp.asarray(bucket)[None, :] == jnp.arange(REL_BUCKETS + 1)[:, None]).astype(F32)
    rb = (rel_bias.astype(F32) - rel_bias[far:far + 1].astype(F32)).T * LOG2E
    rb = jnp.concatenate([rb, jnp.full((rb.shape[0], 1), NEG_INF, F32)], axis=1)
    flat = jnp.dot(rb, onehot, precision=lax.Precision.HIGHEST)
    nh = rel_bias.shape[1]
    n_tile = (ATT_K // 2) * ATT_Q
    tile = flat[:, :n_tile].reshape(nh, ATT_K // 2, ATT_Q)
    cmp_tab = flat[:, n_tile:].reshape(nh, CMP_TAB_ROWS, ATT_Q)
    return tile, cmp_tab


def _attn_kernel(qt_ref, gt_ref, kc_ref, vct_ref, kx_ref, vx_ref, tile_ref, cmpb_ref, ovt_ref, pm_ref,
                 o_ref, qx, selb, old_mask, acc_s, m_sel, acc_w, m_win, ot):
    nh, hd = NSA_HEADS, HEAD_DIM
    tq, tk = ATT_Q, ATT_K
    i = pl.program_id(1)
    t0 = i * tq
    ncmp = kc_ref.shape[1]

    @pl.when(i == 0)
    def _():
        r = lax.broadcasted_iota(jnp.int32, (tk // 2, tq // 2), 0)
        c = lax.broadcasted_iota(jnp.int32, (tk // 2, tq // 2), 1)
        old_mask[...] = jnp.where(r > c, 0.0, NEG_INF)
        for h in range(nh):
            qx[h, hd:QX_ROWS, :] = jnp.zeros((QX_ROWS - hd, tq), BF16)

    for h in range(nh):
        qx[h, 0:hd, :] = qt_ref[0, h * hd:(h + 1) * hd, :]

    def gate(branch, h):
        r = branch * nh + h
        return gt_ref[0, r:r + 1, :]

    def run_pipelined(steps):
        work, before = [], {}
        for st in steps:
            if callable(st):
                before.setdefault(len(work), []).append(st)
            else:
                work.append(st)
        pending = {}

        def issue(j):
            for f in before.get(j, ()):
                f()
            pending[j] = work[j][0]()

        for j in range(min(HEAD_LOOKAHEAD, len(work))):
            issue(j)
        for j in range(len(work)):
            if j + HEAD_LOOKAHEAD < len(work):
                issue(j + HEAD_LOOKAHEAD)
            work[j][1](pending.pop(j))

    kc = kc_ref[0]
    vct = vct_ref[0]
    start = pl.multiple_of(CMP_TAB_SHIFT - (tq // CMP_STRIDE) * i, 16)
    tok = t0 + lax.broadcasted_iota(jnp.int32, (1, tq), 1)
    any_valid = jnp.where(tok >= CMP_BLOCK - 1, 1.0, 0.0)
    psum = []

    def cmp_logits(h):
        return _dot(kc, qx[h, 0:hd, :]) + cmpb_ref[h, pl.ds(start, ncmp), :]

    def cmp_softmax(h, s):
        e = jnp.exp2(s - jnp.max(s, axis=0, keepdims=True))
        prob = e * (any_valid / jnp.sum(e, axis=0, keepdims=True))
        psum[-1] = psum[-1] + prob
        ot[h] = gate(0, h) * _dot(vct, prob.astype(BF16))

    def cmp_steps():
        psum.append(jnp.zeros((ncmp, tq), F32))
        return [(functools.partial(cmp_logits, h), functools.partial(cmp_softmax, h)) for h in range(nh)]

    def select_blocks():
        p_hi = psum[-1].astype(BF16)
        p_lo = (psum[-1] - p_hi.astype(F32)).astype(BF16)
        ovt = ovt_ref[...]
        imp = _dot(ovt, p_hi) + _dot(ovt, p_lo)
        nslc = ovt.shape[0]
        blk = lax.broadcasted_iota(jnp.int32, (nslc, tq), 0)
        cur = (t0 + lax.broadcasted_iota(jnp.int32, (nslc, tq), 1)) // SLC_BLOCK
        forced = (blk == 0) | (blk == cur) | (blk == cur - 1)
        val = jnp.where(forced, FORCE, jnp.where(blk <= cur, imp, -FORCE))
        rank = jnp.zeros((nslc, tq), F32)
        for k in range(nslc):
            vk = val[k:k + 1, :]
            rank = rank + jnp.where(blk > k, jnp.where(vk >= val, 1.0, 0.0), jnp.where(vk > val, 1.0, 0.0))
        selbias = jnp.where(rank < float(min(SLC_TOPK, nslc)), 0.0, NEG_INF).astype(BF16)
        selb[...] = _dot(pm_ref[...], selbias).reshape(selb.shape).astype(BF16)

    def set_sel_rows(c):
        tile = selb[c]
        off = hd + MASK_ROWS * (c % 2)
        if not isinstance(off, int):
            off = pl.multiple_of(off, MASK_ROWS)
        for h in range(nh):
            qx[h, pl.ds(off, MASK_ROWS), :] = tile

    def chunk_steps(row0, selected, mode, first):
        row0 = pl.multiple_of(row0, tk)
        half = tk // 2
        assert first == (mode == "cur")
        acc, m_run = (acc_s, m_sel) if selected else (acc_w, m_win)

        def logits(h):
            kcols = slice(0, QX_ROWS) if selected else slice(KX_WIN, KX_WIN + hd)
            q = qx[h] if selected else qx[h, 0:hd, :]
            if mode == "cur":
                t = tile_ref[h]
                s_top = _dot(kx_ref[0, pl.ds(row0, half), kcols], q) + t
                s_br = _dot(kx_ref[0, pl.ds(row0 + half, half), kcols], q[:, half:]) + t[:, 0:half]
                return s_top, s_br
            if mode == "old":
                tri = old_mask[...]
                s_tl = _dot(kx_ref[0, pl.ds(row0, half), kcols], q[:, 0:half]) + tri
                s_bot = _dot(kx_ref[0, pl.ds(row0 + half, half), kcols], q)
                s_bot = jnp.concatenate([s_bot[:, 0:half], s_bot[:, half:] + tri], axis=1)
                return s_tl, s_bot
            s = _dot(kx_ref[0, pl.ds(row0, tk), kcols], q)
            if mode == "prev":
                t = tile_ref[h]
                bot = jnp.concatenate([s[half:, 0:half] + t[:, half:], s[half:, half:]], axis=1)
                s = jnp.concatenate([s[0:half], bot], axis=0)
            return s

        def update(h, s):
            if selected:
                vx = vx_ref[0, 0:VX_ROWS, pl.ds(row0, tk)]
            else:
                vx = vx_ref[0, VX_ROWS:2 * VX_ROWS, pl.ds(row0, tk)]
            if mode == "cur":
                s_top, s_br = s
                m_left = jnp.max(s_top[:, 0:half], axis=0, keepdims=True)
                m_right = jnp.maximum(jnp.max(s_top[:, half:], axis=0, keepdims=True),
                                      jnp.max(s_br, axis=0, keepdims=True))
                m_new = jnp.concatenate([m_left, m_right], axis=1)
                p_bot = jnp.concatenate([jnp.zeros((half, half), F32), jnp.exp2(s_br - m_right)], axis=1)
                p = jnp.concatenate([jnp.exp2(s_top - m_new), p_bot], axis=0)
                acc[h] = _dot(vx, p.astype(BF16))
            elif mode == "old":
                s_tl, s_bot = s
                m_old = m_run[h]
                m_left = jnp.maximum(jnp.max(s_tl, axis=0, keepdims=True),
                                     jnp.max(s_bot[:, 0:half], axis=0, keepdims=True))
                m_chunk = jnp.concatenate([m_left, jnp.max(s_bot[:, half:], axis=0, keepdims=True)], axis=1)
                m_new = jnp.maximum(m_old, m_chunk)
                p_top = jnp.concatenate([jnp.exp2(s_tl - m_new[:, 0:half]), jnp.zeros((half, half), F32)], axis=1)
                p = jnp.concatenate([p_top, jnp.exp2(s_bot - m_new)], axis=0)
                acc[h] = jnp.exp2(m_old - m_new) * acc[h] + _dot(vx, p.astype(BF16))
            else:
                m_old = m_run[h]
                m_new = jnp.maximum(m_old, jnp.max(s, axis=0, keepdims=True))
                pv = _dot(vx, jnp.exp2(s - m_new).astype(BF16))
                acc[h] = jnp.exp2(m_old - m_new) * acc[h] + pv
            m_run[h] = m_new

        return [(functools.partial(logits, h), functools.partial(update, h)) for h in range(nh)]

    def main_steps():
        return (cmp_steps()
                + chunk_steps(t0, False, "cur", True)
                + [select_blocks, functools.partial(set_sel_rows, i)]
                + chunk_steps(t0, True, "cur", True))

    def prev_steps():
        return ([functools.partial(set_sel_rows, i - 1)]
                + chunk_steps(t0 - tk, True, "prev", False)
                + chunk_steps(t0 - tk, False, "prev", False))

    @pl.when(i == 0)
    def _():
        run_pipelined(main_steps())

    @pl.when(i == 1)
    def _():
        run_pipelined(main_steps() + prev_steps())

    @pl.when(i >= 2)
    def _():
        run_pipelined(main_steps() + prev_steps() + chunk_steps(t0 - 2 * tk, False, "old", False))

    def far_steps(c):
        return chunk_steps(c * tk, True, "far", False)

    n_far = jnp.maximum(i - 1, 0)

    def far_pair(j, carry):
        set_sel_rows(2 * j)
        set_sel_rows(2 * j + 1)
        run_pipelined(far_steps(2 * j) + far_steps(2 * j + 1))
        return carry

    lax.fori_loop(0, n_far // 2, far_pair, 0)

    @pl.when(n_far % 2 == 1)
    def _():
        set_sel_rows(n_far - 1)
        run_pipelined(far_steps(n_far - 1))

    for h in range(nh):
        a_s = acc_s[h]
        a_w = acc_w[h]
        ot[h] = (ot[h] + (gate(1, h) / a_s[hd:hd + 1]) * a_s[0:hd]
                 + (gate(2, h) / a_w[hd:hd + 1]) * a_w[0:hd])
    o_ref[0] = ot[...].reshape(nh * hd, tq).T.astype(BF16)


def _attention(qt, gt, kc, vct, kx, vx, tile, cmp_tab, ovt, pm):
    bsz, _, s = qt.shape
    nh, hd, tq, tk = NSA_HEADS, HEAD_DIM, ATT_Q, ATT_K
    ncmp = kc.shape[1]
    nchunk = s // tk
    return pl.pallas_call(
        _attn_kernel,
        grid=(bsz, s // tq),
        in_specs=[
            pl.BlockSpec((1, NSA_Q, tq), lambda b, i: (b, 0, i)),
            pl.BlockSpec((1, 3 * nh, tq), lambda b, i: (b, 0, i)),
            pl.BlockSpec((1, ncmp, hd), lambda b, i: (b, 0, 0)),
            pl.BlockSpec((1, hd, ncmp), lambda b, i: (b, 0, 0)),
            pl.BlockSpec((1, kx.shape[1], KX_COLS), lambda b, i: (b, 0, 0)),
            pl.BlockSpec((1, 2 * VX_ROWS, vx.shape[2]), lambda b, i: (b, 0, 0)),
            _const_spec(tile.shape),
            _const_spec(cmp_tab.shape),
            _const_spec(ovt.shape),
            _const_spec(pm.shape),
        ],
        out_specs=pl.BlockSpec((1, tq, NSA_Q), lambda b, i: (b, i, 0)),
        out_shape=jax.ShapeDtypeStruct((bsz, s, NSA_Q), BF16),
        scratch_shapes=[
            pltpu.VMEM((nh, QX_ROWS, tq), BF16),
            pltpu.VMEM((nchunk, MASK_ROWS, tq), BF16),
            pltpu.VMEM((tk // 2, tq // 2), F32),
            pltpu.VMEM((nh, VX_ROWS, tq), F32),
            pltpu.VMEM((nh, 1, tq), F32),
            pltpu.VMEM((nh, VX_ROWS, tq), F32),
            pltpu.VMEM((nh, 1, tq), F32),
            pltpu.VMEM((nh, hd, tq), F32),
        ],
        compiler_params=_cparams(("arbitrary", "arbitrary")),
        name="nsa_attention",
    )(qt, gt, kc, vct, kx, vx, tile, cmp_tab, ovt, pm)


def _odd_out_kernel(x_ref, mod_ref, o_ref_in, u_ref, pw_ref, ps_ref, wo_ref, wp_ref, g_ref, b_ref, o_ref, ubuf):
    tm = x_ref.shape[1]
    t = pl.program_id(1)

    @pl.when(t == 0)
    def _():
        ubuf[0:POOL_HALO, :] = jnp.zeros((POOL_HALO, POOL_CH), F32)

    u = u_ref[0]
    ubuf[POOL_HALO:POOL_HALO + tm, :] = u

    lane_group = lax.broadcasted_iota(jnp.int32, (tm, POOL_CH), 1) // POOL_GROUP_CH
    pos = t * tm + lax.broadcasted_iota(jnp.int32, (tm, POOL_CH), 0)
    total = jnp.zeros((tm, POOL_CH), F32)
    width = jnp.ones((tm, POOL_CH), F32)
    sw = ubuf[...]
    span = 1
    for gi, w in enumerate(POOL_WINDOWS):
        while span < w:
            sw = sw + pltpu.roll(sw, span, axis=0)
            span *= 2
        total = jnp.where(lane_group == gi, sw[POOL_HALO:], total)
        width = jnp.where(lane_group == gi, float(w), width)
    cnt = jnp.minimum((pos + 1).astype(F32), width)
    dlt = (total / cnt - u).astype(BF16)
    ubuf[0:POOL_HALO, :] = ubuf[tm:tm + POOL_HALO, :]

    mod = mod_ref[0, 0, 0]
    for r0 in range(0, tm, ODD_OUT_ROWS):
        r1 = r0 + ODD_OUT_ROWS
        o_pool = (_dot(dlt[r0:r1], pw_ref[...]) * ps_ref[...]).astype(BF16)
        y = _dot(o_ref_in[0, r0:r1, :], wo_ref[...]) + _dot(o_pool, wp_ref[...])
        z = DN_ALPHA * x_ref[0, r0:r1, :] + (1.0 + mod[2:3]) * y
        o_ref[0, r0:r1, :] = _layer_norm(z, g_ref[...], b_ref[...])


def _odd_out(x, mod, layer, o_nsa, u, pw, ps, wo, wp, g, b):
    bsz, s, d = x.shape
    tm = min(MIX_ROWS, s)
    return pl.pallas_call(
        _odd_out_kernel,
        grid=(bsz, s // tm),
        in_specs=[
            pl.BlockSpec((1, tm, d), lambda i, t: (i, t, 0)),
            pl.BlockSpec((1, 1, 1, 3, d), lambda i, t: (layer, i, 1, 0, 0)),
            pl.BlockSpec((1, tm, NSA_Q), lambda i, t: (i, t, 0)),
            pl.BlockSpec((1, tm, POOL_CH), lambda i, t: (i, t, 0)),
            _const_spec(pw.shape),
            _const_spec((1, POOL_CH)),
            _const_spec(wo.shape),
            _const_spec(wp.shape),
            _const_spec((1, d)),
            _const_spec((1, d)),
        ],
        out_specs=pl.BlockSpec((1, tm, d), lambda i, t: (i, t, 0)),
        out_shape=jax.ShapeDtypeStruct(x.shape, F32),
        scratch_shapes=[pltpu.VMEM((POOL_HALO + tm, POOL_CH), F32)],
        compiler_params=_cparams(("arbitrary", "arbitrary")),
        name="odd_out",
    )(x, mod, o_nsa, u, pw, ps.reshape(1, POOL_CH), wo, wp, g.reshape(1, d), b.reshape(1, d))


def _odd_in_weights(w_in):
    hd, nh = HEAD_DIM, NSA_HEADS
    c = NSA_Q
    col = lambda k: w_in[:, c + k * hd:c + (k + 1) * hd]
    kc, vc, ks, vs, kw, vw = (col(k) for k in range(6))
    gates = w_in[:, c + 6 * hd:c + 6 * hd + 3 * nh]
    u = w_in[:, c + 6 * hd + 3 * nh:]
    perm = np.array([3 * h + j for j in range(3) for h in range(nh)])
    zc = lambda n: jnp.zeros((w_in.shape[0], n), w_in.dtype)
    w_std = jnp.concatenate([ks, zc(KX_WIN - hd), kw, zc(KX_COLS - KX_WIN - hd), kc, vc, u], axis=1)
    w_t = jnp.concatenate([w_in[:, 0:c], vs, zc(VX_ROWS - hd), vw, zc(VX_ROWS - hd), gates[:, perm]], axis=1)
    return w_std.astype(BF16), w_t.T.astype(BF16)


def _compress_weights(pe_k, pe_v, w1_k, w1_v, w2_k, w2_v):
    hd, hid, half = HEAD_DIM, CMP_HIDDEN, CMP_STRIDE

    def halves(w1):
        w = w1.reshape(CMP_BLOCK, hd, hid)
        return w[:half], w[half:]

    kt, kb = halves(w1_k)
    vt, vb = halves(w1_v)
    z = jnp.zeros((half, hd, hid), F32)

    def assemble(k_part, v_part):
        k_rows = jnp.concatenate([k_part, z], axis=2)
        v_rows = jnp.concatenate([z, v_part], axis=2)
        return jnp.concatenate([k_rows, v_rows], axis=1).reshape(half * 2 * hd, 2 * hid).astype(BF16)

    wt = assemble(kt, vt)
    wb = assemble(kb, vb)
    pe = jnp.concatenate([pe_k.reshape(2, half, hd), pe_v.reshape(2, half, hd)], axis=2)
    pe = pe.reshape(2, half * 2 * hd)
    return pe, wt, wb, w2_k.astype(BF16), w2_v.T.astype(BF16)


def _overlap_matrix_t(n_cols, n_slc):
    n_cmp = n_slc * SLC_BLOCK // CMP_STRIDE - 1
    cmp_start = np.arange(n_cmp) * CMP_STRIDE
    slc_start = np.arange(n_slc) * SLC_BLOCK
    ov = np.clip(np.minimum(cmp_start[:, None] + CMP_BLOCK, slc_start[None, :] + SLC_BLOCK)
                 - np.maximum(cmp_start[:, None], slc_start[None, :]), 0, None) / CMP_BLOCK
    out = np.zeros((n_slc, n_cols), np.float32)
    out[:, :n_cmp] = ov.T
    return jnp.asarray(out, BF16)


def _chunk_row_placement(n_chunk, n_slc):
    per = ATT_K // SLC_BLOCK
    pm = np.zeros((n_chunk * MASK_ROWS, n_slc), np.float32)
    for c in range(n_chunk):
        for b in range(per):
            pm[MASK_ROWS * c + b, per * c + b] = 1.0
    return jnp.asarray(pm, BF16)


def _pool_weight(pool_w):
    ng, gc = len(POOL_WINDOWS), POOL_GROUP_CH
    w = jnp.zeros((ng * gc, ng * gc), F32)
    for gi in range(ng):
        w = w.at[gi * gc:(gi + 1) * gc, gi * gc:(gi + 1) * gc].set(pool_w[gi])
    return w.astype(BF16)


def _odd_mixer(x, mod, layer, w_in, pe_k, pe_v, w1_k, w2_k, w1_v, w2_v, pool_w, pool_scale, w_out,
               bias_tables, g, b):
    bsz, s, _ = x.shape
    assert s % ATT_Q == 0 and s // CMP_STRIDE == LANES and ATT_Q == ATT_K and MIX_ROWS % ATT_K == 0
    assert WINDOW == 2 * ATT_K and SLC_BLOCK * MASK_ROWS >= ATT_K and POOL_WINDOWS == (2, 4, 8, 16)
    w_std, w_t = _odd_in_weights(w_in)
    kx, kvc, u, qt, vx, gt = _odd_proj(x, mod, layer, w_std, w_t)
    kc, vct = _compress(kvc, *_compress_weights(pe_k, pe_v, w1_k, w1_v, w2_k, w2_v))
    tile, cmp_tab = bias_tables
    ovt = _overlap_matrix_t(s // CMP_STRIDE, s // SLC_BLOCK)
    pm = _chunk_row_placement(s // ATT_K, s // SLC_BLOCK)
    o_nsa = _attention(qt, gt, kc, vct, kx, vx, tile, cmp_tab, ovt, pm)
    return _odd_out(x, mod, layer, o_nsa, u, _pool_weight(pool_w), pool_scale,
                    w_out[:NSA_Q].astype(BF16), w_out[NSA_Q:].astype(BF16), g, b)


def kernel(x, c, ada_w, ada_b, ln_g, ln_b, ffn_w_gate, ffn_w_up, ffn_w_down, ev_w_in, ev_conv_a_w,
           ev_conv_a_b, ev_norm_a_g, ev_norm_a_b, ev_conv_b_w, ev_w_out, od_w_in, od_cmp_pe_k,
           od_cmp_pe_v, od_cmp_w1_k, od_cmp_w2_k, od_cmp_w1_v, od_cmp_w2_v, od_pool_w, od_pool_scale,
           od_w_out, rel_bias):
    bsz, s, d = x.shape
    depth = ada_w.shape[0]
    mod = _modulation(c, ada_w, ada_b).reshape(depth, bsz, 3, 3, d)
    bias_tables = _bias_tables(rel_bias)
    weights = tuple(w[0, 0].astype(BF16) for w in (ffn_w_gate, ffn_w_up, ffn_w_down))
    f32_weights = (ffn_w_gate, ffn_w_up, ffn_w_down)
    for layer in range(depth):
        j = layer // 2
        x, weights = _ffn(x, mod, layer, 0, weights, ln_g[layer, 0], ln_b[layer, 0],
                          nxt=(layer, 1) + f32_weights)
        if layer % 2 == 0:
            x = _even_mixer(x, mod, layer, ev_w_in[j].astype(BF16), _sublane_replicated(ev_conv_a_w[j]),
                            ev_conv_a_b[j], ev_norm_a_g[j], ev_norm_a_b[j], ev_conv_b_w[j],
                            ev_w_out[j].astype(BF16), ln_g[layer, 1], ln_b[layer, 1])
        else:
            x = _odd_mixer(x, mod, layer, od_w_in[j], od_cmp_pe_k[j], od_cmp_pe_v[j], od_cmp_w1_k[j],
                           od_cmp_w2_k[j], od_cmp_w1_v[j], od_cmp_w2_v[j], od_pool_w[j], od_pool_scale[j],
                           od_w_out[j], bias_tables, ln_g[layer, 1], ln_b[layer, 1])
        x, weights = _ffn(x, mod, layer, 2, weights, ln_g[layer, 2], ln_b[layer, 2],
                          nxt=(layer + 1, 0) + f32_weights if layer + 1 < depth else None)
    return x
```

```python
import functools
import math

import numpy as np
import jax
import jax.numpy as jnp
from jax import lax
from jax.experimental import pallas as pl
from jax.experimental.pallas import tpu as pltpu

F32 = jnp.float32
BF16 = jnp.bfloat16

DEPTH = 4
FFN_RES_WEIGHT = 0.5
CONV_A_CH = 512
CONV_A_WIDTH = 31
CONV_B_CH = 512
CONV_B_WIDTH = 3
NSA_HEADS = 16
HEAD_DIM = 64
CMP_BLOCK = 32
CMP_STRIDE = 16
CMP_HIDDEN = 128
SLC_BLOCK = 64
SLC_TOPK = 16
WINDOW = 512
POOL_WINDOWS = (2, 4, 8, 16)
POOL_GROUP_CH = 64
POOL_CH = len(POOL_WINDOWS) * POOL_GROUP_CH
REL_BUCKETS = 32
REL_MAX_DIST = 128
DN_ALPHA = (2 * DEPTH) ** 0.25
LN_EPS = 1e-5
NSA_Q = NSA_HEADS * HEAD_DIM
NEG_INF = -1e30
FORCE = 1e30

LANES = 128
SUBLANES = 8
VMEM_LIMIT_BYTES = 56 * 1024 * 1024

FFN_ROWS = 1024
FFN_SUB_ROWS = 256
MIX_ROWS = 1024
EVEN_OUT_ROWS = 256
ODD_OUT_ROWS = 256
ATT_Q = 256
ATT_K = 256
CONV_HALO = 32
CONV_B_HALO = 8
POOL_HALO = 16
HEAD_LOOKAHEAD = 5


def _cparams(sem):
    return pltpu.CompilerParams(dimension_semantics=sem, vmem_limit_bytes=VMEM_LIMIT_BYTES)


def _const_spec(shape):
    n = len(shape)
    return pl.BlockSpec(shape, lambda *_: (0,) * n, pipeline_mode=pl.Buffered(1))


def _layer_norm(z, g, b):
    mu = jnp.mean(z, axis=-1, keepdims=True)
    zc = z - mu
    var = jnp.mean(zc * zc, axis=-1, keepdims=True)
    return zc * lax.rsqrt(var + LN_EPS) * g + b


def _silu(v):
    return v * jax.nn.sigmoid(v)


def _dot(a, b):
    return jnp.dot(a, b, preferred_element_type=F32)


def _dot_nt(a, b):
    return lax.dot_general(a, b, (((1,), (1,)), ((), ())), preferred_element_type=F32)


def _mod_kernel(c_ref, w_ref, b_ref, o_ref):
    cond = _silu(c_ref[...])
    o_ref[0] = _dot(cond.astype(BF16), w_ref[0].astype(BF16)) + b_ref[0]


def _modulation(c, ada_w, ada_b):
    depth, d, n = ada_w.shape
    bsz = c.shape[0]
    tn = 1152
    return pl.pallas_call(
        _mod_kernel,
        grid=(depth, n // tn),
        in_specs=[
            pl.BlockSpec((bsz, d), lambda l, j: (0, 0)),
            pl.BlockSpec((1, d, tn), lambda l, j: (l, 0, j)),
            pl.BlockSpec((1, 1, tn), lambda l, j: (l, 0, j)),
        ],
        out_specs=pl.BlockSpec((1, bsz, tn), lambda l, j: (l, 0, j)),
        out_shape=jax.ShapeDtypeStruct((depth, bsz, n), F32),
        compiler_params=_cparams(("arbitrary", "arbitrary")),
        name="modulation",
    )(c, ada_w, ada_b.reshape(depth, 1, n))


def _ffn_kernel(*refs, cast_next):
    if cast_next:
        (x_ref, mod_ref, wg_ref, wu_ref, wd_ref, g_ref, b_ref, ng_ref, nu_ref, nd_ref,
         o_ref, og_ref, ou_ref, od_ref) = refs
        og_ref[...] = ng_ref[0, 0].astype(BF16)
        ou_ref[...] = nu_ref[0, 0].astype(BF16)
        od_ref[...] = nd_ref[0, 0].astype(BF16)
    else:
        x_ref, mod_ref, wg_ref, wu_ref, wd_ref, g_ref, b_ref, o_ref = refs
    mod = mod_ref[0, 0, 0]
    tm = x_ref.shape[1]
    for r0 in range(0, tm, FFN_SUB_ROWS):
        x = x_ref[0, r0:r0 + FFN_SUB_ROWS, :]
        h = (x * (1.0 + mod[1:2]) + mod[0:1]).astype(BF16)
        ff = wg_ref.shape[1]
        cut = (ff // 256 + 1) // 2 * 256
        y = None
        for c0, c1 in ((0, cut), (cut, ff)):
            a = (_silu(_dot(h, wg_ref[:, c0:c1])) * _dot(h, wu_ref[:, c0:c1])).astype(BF16)
            part = _dot(a, wd_ref[c0:c1, :])
            y = part if y is None else y + part
        z = DN_ALPHA * x + (FFN_RES_WEIGHT * (1.0 + mod[2:3])) * y
        o_ref[0, r0:r0 + FFN_SUB_ROWS, :] = _layer_norm(z, g_ref[...], b_ref[...])


def _ffn(x, mod, layer, sub, weights, g, b, nxt=None):
    bsz, s, d = x.shape
    wg, wu, wd = weights
    ff = wg.shape[1]
    tm = min(FFN_ROWS, s)
    nt = s // tm
    steps = bsz * nt
    in_specs = [
        pl.BlockSpec((1, tm, d), lambda i, t: (i, t, 0)),
        pl.BlockSpec((1, 1, 1, 3, d), lambda i, t: (layer, i, sub, 0, 0)),
        _const_spec((d, ff)),
        _const_spec((d, ff)),
        _const_spec((ff, d)),
        _const_spec((1, d)),
        _const_spec((1, d)),
    ]
    out_specs = [pl.BlockSpec((1, tm, d), lambda i, t: (i, t, 0))]
    out_shape = [jax.ShapeDtypeStruct(x.shape, F32)]
    args = [x, mod, wg, wu, wd, g.reshape(1, d), b.reshape(1, d)]
    if nxt is not None:
        nl, nh, w_gate, w_up, w_down = nxt
        up_rows = d // steps
        down_steps = steps // 2
        down_rows = ff // down_steps
        assert d % steps == 0 and up_rows % 16 == 0 and ff % down_steps == 0 and down_rows % 16 == 0
        in_specs += [
            pl.BlockSpec((1, 1, up_rows, ff), lambda i, t: (nl, nh, i * nt + t, 0)),
            pl.BlockSpec((1, 1, up_rows, ff), lambda i, t: (nl, nh, i * nt + t, 0)),
            pl.BlockSpec((1, 1, down_rows, d), lambda i, t: (nl, nh, (i * nt + t) // 2, 0)),
        ]
        out_specs += [
            pl.BlockSpec((up_rows, ff), lambda i, t: (i * nt + t, 0)),
            pl.BlockSpec((up_rows, ff), lambda i, t: (i * nt + t, 0)),
            pl.BlockSpec((down_rows, d), lambda i, t: ((i * nt + t) // 2, 0)),
        ]
        out_shape += [jax.ShapeDtypeStruct((d, ff), BF16), jax.ShapeDtypeStruct((d, ff), BF16),
                      jax.ShapeDtypeStruct((ff, d), BF16)]
        args += [w_gate, w_up, w_down]
    out = pl.pallas_call(
        functools.partial(_ffn_kernel, cast_next=nxt is not None),
        grid=(bsz, nt),
        in_specs=in_specs,
        out_specs=out_specs,
        out_shape=out_shape,
        compiler_params=_cparams(("arbitrary", "arbitrary")),
        name="ffn",
    )(*args)
    return out[0], (tuple(out[1:]) if nxt is not None else None)


def _even_kernel(x_ref, mod_ref, win_ref, caw_ref, cab_ref, nag_ref, nab_ref, cbw_ref, wout_ref,
                 g_ref, b_ref, o_ref, abuf, ash, bbuf, cat):
    tm = x_ref.shape[1]
    ca, cb = CONV_A_CH, CONV_B_CH

    @pl.when(pl.program_id(1) == 0)
    def _():
        abuf[0:CONV_HALO, :] = jnp.zeros((CONV_HALO, ca), F32)
        bbuf[0:CONV_B_HALO, :] = jnp.zeros((CONV_B_HALO, cb), F32)

    x = x_ref[0]
    mod = mod_ref[0, 0, 0]
    h = (x * (1.0 + mod[1:2]) + mod[0:1]).astype(BF16)
    p = _dot(h, win_ref[...])
    abuf[CONV_HALO:CONV_HALO + tm, :] = p[:, 0:ca] * jax.nn.sigmoid(p[:, ca:2 * ca])
    gate_b = p[:, 2 * ca:2 * ca + cb]
    bbuf[CONV_B_HALO:CONV_B_HALO + tm, :] = p[:, 2 * ca + cb:2 * ca + 2 * cb] * p[:, 2 * ca + 2 * cb:]

    span = tm + CONV_HALO - SUBLANES
    for r in range(1, SUBLANES):
        ash[r - 1] = abuf[r:r + span, :]

    rows = 64
    for r0 in range(0, tm, rows):
        acc = jnp.zeros((rows // SUBLANES, SUBLANES, ca), F32)
        for k in range(CONV_A_WIDTH):
            off = CONV_HALO + r0 - (CONV_A_WIDTH - 1) + k
            r = off % SUBLANES
            tap = abuf[off:off + rows, :] if r == 0 else ash[r - 1, off - r:off - r + rows, :]
            acc = acc + caw_ref[k][None] * tap.reshape(rows // SUBLANES, SUBLANES, ca)
        acc = acc.reshape(rows, ca)
        u = _layer_norm(acc + cab_ref[...], nag_ref[...], nab_ref[...])
        cat[r0:r0 + rows, 0:ca] = _silu(u).astype(BF16)
        accb = jnp.zeros((rows, cb), F32)
        for k in range(CONV_B_WIDTH):
            off = CONV_B_HALO + r0 - (CONV_B_WIDTH - 1) + k
            accb = accb + cbw_ref[k:k + 1, :] * bbuf[off:off + rows, :]
        cat[r0:r0 + rows, ca:ca + cb] = (gate_b[r0:r0 + rows] * accb).astype(BF16)

        s1 = r0 + rows
        if s1 % EVEN_OUT_ROWS == 0:
            s0 = s1 - EVEN_OUT_ROWS
            y = _dot(cat[s0:s1, :], wout_ref[...])
            z = DN_ALPHA * x[s0:s1] + (1.0 + mod[2:3]) * y
            o_ref[0, s0:s1, :] = _layer_norm(z, g_ref[...], b_ref[...])

    abuf[0:CONV_HALO, :] = abuf[tm:tm + CONV_HALO, :]
    bbuf[0:CONV_B_HALO, :] = bbuf[tm:tm + CONV_B_HALO, :]


def _even_mixer(x, mod, layer, win, caw, cab, nag, nab, cbw, wout, g, b):
    bsz, s, d = x.shape
    tm = min(MIX_ROWS, s)
    ca, cb = CONV_A_CH, CONV_B_CH
    return pl.pallas_call(
        _even_kernel,
        grid=(bsz, s // tm),
        in_specs=[
            pl.BlockSpec((1, tm, d), lambda i, t: (i, t, 0)),
            pl.BlockSpec((1, 1, 1, 3, d), lambda i, t: (layer, i, 1, 0, 0)),
            _const_spec(win.shape),
            _const_spec(caw.shape),
            _const_spec((1, ca)),
            _const_spec((1, ca)),
            _const_spec((1, ca)),
            _const_spec(cbw.shape),
            _const_spec(wout.shape),
            _const_spec((1, d)),
            _const_spec((1, d)),
        ],
        out_specs=pl.BlockSpec((1, tm, d), lambda i, t: (i, t, 0)),
        out_shape=jax.ShapeDtypeStruct(x.shape, F32),
        scratch_shapes=[
            pltpu.VMEM((CONV_HALO + tm, ca), F32),
            pltpu.VMEM((SUBLANES - 1, CONV_HALO + tm - SUBLANES, ca), F32),
            pltpu.VMEM((CONV_B_HALO + tm, cb), F32),
            pltpu.VMEM((tm, ca + cb), BF16),
        ],
        compiler_params=_cparams(("arbitrary", "arbitrary")),
        name="even_mixer",
    )(x, mod, win, caw, cab.reshape(1, ca), nag.reshape(1, ca), nab.reshape(1, ca), cbw, wout,
      g.reshape(1, d), b.reshape(1, d))


def _sublane_replicated(w):
    return jnp.broadcast_to(w[:, None, :], (w.shape[0], SUBLANES, w.shape[1]))


KX_COLS = 256
KX_WIN = 128
MASK_ROWS = 16
QX_ROWS = HEAD_DIM + 2 * MASK_ROWS
VX_ROWS = HEAD_DIM + 16
LOG2E = math.log2(math.e)


def _odd_proj_kernel(x_ref, mod_ref, w_ref, wt_ref, kx_ref, kvc_ref, u_ref, qt_ref, vx_ref, gt_ref):
    tm = x_ref.shape[1]
    x = x_ref[0]
    mod = mod_ref[0, 0, 0]
    h = (x * (1.0 + mod[1:2]) + mod[0:1]).astype(BF16)
    p = _dot(h, w_ref[...])
    c1 = KX_COLS
    c2 = c1 + 2 * HEAD_DIM
    row = lax.broadcasted_iota(jnp.int32, (tm, KX_COLS), 0)
    col = lax.broadcasted_iota(jnp.int32, (tm, KX_COLS), 1)
    first = HEAD_DIM + MASK_ROWS * ((row // ATT_K) % 2)
    member = (col - first == (row % ATT_K) // SLC_BLOCK) & (col >= first)
    kx_ref[0] = jnp.where(member, 1.0, p[:, 0:c1]).astype(BF16)
    kvc_ref[0] = p[:, c1:c2]
    u_ref[0] = p[:, c2:]
    pt = _dot_nt(wt_ref[...], h)
    r1 = NSA_Q
    r2 = r1 + 2 * VX_ROWS
    qt_ref[0] = (pt[0:r1] * (HEAD_DIM ** -0.5 * LOG2E)).astype(BF16)
    vrow = lax.broadcasted_iota(jnp.int32, (2 * VX_ROWS, tm), 0)
    vx_ref[0] = jnp.where(vrow % VX_ROWS == HEAD_DIM, 1.0, pt[r1:r2]).astype(BF16)
    gt_ref[0] = jax.nn.sigmoid(pt[r2:])


def _odd_proj(x, mod, layer, w, wt):
    bsz, s, d = x.shape
    tm = min(MIX_ROWS, s)

    def blk(c):
        return pl.BlockSpec((1, tm, c), lambda i, t: (i, t, 0))

    def blk_t(r):
        return pl.BlockSpec((1, r, tm), lambda i, t: (i, 0, t))

    return pl.pallas_call(
        _odd_proj_kernel,
        grid=(bsz, s // tm),
        in_specs=[
            blk(d),
            pl.BlockSpec((1, 1, 1, 3, d), lambda i, t: (layer, i, 1, 0, 0)),
            _const_spec(w.shape),
            _const_spec(wt.shape),
        ],
        out_specs=[blk(KX_COLS), blk(2 * HEAD_DIM), blk(POOL_CH),
                   blk_t(NSA_Q), blk_t(2 * VX_ROWS), blk_t(3 * NSA_HEADS)],
        out_shape=[
            jax.ShapeDtypeStruct((bsz, s, KX_COLS), BF16),
            jax.ShapeDtypeStruct((bsz, s, 2 * HEAD_DIM), F32),
            jax.ShapeDtypeStruct((bsz, s, POOL_CH), F32),
            jax.ShapeDtypeStruct((bsz, NSA_Q, s), BF16),
            jax.ShapeDtypeStruct((bsz, 2 * VX_ROWS, s), BF16),
            jax.ShapeDtypeStruct((bsz, 3 * NSA_HEADS, s), F32),
        ],
        compiler_params=_cparams(("arbitrary", "arbitrary")),
        name="odd_proj",
    )(x, mod, w, wt)


def _compress_kernel(x_ref, pe_ref, wt_ref, wb_ref, w2k_ref, w2vt_ref, kc_ref, vct_ref):
    nrow = kc_ref.shape[1]
    width = x_ref.shape[2]
    top = jnp.zeros((nrow, wt_ref.shape[1]), F32)
    bot = jnp.zeros((nrow, wb_ref.shape[1]), F32)
    for j in range(CMP_STRIDE):
        xj = x_ref[0, pl.ds(j, nrow, stride=CMP_STRIDE), :]
        cols = slice(j * width, (j + 1) * width)
        top = top + _dot((xj + pe_ref[0:1, cols]).astype(BF16), wt_ref[cols, :])
        bot = bot + _dot((xj + pe_ref[1:2, cols]).astype(BF16), wb_ref[cols, :])
    hid = top + pltpu.roll(bot, bot.shape[0] - 1, axis=0)
    act = _silu(hid).astype(BF16)
    kc_ref[0] = _dot(act[:, 0:CMP_HIDDEN], w2k_ref[...]).astype(BF16)
    vct_ref[0] = _dot_nt(w2vt_ref[...], act[:, CMP_HIDDEN:]).astype(BF16)


def _compress(kvc, pe, wt, wb, w2k, w2vt):
    bsz, s, _ = kvc.shape
    nrow = s // CMP_STRIDE
    return pl.pallas_call(
        _compress_kernel,
        grid=(bsz,),
        in_specs=[
            pl.BlockSpec((1, s, 2 * HEAD_DIM), lambda i: (i, 0, 0)),
            _const_spec(pe.shape),
            _const_spec(wt.shape),
            _const_spec(wb.shape),
            _const_spec(w2k.shape),
            _const_spec(w2vt.shape),
        ],
        out_specs=[pl.BlockSpec((1, nrow, HEAD_DIM), lambda i: (i, 0, 0)),
                   pl.BlockSpec((1, HEAD_DIM, nrow), lambda i: (i, 0, 0))],
        out_shape=[jax.ShapeDtypeStruct((bsz, nrow, HEAD_DIM), BF16),
                   jax.ShapeDtypeStruct((bsz, HEAD_DIM, nrow), BF16)],
        compiler_params=_cparams(("arbitrary",)),
        name="compress",
    )(kvc, pe, wt, wb, w2k, w2vt)


CMP_TAB_SHIFT = LANES - ATT_Q // CMP_STRIDE
CMP_TAB_ROWS = CMP_TAB_SHIFT + LANES


def _t5_bucket_table(max_dist):
    n = np.arange(max_dist, dtype=np.int64)
    exact = REL_BUCKETS // 2
    nf = np.maximum(n, 1).astype(np.float32)
    large = exact + (np.log(nf / np.float32(exact)) / np.float32(math.log(REL_MAX_DIST / exact))
                     * np.float32(REL_BUCKETS - exact)).astype(np.int32)
    return np.where(n < exact, n, np.minimum(large, REL_BUCKETS - 1)).astype(np.int32)


def _bias_tables(rel_bias):
    far = REL_BUCKETS - 1
    table = _t5_bucket_table(1024)

    def buckets(dist):
        return np.where(dist >= 0, table[np.clip(dist, 0, 1023)], REL_BUCKETS).reshape(-1)

    r = np.arange(ATT_K // 2)[:, None]
    c = np.arange(ATT_Q)[None, :]
    jj = np.arange(CMP_TAB_ROWS)[:, None]
    bucket = np.concatenate([buckets(c - r), buckets(c - CMP_STRIDE * (jj - CMP_TAB_SHIFT) - (CMP_BLOCK - 1))])
    onehot = (jnp.asarray(bucket)[None, :] == jnp.arange(REL_BUCKETS + 1)[:, None]).astype(F32)
    rb = (rel_bias.astype(F32) - rel_bias[far:far + 1].astype(F32)).T * LOG2E
    rb = jnp.concatenate([rb, jnp.full((rb.shape[0], 1), NEG_INF, F32)], axis=1)
    flat = jnp.dot(rb, onehot, precision=lax.Precision.HIGHEST)
    nh = rel_bias.shape[1]
    n_tile = (ATT_K // 2) * ATT_Q
    tile = flat[:, :n_tile].reshape(nh, ATT_K // 2, ATT_Q)
    cmp_tab = flat[:, n_tile:].reshape(nh, CMP_TAB_ROWS, ATT_Q)
    return tile, cmp_tab


def _attn_kernel(qt_ref, gt_ref, kc_ref, vct_ref, kx_ref, vx_ref, tile_ref, cmpb_ref, ovt_ref, pm_ref,
                 o_ref, qx, selb, old_mask, acc_s, m_sel, acc_w, m_win, ot):
    nh, hd = NSA_HEADS, HEAD_DIM
    tq, tk = ATT_Q, ATT_K
    i = pl.program_id(1)
    t0 = i * tq
    ncmp = kc_ref.shape[1]

    @pl.when(i == 0)
    def _():
        r = lax.broadcasted_iota(jnp.int32, (tk // 2, tq // 2), 0)
        c = lax.broadcasted_iota(jnp.int32, (tk // 2, tq // 2), 1)
        old_mask[...] = jnp.where(r > c, 0.0, NEG_INF)
        for h in range(nh):
            qx[h, hd:QX_ROWS, :] = jnp.zeros((QX_ROWS - hd, tq), BF16)

    for h in range(nh):
        qx[h, 0:hd, :] = qt_ref[0, h * hd:(h + 1) * hd, :]

    def gate(branch, h):
        r = branch * nh + h
        return gt_ref[0, r:r + 1, :]

    def run_pipelined(steps):
        work, before = [], {}
        for st in steps:
            if callable(st):
                before.setdefault(len(work), []).append(st)
            else:
                work.append(st)
        pending = {}

        def issue(j):
            for f in before.get(j, ()):
                f()
            pending[j] = work[j][0]()

        for j in range(min(HEAD_LOOKAHEAD, len(work))):
            issue(j)
        for j in range(len(work)):
            if j + HEAD_LOOKAHEAD < len(work):
                issue(j + HEAD_LOOKAHEAD)
            work[j][1](pending.pop(j))

    kc = kc_ref[0]
    vct = vct_ref[0]
    start = pl.multiple_of(CMP_TAB_SHIFT - (tq // CMP_STRIDE) * i, 16)
    tok = t0 + lax.broadcasted_iota(jnp.int32, (1, tq), 1)
    any_valid = jnp.where(tok >= CMP_BLOCK - 1, 1.0, 0.0)
    psum = []

    def cmp_logits(h):
        return _dot(kc, qx[h, 0:hd, :]) + cmpb_ref[h, pl.ds(start, ncmp), :]

    def cmp_softmax(h, s):
        e = jnp.exp2(s - jnp.max(s, axis=0, keepdims=True))
        prob = e * (any_valid / jnp.sum(e, axis=0, keepdims=True))
        psum[-1] = psum[-1] + prob
        ot[h] = gate(0, h) * _dot(vct, prob.astype(BF16))

    def cmp_steps():
        psum.append(jnp.zeros((ncmp, tq), F32))
        return [(functools.partial(cmp_logits, h), functools.partial(cmp_softmax, h)) for h in range(nh)]

    def select_blocks():
        p_hi = psum[-1].astype(BF16)
        p_lo = (psum[-1] - p_hi.astype(F32)).astype(BF16)
        ovt = ovt_ref[...]
        imp = _dot(ovt, p_hi) + _dot(ovt, p_lo)
        nslc = ovt.shape[0]
        blk = lax.broadcasted_iota(jnp.int32, (nslc, tq), 0)
        cur = (t0 + lax.broadcasted_iota(jnp.int32, (nslc, tq), 1)) // SLC_BLOCK
        forced = (blk == 0) | (blk == cur) | (blk == cur - 1)
        val = jnp.where(forced, FORCE, jnp.where(blk <= cur, imp, -FORCE))
        rank = jnp.zeros((nslc, tq), F32)
        for k in range(nslc):
            vk = val[k:k + 1, :]
            rank = rank + jnp.where(blk > k, jnp.where(vk >= val, 1.0, 0.0), jnp.where(vk > val, 1.0, 0.0))
        selbias = jnp.where(rank < float(min(SLC_TOPK, nslc)), 0.0, NEG_INF).astype(BF16)
        selb[...] = _dot(pm_ref[...], selbias).reshape(selb.shape).astype(BF16)

    def set_sel_rows(c):
        tile = selb[c]
        off = hd + MASK_ROWS * (c % 2)
        if not isinstance(off, int):
            off = pl.multiple_of(off, MASK_ROWS)
        for h in range(nh):
            qx[h, pl.ds(off, MASK_ROWS), :] = tile

    def chunk_steps(row0, selected, mode, first):
        row0 = pl.multiple_of(row0, tk)
        half = tk // 2
        assert first == (mode == "cur")
        acc, m_run = (acc_s, m_sel) if selected else (acc_w, m_win)

        def logits(h):
            kcols = slice(0, QX_ROWS) if selected else slice(KX_WIN, KX_WIN + hd)
            q = qx[h] if selected else qx[h, 0:hd, :]
            if mode == "cur":
                t = tile_ref[h]
                s_top = _dot(kx_ref[0, pl.ds(row0, half), kcols], q) + t
                s_br = _dot(kx_ref[0, pl.ds(row0 + half, half), kcols], q[:, half:]) + t[:, 0:half]
                return s_top, s_br
            if mode == "old":
                tri = old_mask[...]
                s_tl = _dot(kx_ref[0, pl.ds(row0, half), kcols], q[:, 0:half]) + tri
                s_bot = _dot(kx_ref[0, pl.ds(row0 + half, half), kcols], q)
                s_bot = jnp.concatenate([s_bot[:, 0:half], s_bot[:, half:] + tri], axis=1)
                return s_tl, s_bot
            s = _dot(kx_ref[0, pl.ds(row0, tk), kcols], q)
            if mode == "prev":
                t = tile_ref[h]
                bot = jnp.concatenate([s[half:, 0:half] + t[:, half:], s[half:, half:]], axis=1)
                s = jnp.concatenate([s[0:half], bot], axis=0)
            return s

        def update(h, s):
            if selected:
                vx = vx_ref[0, 0:VX_ROWS, pl.ds(row0, tk)]
            else:
                vx = vx_ref[0, VX_ROWS:2 * VX_ROWS, pl.ds(row0, tk)]
            if mode == "cur":
                s_top, s_br = s
                m_left = jnp.max(s_top[:, 0:half], axis=0, keepdims=True)
                m_right = jnp.maximum(jnp.max(s_top[:, half:], axis=0, keepdims=True),
                                      jnp.max(s_br, axis=0, keepdims=True))
                m_new = jnp.concatenate([m_left, m_right], axis=1)
                p_bot = jnp.concatenate([jnp.zeros((half, half), F32), jnp.exp2(s_br - m_right)], axis=1)
                p = jnp.concatenate([jnp.exp2(s_top - m_new), p_bot], axis=0)
                acc[h] = _dot(vx, p.astype(BF16))
            elif mode == "old":
                s_tl, s_bot = s
                m_old = m_run[h]
                m_left = jnp.maximum(jnp.max(s_tl, axis=0, keepdims=True),
                                     jnp.max(s_bot[:, 0:half], axis=0, keepdims=True))
                m_chunk = jnp.concatenate([m_left, jnp.max(s_bot[:, half:], axis=0, keepdims=True)], axis=1)
                m_new = jnp.maximum(m_old, m_chunk)
                p_top = jnp.concatenate([jnp.exp2(s_tl - m_new[:, 0:half]), jnp.zeros((half, half), F32)], axis=1)
                p = jnp.concatenate([p_top, jnp.exp2(s_bot - m_new)], axis=0)
                acc[h] = jnp.exp2(m_old - m_new) * acc[h] + _dot(vx, p.astype(BF16))
            else:
                m_old = m_run[h]
                m_new = jnp.maximum(m_old, jnp.max(s, axis=0, keepdims=True))
                pv = _dot(vx, jnp.exp2(s - m_new).astype(BF16))
                acc[h] = jnp.exp2(m_old - m_new) * acc[h] + pv
            m_run[h] = m_new

        return [(functools.partial(logits, h), functools.partial(update, h)) for h in range(nh)]

    def main_steps():
        return (cmp_steps()
                + chunk_steps(t0, False, "cur", True)
                + [select_blocks, functools.partial(set_sel_rows, i)]
                + chunk_steps(t0, True, "cur", True))

    def prev_steps():
        return ([functools.partial(set_sel_rows, i - 1)]
                + chunk_steps(t0 - tk, True, "prev", False)
                + chunk_steps(t0 - tk, False, "prev", False))

    @pl.when(i == 0)
    def _():
        run_pipelined(main_steps())

    @pl.when(i == 1)
    def _():
        run_pipelined(main_steps() + prev_steps())

    @pl.when(i >= 2)
    def _():
        run_pipelined(main_steps() + prev_steps() + chunk_steps(t0 - 2 * tk, False, "old", False))

    def far_steps(c):
        return chunk_steps(c * tk, True, "far", False)

    n_far = jnp.maximum(i - 1, 0)

    def far_pair(j, carry):
        set_sel_rows(2 * j)
        set_sel_rows(2 * j + 1)
        run_pipelined(far_steps(2 * j) + far_steps(2 * j + 1))
        return carry

    lax.fori_loop(0, n_far // 2, far_pair, 0)

    @pl.when(n_far % 2 == 1)
    def _():
        set_sel_rows(n_far - 1)
        run_pipelined(far_steps(n_far - 1))

    for h in range(nh):
        a_s = acc_s[h]
        a_w = acc_w[h]
        ot[h] = (ot[h] + (gate(1, h) / a_s[hd:hd + 1]) * a_s[0:hd]
                 + (gate(2, h) / a_w[hd:hd + 1]) * a_w[0:hd])
    o_ref[0] = ot[...].reshape(nh * hd, tq).T.astype(BF16)


def _attention(qt, gt, kc, vct, kx, vx, tile, cmp_tab, ovt, pm):
    bsz, _, s = qt.shape
    nh, hd, tq, tk = NSA_HEADS, HEAD_DIM, ATT_Q, ATT_K
    ncmp = kc.shape[1]
    nchunk = s // tk
    return pl.pallas_call(
        _attn_kernel,
        grid=(bsz, s // tq),
        in_specs=[
            pl.BlockSpec((1, NSA_Q, tq), lambda b, i: (b, 0, i)),
            pl.BlockSpec((1, 3 * nh, tq), lambda b, i: (b, 0, i)),
            pl.BlockSpec((1, ncmp, hd), lambda b, i: (b, 0, 0)),
            pl.BlockSpec((1, hd, ncmp), lambda b, i: (b, 0, 0)),
            pl.BlockSpec((1, kx.shape[1], KX_COLS), lambda b, i: (b, 0, 0)),
            pl.BlockSpec((1, 2 * VX_ROWS, vx.shape[2]), lambda b, i: (b, 0, 0)),
            _const_spec(tile.shape),
            _const_spec(cmp_tab.shape),
            _const_spec(ovt.shape),
            _const_spec(pm.shape),
        ],
        out_specs=pl.BlockSpec((1, tq, NSA_Q), lambda b, i: (b, i, 0)),
        out_shape=jax.ShapeDtypeStruct((bsz, s, NSA_Q), BF16),
        scratch_shapes=[
            pltpu.VMEM((nh, QX_ROWS, tq), BF16),
            pltpu.VMEM((nchunk, MASK_ROWS, tq), BF16),
            pltpu.VMEM((tk // 2, tq // 2), F32),
            pltpu.VMEM((nh, VX_ROWS, tq), F32),
            pltpu.VMEM((nh, 1, tq), F32),
            pltpu.VMEM((nh, VX_ROWS, tq), F32),
            pltpu.VMEM((nh, 1, tq), F32),
            pltpu.VMEM((nh, hd, tq), F32),
        ],
        compiler_params=_cparams(("arbitrary", "arbitrary")),
        name="nsa_attention",
    )(qt, gt, kc, vct, kx, vx, tile, cmp_tab, ovt, pm)


def _odd_out_kernel(x_ref, mod_ref, o_ref_in, u_ref, pw_ref, ps_ref, wo_ref, wp_ref, g_ref, b_ref, o_ref, ubuf):
    tm = x_ref.shape[1]
    t = pl.program_id(1)

    @pl.when(t == 0)
    def _():
        ubuf[0:POOL_HALO, :] = jnp.zeros((POOL_HALO, POOL_CH), F32)

    u = u_ref[0]
    ubuf[POOL_HALO:POOL_HALO + tm, :] = u

    lane_group = lax.broadcasted_iota(jnp.int32, (tm, POOL_CH), 1) // POOL_GROUP_CH
    pos = t * tm + lax.broadcasted_iota(jnp.int32, (tm, POOL_CH), 0)
    total = jnp.zeros((tm, POOL_CH), F32)
    width = jnp.ones((tm, POOL_CH), F32)
    sw = ubuf[...]
    span = 1
    for gi, w in enumerate(POOL_WINDOWS):
        while span < w:
            sw = sw + pltpu.roll(sw, span, axis=0)
            span *= 2
        total = jnp.where(lane_group == gi, sw[POOL_HALO:], total)
        width = jnp.where(lane_group == gi, float(w), width)
    cnt = jnp.minimum((pos + 1).astype(F32), width)
    dlt = (total / cnt - u).astype(BF16)
    ubuf[0:POOL_HALO, :] = ubuf[tm:tm + POOL_HALO, :]

    mod = mod_ref[0, 0, 0]
    for r0 in range(0, tm, ODD_OUT_ROWS):
        r1 = r0 + ODD_OUT_ROWS
        o_pool = (_dot(dlt[r0:r1], pw_ref[...]) * ps_ref[...]).astype(BF16)
        y = _dot(o_ref_in[0, r0:r1, :], wo_ref[...]) + _dot(o_pool, wp_ref[...])
        z = DN_ALPHA * x_ref[0, r0:r1, :] + (1.0 + mod[2:3]) * y
        o_ref[0, r0:r1, :] = _layer_norm(z, g_ref[...], b_ref[...])


def _odd_out(x, mod, layer, o_nsa, u, pw, ps, wo, wp, g, b):
    bsz, s, d = x.shape
    tm = min(MIX_ROWS, s)
    return pl.pallas_call(
        _odd_out_kernel,
        grid=(bsz, s // tm),
        in_specs=[
            pl.BlockSpec((1, tm, d), lambda i, t: (i, t, 0)),
            pl.BlockSpec((1, 1, 1, 3, d), lambda i, t: (layer, i, 1, 0, 0)),
            pl.BlockSpec((1, tm, NSA_Q), lambda i, t: (i, t, 0)),
            pl.BlockSpec((1, tm, POOL_CH), lambda i, t: (i, t, 0)),
            _const_spec(pw.shape),
            _const_spec((1, POOL_CH)),
            _const_spec(wo.shape),
            _const_spec(wp.shape),
            _const_spec((1, d)),
            _const_spec((1, d)),
        ],
        out_specs=pl.BlockSpec((1, tm, d), lambda i, t: (i, t, 0)),
        out_shape=jax.ShapeDtypeStruct(x.shape, F32),
        scratch_shapes=[pltpu.VMEM((POOL_HALO + tm, POOL_CH), F32)],
        compiler_params=_cparams(("arbitrary", "arbitrary")),
        name="odd_out",
    )(x, mod, o_nsa, u, pw, ps.reshape(1, POOL_CH), wo, wp, g.reshape(1, d), b.reshape(1, d))


def _odd_in_weights(w_in):
    hd, nh = HEAD_DIM, NSA_HEADS
    c = NSA_Q
    col = lambda k: w_in[:, c + k * hd:c + (k + 1) * hd]
    kc, vc, ks, vs, kw, vw = (col(k) for k in range(6))
    gates = w_in[:, c + 6 * hd:c + 6 * hd + 3 * nh]
    u = w_in[:, c + 6 * hd + 3 * nh:]
    perm = np.array([3 * h + j for j in range(3) for h in range(nh)])
    zc = lambda n: jnp.zeros((w_in.shape[0], n), w_in.dtype)
    w_std = jnp.concatenate([ks, zc(KX_WIN - hd), kw, zc(KX_COLS - KX_WIN - hd), kc, vc, u], axis=1)
    w_t = jnp.concatenate([w_in[:, 0:c], vs, zc(VX_ROWS - hd), vw, zc(VX_ROWS - hd), gates[:, perm]], axis=1)
    return w_std.astype(BF16), w_t.T.astype(BF16)


def _compress_weights(pe_k, pe_v, w1_k, w1_v, w2_k, w2_v):
    hd, hid, half = HEAD_DIM, CMP_HIDDEN, CMP_STRIDE

    def halves(w1):
        w = w1.reshape(CMP_BLOCK, hd, hid)
        return w[:half], w[half:]

    kt, kb = halves(w1_k)
    vt, vb = halves(w1_v)
    z = jnp.zeros((half, hd, hid), F32)

    def assemble(k_part, v_part):
        k_rows = jnp.concatenate([k_part, z], axis=2)
        v_rows = jnp.concatenate([z, v_part], axis=2)
        return jnp.concatenate([k_rows, v_rows], axis=1).reshape(half * 2 * hd, 2 * hid).astype(BF16)

    wt = assemble(kt, vt)
    wb = assemble(kb, vb)
    pe = jnp.concatenate([pe_k.reshape(2, half, hd), pe_v.reshape(2, half, hd)], axis=2)
    pe = pe.reshape(2, half * 2 * hd)
    return pe, wt, wb, w2_k.astype(BF16), w2_v.T.astype(BF16)


def _overlap_matrix_t(n_cols, n_slc):
    n_cmp = n_slc * SLC_BLOCK // CMP_STRIDE - 1
    cmp_start = np.arange(n_cmp) * CMP_STRIDE
    slc_start = np.arange(n_slc) * SLC_BLOCK
    ov = np.clip(np.minimum(cmp_start[:, None] + CMP_BLOCK, slc_start[None, :] + SLC_BLOCK)
                 - np.maximum(cmp_start[:, None], slc_start[None, :]), 0, None) / CMP_BLOCK
    out = np.zeros((n_slc, n_cols), np.float32)
    out[:, :n_cmp] = ov.T
    return jnp.asarray(out, BF16)


def _chunk_row_placement(n_chunk, n_slc):
    per = ATT_K // SLC_BLOCK
    pm = np.zeros((n_chunk * MASK_ROWS, n_slc), np.float32)
    for c in range(n_chunk):
        for b in range(per):
            pm[MASK_ROWS * c + b, per * c + b] = 1.0
    return jnp.asarray(pm, BF16)


def _pool_weight(pool_w):
    ng, gc = len(POOL_WINDOWS), POOL_GROUP_CH
    w = jnp.zeros((ng * gc, ng * gc), F32)
    for gi in range(ng):
        w = w.at[gi * gc:(gi + 1) * gc, gi * gc:(gi + 1) * gc].set(pool_w[gi])
    return w.astype(BF16)


def _odd_mixer(x, mod, layer, w_in, pe_k, pe_v, w1_k, w2_k, w1_v, w2_v, pool_w, pool_scale, w_out,
               bias_tables, g, b):
    bsz, s, _ = x.shape
    assert s % ATT_Q == 0 and s // CMP_STRIDE == LANES and ATT_Q == ATT_K and MIX_ROWS % ATT_K == 0
    assert WINDOW == 2 * ATT_K and SLC_BLOCK * MASK_ROWS >= ATT_K and POOL_WINDOWS == (2, 4, 8, 16)
    w_std, w_t = _odd_in_weights(w_in)
    kx, kvc, u, qt, vx, gt = _odd_proj(x, mod, layer, w_std, w_t)
    kc, vct = _compress(kvc, *_compress_weights(pe_k, pe_v, w1_k, w1_v, w2_k, w2_v))
    tile, cmp_tab = bias_tables
    ovt = _overlap_matrix_t(s // CMP_STRIDE, s // SLC_BLOCK)
    pm = _chunk_row_placement(s // ATT_K, s // SLC_BLOCK)
    o_nsa = _attention(qt, gt, kc, vct, kx, vx, tile, cmp_tab, ovt, pm)
    return _odd_out(x, mod, layer, o_nsa, u, _pool_weight(pool_w), pool_scale,
                    w_out[:NSA_Q].astype(BF16), w_out[NSA_Q:].astype(BF16), g, b)


def kernel(x, c, ada_w, ada_b, ln_g, ln_b, ffn_w_gate, ffn_w_up, ffn_w_down, ev_w_in, ev_conv_a_w,
           ev_conv_a_b, ev_norm_a_g, ev_norm_a_b, ev_conv_b_w, ev_w_out, od_w_in, od_cmp_pe_k,
           od_cmp_pe_v, od_cmp_w1_k, od_cmp_w2_k, od_cmp_w1_v, od_cmp_w2_v, od_pool_w, od_pool_scale,
           od_w_out, rel_bias):
    bsz, s, d = x.shape
    depth = ada_w.shape[0]
    mod = _modulation(c, ada_w, ada_b).reshape(depth, bsz, 3, 3, d)
    bias_tables = _bias_tables(rel_bias)
    weights = tuple(w[0, 0].astype(BF16) for w in (ffn_w_gate, ffn_w_up, ffn_w_down))
    f32_weights = (ffn_w_gate, ffn_w_up, ffn_w_down)
    for layer in range(depth):
        j = layer // 2
        x, weights = _ffn(x, mod, layer, 0, weights, ln_g[layer, 0], ln_b[layer, 0],
                          nxt=(layer, 1) + f32_weights)
        if layer % 2 == 0:
            x = _even_mixer(x, mod, layer, ev_w_in[j].astype(BF16), _sublane_replicated(ev_conv_a_w[j]),
                            ev_conv_a_b[j], ev_norm_a_g[j], ev_norm_a_b[j], ev_conv_b_w[j],
                            ev_w_out[j].astype(BF16), ln_g[layer, 1], ln_b[layer, 1])
        else:
            x = _odd_mixer(x, mod, layer, od_w_in[j], od_cmp_pe_k[j], od_cmp_pe_v[j], od_cmp_w1_k[j],
                           od_cmp_w2_k[j], od_cmp_w1_v[j], od_cmp_w2_v[j], od_pool_w[j], od_pool_scale[j],
                           od_w_out[j], bias_tables, ln_g[layer, 1], ln_b[layer, 1])
        x, weights = _ffn(x, mod, layer, 2, weights, ln_g[layer, 2], ln_b[layer, 2],
                          nxt=(layer + 1, 0) + f32_weights if layer + 1 < depth else None)
    return x
```
